```python
import jax, jax.numpy as jnp
from jax import lax

D_MODEL = 1024
BATCH = 16
SEQ = 2048
DEPTH = 1

CHUNK = 64
SB_HEADS = 8
SB_HEAD_DIM = 64
SB_WIDTH = SB_HEADS * SB_HEAD_DIM
HG_HEADS = 4
HG_HEAD_DIM = 128
HG_WIDTH = HG_HEADS * HG_HEAD_DIM
MIX_WIDTH = SB_WIDTH + HG_WIDTH
IN_PROJ_WIDTH = 3 * SB_WIDTH + 4 * HG_WIDTH
Q_BLOCK = 128
RECUR_BLOCK = CHUNK // 2
N_EXPERTS = 32
TOP_K = 4
D_EXPERT = D_MODEL
SWIGLU_LIMIT = 7.0
SWIGLU_ALPHA = 1.702
EXPERT_BLOCK = 128
EPS = 1e-6

kernel_name = "hybrid_stickbreak_hgrn2_moe"


def rms_norm(x, g):
    xf = x.astype(jnp.float32)
    y = xf * lax.rsqrt(jnp.mean(xf * xf, axis=-1, keepdims=True) + EPS)
    return (y * g.astype(jnp.float32)).astype(x.dtype)


def stick_breaking_attention(q, k, v):
    seq = q.shape[2]
    scale = SB_HEAD_DIM ** -0.5
    outs = []
    for blk in range(seq // Q_BLOCK):
        lo, hi = blk * Q_BLOCK, (blk + 1) * Q_BLOCK
        z = jnp.einsum("bhqd,bhkd->bhqk", q[:, :, lo:hi], k[:, :, :hi]).astype(jnp.float32) * scale
        t_idx = lo + jnp.arange(Q_BLOCK)[:, None]
        s_idx = jnp.arange(hi)[None, :]
        strict = s_idx < t_idx
        log_1mb = jnp.where(strict, jax.nn.log_sigmoid(-z), 0.0)
        after = lax.cumsum(log_1mb, axis=3, reverse=True) - log_1mb
        w = jnp.where(strict, jnp.exp(jax.nn.log_sigmoid(z) + after), 0.0)
        outs.append(jnp.einsum("bhqk,bhkd->bhqd", w.astype(v.dtype), v[:, :, :hi]))
    return jnp.concatenate(outs, axis=2)


def hgrn2_recurrence(q, f_pre, i, lb):
    bsz, seq, nh, dk = q.shape
    nb = seq // RECUR_BLOCK
    qf = q.astype(jnp.float32)
    ff = f_pre.astype(jnp.float32)
    vf = i.astype(jnp.float32)
    lbf = lb.astype(jnp.float32)
    log_f = jnp.log(lbf + (1.0 - lbf) * jax.nn.sigmoid(ff))
    kf = (1.0 - lbf) * jax.nn.sigmoid(-ff)

    def to_blocks(a):
        return a.reshape(bsz, nb, RECUR_BLOCK, nh, a.shape[-1]).transpose(1, 0, 3, 2, 4)

    causal = jnp.tril(jnp.ones((RECUR_BLOCK, RECUR_BLOCK), jnp.float32))

    def step(state, inp):
        qc, kc, vc, gc = inp
        b = jnp.cumsum(gc, axis=2)
        q_e = qc * jnp.exp(b)
        k_e = kc * jnp.exp(-b)
        attn = jnp.einsum("bhtd,bhsd->bhts", q_e, k_e) * causal
        o = jnp.einsum("bhts,bhse->bhte", attn, vc) + jnp.einsum("bhtd,bhde->bhte", q_e, state)
        b_last = b[:, :, -1:, :]
        state = jnp.exp(b_last[:, :, 0, :])[..., None] * state + jnp.einsum(
            "bhsd,bhse->bhde", kc * jnp.exp(b_last - b), vc)
        return state, o

    s0 = jnp.zeros((bsz, nh, dk, vf.shape[-1]), jnp.float32)
    _, o = lax.scan(step, s0, (to_blocks(qf), to_blocks(kf), to_blocks(vf), to_blocks(log_f)))
    o = o.transpose(1, 0, 3, 2, 4).reshape(bsz, seq, nh, vf.shape[-1])
    return o.astype(q.dtype)


def moe_ffn(h, w_router, b_router, w_gate_up, b_gate_up, w_down, b_down):
    n_tok, d = h.shape
    logits = (h @ w_router + b_router).astype(jnp.float32)
    top_val, top_idx = lax.top_k(logits, TOP_K)
    gate = jax.nn.softmax(top_val, axis=-1)
    n_assign = n_tok * TOP_K
    flat_e = top_idx.reshape(-1).astype(jnp.int32)
    flat_tok = jnp.arange(n_assign, dtype=jnp.int32) // TOP_K
    flat_g = gate.reshape(-1)
    order = jnp.argsort(flat_e)
    e_sorted = flat_e[order]
    counts = jnp.zeros((N_EXPERTS,), jnp.int32).at[flat_e].add(1)
    starts = jnp.cumsum(counts) - counts
    padded = (counts + EXPERT_BLOCK - 1) // EXPERT_BLOCK * EXPERT_BLOCK
    pad_ends = jnp.cumsum(padded)
    pad_starts = pad_ends - padded
    dest = pad_starts[e_sorted] + (jnp.arange(n_assign, dtype=jnp.int32) - starts[e_sorted])
    n_blocks = (n_assign + EXPERT_BLOCK - 1) // EXPERT_BLOCK + N_EXPERTS
    n_rows = n_blocks * EXPERT_BLOCK
    row_tok = jnp.full((n_rows,), n_tok, jnp.int32).at[dest].set(flat_tok[order])
    row_g = jnp.zeros((n_rows,), jnp.float32).at[dest].set(flat_g[order])
    block_start = jnp.arange(n_blocks, dtype=jnp.int32) * EXPERT_BLOCK
    block_e = jnp.minimum(jnp.searchsorted(pad_ends, block_start, side="right"), N_EXPERTS - 1)
    h_pad = jnp.concatenate([h, jnp.zeros((1, d), h.dtype)], axis=0)
    xb = h_pad[row_tok].reshape(n_blocks, EXPERT_BLOCK, d)

    def expert_block(args):
        xblk, e = args
        gu = xblk @ w_gate_up[e] + b_gate_up[e]
        g_lin = jnp.minimum(gu[:, :D_EXPERT], SWIGLU_LIMIT)
        u_lin = jnp.clip(gu[:, D_EXPERT:], -SWIGLU_LIMIT, SWIGLU_LIMIT)
        act = g_lin * jax.nn.sigmoid(SWIGLU_ALPHA * g_lin) * (u_lin + 1.0)
        return act @ w_down[e] + b_down[e]

    yb = lax.map(expert_block, (xb, block_e)).reshape(n_rows, d)
    y = jnp.zeros((n_tok + 1, d), h.dtype).at[row_tok].add(yb * row_g[:, None].astype(h.dtype))
    return y[:n_tok]


def setup_inputs(seed: int = 0) -> dict:
    key = jax.random.key(seed)
    ks = jax.random.split(key, 16)
    nrm = jax.random.normal
    f32 = jnp.float32
    return {
        "x": nrm(ks[0], (BATCH, SEQ, D_MODEL), f32),
        "norm1_g": 1.0 + 0.02 * nrm(ks[1], (DEPTH, D_MODEL), f32),
        "w_in": nrm(ks[2], (DEPTH, D_MODEL, IN_PROJ_WIDTH), f32) * D_MODEL ** -0.5,
        "sb_q_norm_g": 1.0 + 0.02 * nrm(ks[3], (DEPTH, SB_HEAD_DIM), f32),
        "sb_k_norm_g": 1.0 + 0.02 * nrm(ks[4], (DEPTH, SB_HEAD_DIM), f32),
        "sb_out_norm_g": 1.0 + 0.02 * nrm(ks[5], (DEPTH, SB_WIDTH), f32),
        "hg_lb_logits": 0.1 * nrm(ks[6], (DEPTH + 1, HG_WIDTH), f32),
        "hg_out_norm_g": 1.0 + 0.02 * nrm(ks[7], (DEPTH, HG_WIDTH), f32),
        "w_out": nrm(ks[8], (DEPTH, MIX_WIDTH, D_MODEL), f32) * MIX_WIDTH ** -0.5,
        "norm2_g": 1.0 + 0.02 * nrm(ks[9], (DEPTH, D_MODEL), f32),
        "w_router": nrm(ks[10], (DEPTH, D_MODEL, N_EXPERTS), f32) * D_MODEL ** -0.5,
        "b_router": 0.01 * nrm(ks[11], (DEPTH, N_EXPERTS), f32),
        "w_gate_up": nrm(ks[12], (DEPTH, N_EXPERTS, D_MODEL, 2 * D_EXPERT), f32) * D_MODEL ** -0.5,
        "b_gate_up": 0.02 * nrm(ks[13], (DEPTH, N_EXPERTS, 2 * D_EXPERT), f32),
        "w_down": nrm(ks[14], (DEPTH, N_EXPERTS, D_EXPERT, D_MODEL), f32) * D_EXPERT ** -0.5,
        "b_down": 0.02 * nrm(ks[15], (DEPTH, N_EXPERTS, D_MODEL), f32),
    }


def reference(x, norm1_g, w_in, sb_q_norm_g, sb_k_norm_g, sb_out_norm_g, hg_lb_logits,
              hg_out_norm_g, w_out, norm2_g, w_router, b_router, w_gate_up, b_gate_up,
              w_down, b_down):
    bsz, seq, d = x.shape
    lb_all = jnp.cumsum(jax.nn.softmax(hg_lb_logits.astype(jnp.float32), axis=0), axis=0)
    for l in range(DEPTH):
        h = rms_norm(x, norm1_g[l])
        p = h @ w_in[l]
        sb_q, sb_k, sb_v, hg_q, hg_f, hg_i, hg_g = jnp.split(
            p, [SB_WIDTH, 2 * SB_WIDTH, 3 * SB_WIDTH, 3 * SB_WIDTH + HG_WIDTH,
                3 * SB_WIDTH + 2 * HG_WIDTH, 3 * SB_WIDTH + 3 * HG_WIDTH], axis=-1)
        heads = lambda a: a.reshape(bsz, seq, SB_HEADS, SB_HEAD_DIM).transpose(0, 2, 1, 3)
        q = rms_norm(heads(sb_q), sb_q_norm_g[l])
        k = rms_norm(heads(sb_k), sb_k_norm_g[l])
        a = stick_breaking_attention(q, k, heads(sb_v)).transpose(0, 2, 1, 3)
        a = rms_norm(a, sb_out_norm_g[l].reshape(SB_HEADS, SB_HEAD_DIM)).reshape(bsz, seq, SB_WIDTH)
        hh = lambda t: t.reshape(bsz, seq, HG_HEADS, HG_HEAD_DIM)
        lb = lb_all[l].reshape(HG_HEADS, HG_HEAD_DIM)
        r = hgrn2_recurrence(hh(hg_q), hh(hg_f), hh(hg_i), lb)
        r = rms_norm(r, hg_out_norm_g[l].reshape(HG_HEADS, HG_HEAD_DIM)).reshape(bsz, seq, HG_WIDTH)
        r = r * jax.nn.silu(hg_g)
        x = x + jnp.concatenate([a, r], axis=-1) @ w_out[l]
        h2 = rms_norm(x, norm2_g[l]).reshape(bsz * seq, d)
        x = x + moe_ffn(h2, w_router[l], b_router[l], w_gate_up[l], b_gate_up[l],
                        w_down[l], b_down[l]).reshape(bsz, seq, d)
    return x
```

```python
import functools

import jax
import jax.numpy as jnp
from jax import lax
from jax.experimental import pallas as pl
from jax.experimental.pallas import tpu as pltpu

F32 = jnp.float32
BF16 = jnp.bfloat16
I32 = jnp.int32
U32 = jnp.uint32

EPS = 1e-6
LANES = 128
SB_HEAD_DIM = 64
SB_WIDTH = 512
HG_HEAD_DIM = 128
HG_WIDTH = 512
HG_CHUNK = 32
N_EXPERTS = 32
TOP_K = 4
SWIGLU_LIMIT = 7.0
SWIGLU_ALPHA = 1.702
VMEM_LIMIT = 56 * 1024 * 1024

NT_DIMS = (((1,), (1,)), ((), ()))


def _cparams(*sem):
    return pltpu.CompilerParams(dimension_semantics=sem, vmem_limit_bytes=VMEM_LIMIT)


def _bf16_split(a, n):
    parts = []
    for _ in range(n - 1):
        p = a.astype(BF16)
        parts.append(p)
        a = a - p.astype(F32)
    parts.append(a.astype(BF16))
    return parts


def _sigmoid(a):
    return 1.0 / (1.0 + jnp.exp(-a))


def _inproj_kernel(x_ref, g1_ref, w_ref, gq_ref, gk_ref, lb_ref, bd_ref,
                   qkv_ref, hq_ref, hg_ref, hk_ref, hi_ref, sg_ref):
    x = x_ref[...]
    ms = jnp.mean(x * x, axis=-1, keepdims=True)
    h = (x * lax.rsqrt(ms + EPS) * g1_ref[...]).astype(BF16)
    bd = bd_ref[...]

    def seg(j):
        return jnp.dot(h, w_ref[:, j * 512:(j + 1) * 512], preferred_element_type=F32)

    def head_norm(a, g, scale):
        outs = []
        for c in range(2):
            ac = a[:, c * 256:(c + 1) * 256]
            ss = jnp.dot((ac * ac).astype(BF16), bd, preferred_element_type=F32)
            outs.append(ac * lax.rsqrt(ss * (1.0 / SB_HEAD_DIM) + EPS))
        y = jnp.concatenate(outs, axis=1) * g
        return y * scale if scale != 1.0 else y

    qkv_ref[:, 0:512] = head_norm(seg(0), gq_ref[...], SB_HEAD_DIM ** -0.5).astype(BF16)
    qkv_ref[:, 512:1024] = head_norm(seg(1), gk_ref[...], 1.0).astype(BF16)
    qkv_ref[:, 1024:1536] = seg(2).astype(BF16)
    hq_ref[...] = seg(3).astype(BF16)
    f = seg(4)
    lb = lb_ref[...]
    hg_ref[...] = jnp.log(lb + (1.0 - lb) * _sigmoid(f))
    hk_ref[...] = (1.0 - lb) * _sigmoid(-f)
    hi_ref[...] = seg(5).astype(BF16)
    gate = seg(6)
    sg_ref[...] = (gate * _sigmoid(gate)).astype(BF16)


def _inproj(x2, g1, w_bf, gq, gk, lb, bd, *, tm):
    t, d = x2.shape
    n = w_bf.shape[1]
    row = lambda i: (i, 0)
    fix = lambda i: (0, 0)
    return pl.pallas_call(
        _inproj_kernel,
        grid=(t // tm,),
        in_specs=[
            pl.BlockSpec((tm, d), row),
            pl.BlockSpec((1, d), fix),
            pl.BlockSpec((d, n), fix),
            pl.BlockSpec((1, 512), fix),
            pl.BlockSpec((1, 512), fix),
            pl.BlockSpec((1, 512), fix),
            pl.BlockSpec((256, 256), fix),
        ],
        out_specs=[
            pl.BlockSpec((tm, 1536), row),
            pl.BlockSpec((tm, 512), row),
            pl.BlockSpec((tm, 512), row),
            pl.BlockSpec((tm, 512), row),
            pl.BlockSpec((tm, 512), row),
            pl.BlockSpec((tm, 512), row),
        ],
        out_shape=[
            jax.ShapeDtypeStruct((t, 1536), BF16),
            jax.ShapeDtypeStruct((t, 512), BF16),
            jax.ShapeDtypeStruct((t, 512), F32),
            jax.ShapeDtypeStruct((t, 512), F32),
            jax.ShapeDtypeStruct((t, 512), BF16),
            jax.ShapeDtypeStruct((t, 512), BF16),
        ],
        compiler_params=_cparams("parallel"),
        name="inproj",
    )(x2, g1, w_bf, gq, gk, lb, bd)


def _sbattn_kernel(q_ref, k_ref, v_ref, go_ref, ur_ref, bd_ref, o_ref, acc_ref, carry_ref, *, tq):
    qi = pl.program_id(2)
    lane = lax.broadcasted_iota(I32, (tq, LANES), 1)
    lo = lane < SB_HEAD_DIM
    q = q_ref[...]
    zq = jnp.zeros_like(q)
    q2 = jnp.concatenate([jnp.where(lo, q, zq), jnp.where(lo, zq, q)], axis=0)
    ur = ur_ref[...]
    acc_ref[...] = jnp.zeros_like(acc_ref)
    carry_ref[...] = jnp.zeros_like(carry_ref)
    row = lax.broadcasted_iota(I32, (2 * tq, tq), 0)
    col = lax.broadcasted_iota(I32, (2 * tq, tq), 1)
    strict = col < jnp.where(row >= tq, row - tq, row)

    def block(kb, masked):
        off = pl.multiple_of(kb * tq, tq)
        kblk = k_ref[pl.ds(off, tq), :]
        vblk = v_ref[pl.ds(off, tq), :]
        z = lax.dot_general(q2, kblk, NT_DIMS, preferred_element_type=F32)
        sp = jnp.maximum(z, 0.0) + jnp.log(1.0 + jnp.exp(-jnp.abs(z)))
        lg = -sp
        if masked:
            lg = jnp.where(strict, lg, 0.0)
        l_hi, l_lo = _bf16_split(lg, 2)
        res = (jnp.dot(l_hi, ur, preferred_element_type=F32)
               + jnp.dot(l_lo, ur, preferred_element_type=F32))
        within = res[:, :tq]
        rowsum = res[:, tq:]
        w = jnp.exp((z - sp) + carry_ref[...] + within)
        if masked:
            w = jnp.where(strict, w, 0.0)
        carry_ref[...] += rowsum
        wcat = jnp.concatenate([w[:tq], w[tq:]], axis=1).astype(BF16)
        zv = jnp.zeros_like(vblk)
        vcat = jnp.concatenate([jnp.where(lo, vblk, zv), jnp.where(lo, zv, vblk)], axis=0)
        acc_ref[...] += jnp.dot(wcat, vcat, preferred_element_type=F32)

    block(qi, True)

    def body(it, c):
        block(qi - 1 - it, False)
        return c

    lax.fori_loop(0, qi, body, 0)

    a = acc_ref[...]
    ss = jnp.dot((a * a).astype(BF16), bd_ref[...], preferred_element_type=F32)
    o_ref[...] = (a * lax.rsqrt(ss * (1.0 / SB_HEAD_DIM) + EPS) * go_ref[...]).astype(BF16)


def _sbattn(qkv, go, ur, bd, *, tq):
    b, s, _ = qkv.shape
    npair = SB_WIDTH // LANES
    return pl.pallas_call(
        functools.partial(_sbattn_kernel, tq=tq),
        grid=(b, npair, s // tq),
        in_specs=[
            pl.BlockSpec((None, tq, LANES), lambda bi, hp, qi: (bi, qi, hp)),
            pl.BlockSpec((None, s, LANES), lambda bi, hp, qi: (bi, 0, npair + hp)),
            pl.BlockSpec((None, s, LANES), lambda bi, hp, qi: (bi, 0, 2 * npair + hp)),
            pl.BlockSpec((1, LANES), lambda bi, hp, qi: (0, hp)),
            pl.BlockSpec((tq, tq + LANES), lambda bi, hp, qi: (0, 0)),
            pl.BlockSpec((LANES, LANES), lambda bi, hp, qi: (0, 0)),
        ],
        out_specs=pl.BlockSpec((None, tq, LANES), lambda bi, hp, qi: (bi, qi, hp)),
        out_shape=jax.ShapeDtypeStruct((b, s, SB_WIDTH), BF16),
        scratch_shapes=[pltpu.VMEM((tq, LANES), F32), pltpu.VMEM((2 * tq, LANES), F32)],
        compiler_params=_cparams("parallel", "parallel", "parallel"),
        name="sbattn",
    )(qkv, qkv, qkv, go, ur, bd)


def _hgrn_kernel(q_ref, g_ref, k_ref, v_ref, sg_ref, gn_ref, tl_ref, o_ref, state_ref, stack_ref):
    s = q_ref.shape[0]
    grp = LANES
    nch = grp // HG_CHUNK
    state_ref[...] = jnp.zeros_like(state_ref)
    tl = tl_ref[...]
    rr = lax.broadcasted_iota(I32, (grp, grp), 0)
    cc = lax.broadcasted_iota(I32, (grp, grp), 1)
    shift = HG_CHUNK.bit_length() - 1
    rchunk = rr >> shift
    causal = (rchunk == (cc >> shift)) & (cc <= rr)
    gn = gn_ref[...]

    def group(r, c):
        off = pl.multiple_of(r * grp, grp)
        g = g_ref[pl.ds(off, grp), :]
        kf = k_ref[pl.ds(off, grp), :]
        qf = q_ref[pl.ds(off, grp), :].astype(F32)
        v = v_ref[pl.ds(off, grp), :]
        cs = sum(jnp.dot(tl, p, preferred_element_type=F32) for p in _bf16_split(g, 3))
        b = cs[:grp]
        bl = cs[grp:]
        qe = (qf * jnp.exp(b)).astype(BF16)
        ke = (kf * jnp.exp(-b)).astype(BF16)
        kd = kf * jnp.exp(bl - b)
        attn = lax.dot_general(qe, ke, NT_DIMS, preferred_element_type=F32)
        attn = jnp.where(causal, attn, 0.0).astype(BF16)
        o = jnp.dot(attn, v, preferred_element_type=F32)
        kdt = kd.T.astype(BF16)
        dec = jnp.exp(bl).T
        zv = jnp.zeros_like(v)
        st = state_ref[...]
        qparts = []
        for ci in range(nch):
            stack_ref[ci * grp:(ci + 1) * grp, :] = st.astype(BF16)
            inchunk = rchunk == ci
            upd = jnp.dot(kdt, jnp.where(inchunk, v, zv), preferred_element_type=F32)
            dcol = dec[:, ci * HG_CHUNK:ci * HG_CHUNK + 1]
            st = dcol * st + upd
            qparts.append(jnp.where(inchunk, qe, jnp.zeros_like(qe)))
        state_ref[...] = st
        o = o + jnp.dot(jnp.concatenate(qparts, axis=1), stack_ref[...], preferred_element_type=F32)
        ms = jnp.mean(o * o, axis=-1, keepdims=True)
        y = o * lax.rsqrt(ms + EPS) * gn
        o_ref[pl.ds(off, grp), :] = (y * sg_ref[pl.ds(off, grp), :].astype(F32)).astype(BF16)
        return c

    lax.fori_loop(0, s // grp, group, 0)


def _hgrn(hq, hg, hk, hi, sg, gn, tl):
    b, s, _ = hq.shape
    nh = HG_WIDTH // HG_HEAD_DIM
    blk = pl.BlockSpec((None, s, HG_HEAD_DIM), lambda bi, h: (bi, 0, h))
    return pl.pallas_call(
        _hgrn_kernel,
        grid=(b, nh),
        in_specs=[blk, blk, blk, blk, blk,
                  pl.BlockSpec((1, HG_HEAD_DIM), lambda bi, h: (0, h)),
                  pl.BlockSpec((2 * LANES, LANES), lambda bi, h: (0, 0))],
        out_specs=blk,
        out_shape=jax.ShapeDtypeStruct((b, s, HG_WIDTH), BF16),
        scratch_shapes=[pltpu.VMEM((HG_HEAD_DIM, HG_HEAD_DIM), F32),
                        pltpu.VMEM((4 * HG_HEAD_DIM, HG_HEAD_DIM), BF16)],
        compiler_params=_cparams("parallel", "parallel"),
        name="hgrn",
    )(hq, hg, hk, hi, sg, gn, tl)


def _outproj_kernel(x_ref, a_ref, r_ref, wo_ref, g2_ref, wr_ref, br_ref, tri_ref,
                    x1_ref, hp_ref, ti_ref, tg_ref, rk_ref, cnt_ref, run_ref):
    i = pl.program_id(0)

    @pl.when(i == 0)
    def _():
        run_ref[...] = jnp.zeros_like(run_ref)

    half = a_ref.shape[1]
    x1 = (x_ref[...]
          + jnp.dot(a_ref[...], wo_ref[:half, :], preferred_element_type=F32)
          + jnp.dot(r_ref[...], wo_ref[half:, :], preferred_element_type=F32))
    x1_ref[...] = x1
    ms = jnp.mean(x1 * x1, axis=-1, keepdims=True)
    h2 = x1 * lax.rsqrt(ms + EPS) * g2_ref[...]
    d2 = h2.shape[1] // 2
    lo_bits = pltpu.bitcast(h2[:, :d2].astype(BF16).astype(F32), U32)
    hi_bits = pltpu.bitcast(h2[:, d2:].astype(BF16).astype(F32), U32)
    hp_ref[...] = (hi_bits & jnp.uint32(0xFFFF0000)) | (lo_bits >> 16)

    h_hi, h_lo = _bf16_split(h2, 2)
    w_hi, w_lo = _bf16_split(wr_ref[...], 2)
    logits = (jnp.dot(h_hi, w_hi, preferred_element_type=F32)
              + jnp.dot(h_lo, w_hi, preferred_element_type=F32)
              + jnp.dot(h_hi, w_lo, preferred_element_type=F32)) + br_ref[...]
    tm = logits.shape[0]
    lane = lax.broadcasted_iota(I32, (tm, LANES), 1).astype(F32)
    neg = jnp.float32(-jnp.inf)
    ti = jnp.zeros((tm, LANES), F32)
    tv = jnp.zeros((tm, LANES), F32)
    onehot = jnp.zeros((tm, LANES), F32)
    sels = []
    work = logits
    for k in range(TOP_K):
        m = jnp.max(work, axis=-1, keepdims=True)
        idx = jnp.min(jnp.where(work == m, lane, float(LANES)), axis=-1, keepdims=True)
        sel = lane == idx
        sels.append(sel)
        ti = jnp.where(lane == k, idx, ti)
        tv = jnp.where(lane == k, m, tv)
        onehot = jnp.where(sel, 1.0, onehot)
        work = jnp.where(sel, neg, work)
    valid = lane < TOP_K
    e = jnp.where(valid, jnp.exp(tv - jnp.max(jnp.where(valid, tv, neg), axis=-1, keepdims=True)), 0.0)
    tg_ref[...] = e / jnp.sum(e, axis=-1, keepdims=True)
    ti_ref[...] = ti.astype(I32)

    before = jnp.dot(tri_ref[...], onehot.astype(BF16), preferred_element_type=F32) + run_ref[...]
    rk = jnp.zeros((tm, LANES), F32)
    for k in range(TOP_K):
        rk = jnp.where(lane == k, jnp.sum(jnp.where(sels[k], before, 0.0), axis=-1, keepdims=True), rk)
    rk_ref[...] = rk.astype(I32)
    run_ref[...] += jnp.sum(onehot, axis=0, keepdims=True)
    cnt_ref[...] = run_ref[...]


def _outproj(x2, a2, r2, wo_bf, g2, wr, br, tri, *, tm):
    t, d = x2.shape
    half = a2.shape[1]
    row = lambda i: (i, 0)
    fix = lambda i: (0, 0)
    return pl.pallas_call(
        _outproj_kernel,
        grid=(t // tm,),
        in_specs=[
            pl.BlockSpec((tm, d), row),
            pl.BlockSpec((tm, half), row),
            pl.BlockSpec((tm, half), row),
            pl.BlockSpec((2 * half, d), fix),
            pl.BlockSpec((1, d), fix),
            pl.BlockSpec((d, LANES), fix),
            pl.BlockSpec((1, LANES), fix),
            pl.BlockSpec((tm, tm), fix),
        ],
        out_specs=[
            pl.BlockSpec((tm, d), row),
            pl.BlockSpec((tm, d // 2), row),
            pl.BlockSpec((tm, LANES), row),
            pl.BlockSpec((tm, LANES), row),
            pl.BlockSpec((tm, LANES), row),
            pl.BlockSpec((1, LANES), fix),
        ],
        out_shape=[
            jax.ShapeDtypeStruct((t, d), F32),
            jax.ShapeDtypeStruct((t, d // 2), U32),
            jax.ShapeDtypeStruct((t, LANES), I32),
            jax.ShapeDtypeStruct((t, LANES), F32),
            jax.ShapeDtypeStruct((t, LANES), I32),
            jax.ShapeDtypeStruct((1, LANES), F32),
        ],
        scratch_shapes=[pltpu.VMEM((1, LANES), F32)],
        compiler_params=_cparams("arbitrary"),
        name="outproj",
    )(x2, a2, r2, wo_bf, g2, wr, br, tri)


def _row_copy_wait(src, dst, sem, n):
    pltpu.make_async_copy(src.at[pl.ds(0, n)], dst.at[pl.ds(0, n)], sem).wait()


def _scatter_kernel(dest_ref, hp_ref, xz_ref, xb_ref, sem):
    del xz_ref
    tm = hp_ref.shape[0]

    def body(t, c):
        for k in range(TOP_K):
            r = dest_ref[0, t * TOP_K + k]
            pltpu.make_async_copy(hp_ref.at[pl.ds(t, 1)], xb_ref.at[pl.ds(r, 1)], sem).start()
        return c

    lax.fori_loop(0, tm, body, 0)
    for _ in range(TOP_K):
        _row_copy_wait(hp_ref, xb_ref, sem, tm)


def _scatter(dest2, hp, xzero, *, tm):
    t, w = hp.shape
    return pl.pallas_call(
        _scatter_kernel,
        grid=(t // tm,),
        in_specs=[
            pl.BlockSpec((None, 1, tm * TOP_K), lambda i: (i, 0, 0), memory_space=pltpu.SMEM),
            pl.BlockSpec((tm, w), lambda i: (i, 0)),
            pl.BlockSpec(memory_space=pl.ANY),
        ],
        out_specs=pl.BlockSpec(memory_space=pl.ANY),
        out_shape=jax.ShapeDtypeStruct(xzero.shape, xzero.dtype),
        scratch_shapes=[pltpu.SemaphoreType.DMA],
        input_output_aliases={2: 0},
        compiler_params=_cparams("arbitrary"),
        name="scatter",
    )(dest2, hp, xzero)


def _experts_kernel(be_ref, nu_ref, xb_ref, wgu_ref, bgu_ref, wd_ref, bd_ref, yb_ref, wgu_bf, wd_bf):
    i = pl.program_id(0)
    fresh = jnp.logical_or(i == 0, be_ref[i] != be_ref[jnp.maximum(i - 1, 0)])

    @pl.when(fresh)
    def _():
        wgu_bf[...] = wgu_ref[...].astype(BF16)
        wd_bf[...] = wd_ref[...].astype(BF16)

    @pl.when(i < nu_ref[0])
    def _():
        xp = xb_ref[...]
        x_lo = pltpu.bitcast(xp << 16, F32).astype(BF16)
        x_hi = pltpu.bitcast(xp & jnp.uint32(0xFFFF0000), F32).astype(BF16)
        x = jnp.concatenate([x_lo, x_hi], axis=1)
        gu = jnp.dot(x, wgu_bf[...], preferred_element_type=F32) + bgu_ref[...]
        f = gu.shape[1] // 2
        g_lin = jnp.minimum(gu[:, :f], SWIGLU_LIMIT)
        u_lin = jnp.clip(gu[:, f:], -SWIGLU_LIMIT, SWIGLU_LIMIT)
        act = g_lin * _sigmoid(SWIGLU_ALPHA * g_lin) * (u_lin + 1.0)
        yb_ref[...] = jnp.dot(act.astype(BF16), wd_bf[...], preferred_element_type=F32) + bd_ref[...]

    @pl.when(i >= nu_ref[0])
    def _():
        yb_ref[...] = jnp.zeros_like(yb_ref)


def _experts(block_e, n_used, xb, w_gate_up, b_gate_up3, w_down, b_down3, *, rb):
    n_rows, w = xb.shape
    d = 2 * w
    f2 = w_gate_up.shape[2]
    f = w_down.shape[1]
    nblk = n_rows // rb
    rowblk = lambda i, be, nu: (jnp.minimum(i, nu[0] - 1), 0)
    wsel = lambda i, be, nu: (be[i], 0, 0)
    grid_spec = pltpu.PrefetchScalarGridSpec(
        num_scalar_prefetch=2,
        grid=(nblk,),
        in_specs=[
            pl.BlockSpec((rb, w), rowblk),
            pl.BlockSpec((None, d, f2), wsel),
            pl.BlockSpec((None, 1, f2), wsel),
            pl.BlockSpec((None, f, d), wsel),
            pl.BlockSpec((None, 1, d), wsel),
        ],
        out_specs=pl.BlockSpec((rb, d), lambda i, be, nu: (i, 0)),
        scratch_shapes=[pltpu.VMEM((d, f2), BF16), pltpu.VMEM((f, d), BF16)],
    )
    return pl.pallas_call(
        _experts_kernel,
        grid_spec=grid_spec,
        out_shape=jax.ShapeDtypeStruct((n_rows, d), F32),
        compiler_params=_cparams("arbitrary"),
        name="experts",
    )(block_e, n_used, xb, w_gate_up, b_gate_up3, w_down, b_down3)


def _combine_kernel(dest_ref, x1_ref, tg_ref, yb_ref, o_ref, buf, sem):
    tm = x1_ref.shape[0]

    def body(t, c):
        for k in range(TOP_K):
            r = dest_ref[0, t * TOP_K + k]
            pltpu.make_async_copy(yb_ref.at[pl.ds(r, 1)], buf.at[k, pl.ds(t, 1)], sem).start()
        return c

    lax.fori_loop(0, tm, body, 0)
    for k in range(TOP_K):
        _row_copy_wait(yb_ref, buf.at[k], sem, tm)
    tg = tg_ref[...]
    out = x1_ref[...]
    for k in range(TOP_K):
        out = out + tg[:, k:k + 1] * buf[k]
    o_ref[...] = out


def _combine(dest2, x1, tg, yb, *, tm):
    t, d = x1.shape
    return pl.pallas_call(
        _combine_kernel,
        grid=(t // tm,),
        in_specs=[
            pl.BlockSpec((None, 1, tm * TOP_K), lambda i: (i, 0, 0), memory_space=pltpu.SMEM),
            pl.BlockSpec((tm, d), lambda i: (i, 0)),
            pl.BlockSpec((tm, LANES), lambda i: (i, 0)),
            pl.BlockSpec(memory_space=pl.ANY),
        ],
        out_specs=pl.BlockSpec((tm, d), lambda i: (i, 0)),
        out_shape=jax.ShapeDtypeStruct((t, d), F32),
        scratch_shapes=[pltpu.VMEM((TOP_K, tm, d), F32), pltpu.SemaphoreType.DMA],
        compiler_params=_cparams("arbitrary"),
        name="combine",
    )(dest2, x1, tg, yb)


def _block_ones(n, blk):
    i = jnp.arange(n)
    return (i[:, None] // blk == i[None, :] // blk).astype(BF16)


def _suffix_and_ones(tq):
    j = jnp.arange(tq)
    u = (j[:, None] > j[None, :]).astype(BF16)
    return jnp.concatenate([u, jnp.ones((tq, LANES), BF16)], axis=1)


def _chunk_tril_and_ones(n, blk):
    i = jnp.arange(n)
    same = i[:, None] // blk == i[None, :] // blk
    tril = same & (i[None, :] <= i[:, None])
    return jnp.concatenate([tril.astype(BF16), same.astype(BF16)], axis=0)


def _strict_tril(n):
    i = jnp.arange(n)
    return (i[None, :] < i[:, None]).astype(BF16)


def _layer(x, norm1_g, w_in, q_g, k_g, sbo_g, lb, hgo_g, w_out, norm2_g, w_router, b_router,
           w_gate_up, b_gate_up, w_down, b_down, *, tm_in, tq, tm_out, tm_row, rb):
    bsz, seq, d = x.shape
    t = bsz * seq
    x2 = x.reshape(t, d)
    nsb = SB_WIDTH // SB_HEAD_DIM

    qkv, hq, hg, hk, hi, sg = _inproj(
        x2, norm1_g.reshape(1, d), w_in.astype(BF16),
        jnp.tile(q_g, nsb).reshape(1, SB_WIDTH), jnp.tile(k_g, nsb).reshape(1, SB_WIDTH),
        lb.reshape(1, HG_WIDTH), _block_ones(256, SB_HEAD_DIM), tm=tm_in)

    a = _sbattn(qkv.reshape(bsz, seq, 3 * SB_WIDTH), sbo_g.reshape(1, SB_WIDTH),
                _suffix_and_ones(tq), _block_ones(LANES, SB_HEAD_DIM), tq=tq)
    sh = (bsz, seq, HG_WIDTH)
    r = _hgrn(hq.reshape(sh), hg.reshape(sh), hk.reshape(sh), hi.reshape(sh), sg.reshape(sh),
              hgo_g.reshape(1, HG_WIDTH), _chunk_tril_and_ones(LANES, HG_CHUNK))

    wr = jnp.zeros((d, LANES), F32).at[:, :N_EXPERTS].set(w_router)
    br = jnp.full((1, LANES), -1e30, F32).at[0, :N_EXPERTS].set(b_router)
    x1, hp, ti, tg, rk, cnt = _outproj(
        x2, a.reshape(t, SB_WIDTH), r.reshape(t, HG_WIDTH), w_out.astype(BF16),
        norm2_g.reshape(1, d), wr, br, _strict_tril(tm_out), tm=tm_out)

    counts = cnt[0, :N_EXPERTS].astype(I32)
    padded = (counts + rb - 1) // rb * rb
    pad_ends = jnp.cumsum(padded)
    pad_starts = pad_ends - padded
    n_assign = t * TOP_K
    nblk = n_assign // rb + N_EXPERTS
    dest = pad_starts[ti[:, :TOP_K]] + rk[:, :TOP_K]
    dest2 = dest.reshape(t // tm_row, 1, tm_row * TOP_K)
    n_used = (pad_ends[-1] // rb).astype(I32).reshape(1)
    blk_start = jnp.arange(nblk, dtype=I32) * rb
    block_e = jnp.minimum(jnp.searchsorted(pad_ends, blk_start, side="right"), N_EXPERTS - 1).astype(I32)
    block_e = jnp.where(jnp.arange(nblk) < n_used[0], block_e, block_e[jnp.maximum(n_used[0] - 1, 0)])

    xb = _scatter(dest2, hp, jnp.zeros((nblk * rb, d // 2), U32), tm=tm_row)
    yb = _experts(block_e, n_used, xb, w_gate_up, b_gate_up.reshape(N_EXPERTS, 1, -1),
                  w_down, b_down.reshape(N_EXPERTS, 1, -1), rb=rb)
    out = _combine(dest2, x1, tg, yb, tm=tm_row)
    return out.reshape(bsz, seq, d)


def kernel(x, norm1_g, w_in, sb_q_norm_g, sb_k_norm_g, sb_out_norm_g, hg_lb_logits, hg_out_norm_g,
           w_out, norm2_g, w_router, b_router, w_gate_up, b_gate_up, w_down, b_down):
    depth = w_in.shape[0]
    lb_all = jnp.cumsum(jax.nn.softmax(hg_lb_logits.astype(F32), axis=0), axis=0)
    for l in range(depth):
        x = _layer(x, norm1_g[l], w_in[l], sb_q_norm_g[l], sb_k_norm_g[l], sb_out_norm_g[l], lb_all[l],
                   hg_out_norm_g[l], w_out[l], norm2_g[l], w_router[l], b_router[l],
                   w_gate_up[l], b_gate_up[l], w_down[l], b_down[l],
                   tm_in=256, tq=128, tm_out=256, tm_row=256, rb=256)
    return x
```

```python
import functools

import jax
import jax.numpy as jnp
from jax import lax
from jax.experimental import pallas as pl
from jax.experimental.pallas import tpu as pltpu

F32 = jnp.float32
BF16 = jnp.bfloat16
I32 = jnp.int32
U32 = jnp.uint32

EPS = 1e-6
LANES = 128
SB_HEAD_DIM = 64
SB_WIDTH = 512
HG_HEAD_DIM = 128
HG_WIDTH = 512
HG_CHUNK = 32
N_EXPERTS = 32
TOP_K = 4
SWIGLU_LIMIT = 7.0
SWIGLU_ALPHA = 1.702
VMEM_LIMIT = 56 * 1024 * 1024

NT_DIMS = (((1,), (1,)), ((), ()))
NPAIR = SB_WIDTH // LANES
LOG2E = 1.4426950408889634


def _cparams(*sem):
    return pltpu.CompilerParams(dimension_semantics=sem, vmem_limit_bytes=VMEM_LIMIT)


def _bf16_split(a, n):
    parts = []
    for _ in range(n - 1):
        p = a.astype(BF16)
        parts.append(p)
        a = a - p.astype(F32)
    parts.append(a.astype(BF16))
    return parts


def _sigmoid(a):
    return 1.0 / (1.0 + jnp.exp(-a))


def _inproj_kernel(x_ref, g1_ref, w_ref, gq_ref, gk_ref, lb_ref, bd_ref,
                   qkv_ref, hq_ref, hg_ref, hk_ref, hi_ref, sg_ref):
    x = x_ref[...]
    ms = jnp.mean(x * x, axis=-1, keepdims=True)
    h = (x * lax.rsqrt(ms + EPS) * g1_ref[...]).astype(BF16)
    bd = bd_ref[...]

    def seg(j):
        return jnp.dot(h, w_ref[:, j * 512:(j + 1) * 512], preferred_element_type=F32)

    def head_norm(a, g, scale):
        outs = []
        for c in range(2):
            ac = a[:, c * 256:(c + 1) * 256]
            ss = jnp.dot((ac * ac).astype(BF16), bd, preferred_element_type=F32)
            outs.append(ac * lax.rsqrt(ss * (1.0 / SB_HEAD_DIM) + EPS))
        y = jnp.concatenate(outs, axis=1) * g
        return y * scale if scale != 1.0 else y

    qkv_ref[:, 0:512] = head_norm(seg(0), gq_ref[...], LOG2E * SB_HEAD_DIM ** -0.5).astype(BF16)
    qkv_ref[:, 512:1024] = head_norm(seg(1), gk_ref[...], 1.0).astype(BF16)
    qkv_ref[:, 1024:1536] = seg(2).astype(BF16)
    hq_ref[...] = seg(3).astype(BF16)
    f = seg(4)
    lb = lb_ref[...]
    hg_ref[...] = jnp.log(lb + (1.0 - lb) * _sigmoid(f))
    hk_ref[...] = (1.0 - lb) * _sigmoid(-f)
    hi_ref[...] = seg(5).astype(BF16)
    gate = seg(6)
    sg_ref[...] = (gate * _sigmoid(gate)).astype(BF16)


def _inproj(x2, g1, w_bf, gq, gk, lb, bd, *, tm):
    t, d = x2.shape
    n = w_bf.shape[1]
    row = lambda i: (i, 0)
    fix = lambda i: (0, 0)
    return pl.pallas_call(
        _inproj_kernel,
        grid=(t // tm,),
        in_specs=[
            pl.BlockSpec((tm, d), row),
            pl.BlockSpec((1, d), fix),
            pl.BlockSpec((d, n), fix),
            pl.BlockSpec((1, 512), fix),
            pl.BlockSpec((1, 512), fix),
            pl.BlockSpec((1, 512), fix),
            pl.BlockSpec((256, 256), fix),
        ],
        out_specs=[
            pl.BlockSpec((tm, 1536), row),
            pl.BlockSpec((tm, 512), row),
            pl.BlockSpec((tm, 512), row),
            pl.BlockSpec((tm, 512), row),
            pl.BlockSpec((tm, 512), row),
            pl.BlockSpec((tm, 512), row),
        ],
        out_shape=[
            jax.ShapeDtypeStruct((t, 1536), BF16),
            jax.ShapeDtypeStruct((t, 512), BF16),
            jax.ShapeDtypeStruct((t, 512), F32),
            jax.ShapeDtypeStruct((t, 512), F32),
            jax.ShapeDtypeStruct((t, 512), BF16),
            jax.ShapeDtypeStruct((t, 512), BF16),
        ],
        compiler_params=_cparams("parallel"),
        name="inproj",
    )(x2, g1, w_bf, gq, gk, lb, bd)


def _sbattn_kernel(q_ref, k_ref, v_ref, go_ref, u2_ref, bd_ref, o_ref, vm_ref, *scr, tq):
    acc_ref = scr[0:NPAIR]
    carry_ref = scr[NPAIR:2 * NPAIR]
    z_ref = scr[2 * NPAIR:3 * NPAIR]
    w_ref = scr[3 * NPAIR:4 * NPAIR]
    qi = pl.program_id(1)
    hps = range(NPAIR)
    lane = lax.broadcasted_iota(I32, (tq, LANES), 1)
    lo = lane < SB_HEAD_DIM

    @pl.when(qi == 0)
    def _():
        v = v_ref[...]
        lo_all = (lax.broadcasted_iota(I32, v.shape, 1) & (LANES - 1)) < SB_HEAD_DIM
        zv = jnp.zeros_like(v)
        vm_ref[0] = jnp.where(lo_all, v, zv)
        vm_ref[1] = jnp.where(lo_all, zv, v)

    u2 = u2_ref[...]
    for hp in hps:
        acc_ref[hp][...] = jnp.zeros_like(acc_ref[hp])
        carry_ref[hp][...] = jnp.zeros_like(carry_ref[hp])
    row = lax.broadcasted_iota(I32, (2 * tq, tq), 0)
    col = lax.broadcasted_iota(I32, (2 * tq, tq), 1)
    strict = col < jnp.where(row >= tq, row - tq, row)
    q2s = []
    for hp in hps:
        q = q_ref[:, hp * LANES:(hp + 1) * LANES]
        zq = jnp.zeros_like(q)
        q2s.append(jnp.concatenate([jnp.where(lo, q, zq), jnp.where(lo, zq, q)], axis=0))

    def scores(kb):
        off = pl.multiple_of(kb * tq, tq)
        return [lax.dot_general(q2s[hp], k_ref[pl.ds(off, tq), hp * LANES:(hp + 1) * LANES], NT_DIMS,
                                preferred_element_type=F32) for hp in hps]

    def weighted_values(kb):
        off = pl.multiple_of(kb * tq, tq)
        for hp in hps:
            vcat = jnp.concatenate([vm_ref[0, pl.ds(off, tq), hp * LANES:(hp + 1) * LANES],
                                    vm_ref[1, pl.ds(off, tq), hp * LANES:(hp + 1) * LANES]], axis=0)
            acc_ref[hp][...] += jnp.dot(w_ref[hp][...], vcat, preferred_element_type=F32)

    def weights(zs, masked):
        lgs = []
        for z in zs:
            nz = -z
            lg = jnp.minimum(nz, 0.0) - jnp.log2(1.0 + jnp.exp2(jnp.minimum(z, nz)))
            if masked:
                lg = jnp.where(strict, lg, 0.0)
            lgs.append(lg)
        withins = []
        for pr in range(NPAIR // 2):
            cat = jnp.concatenate([lgs[2 * pr], lgs[2 * pr + 1]], axis=1).astype(BF16)
            res = jnp.dot(cat, u2, preferred_element_type=F32)
            withins += [res[:, :tq], res[:, tq:]]
        ws = [jnp.exp2((zs[hp] + lgs[hp]) + (carry_ref[hp][...] + withins[hp])) for hp in hps]
        if masked:
            ws = [jnp.where(strict, w, 0.0) for w in ws]
        for hp in hps:
            carry_ref[hp][...] += jnp.sum(lgs[hp], axis=-1, keepdims=True)
        return [jnp.concatenate([w[:tq], w[tq:]], axis=1).astype(BF16) for w in ws]

    ws = weights(scores(qi), True)
    zn = scores(jnp.maximum(qi - 1, 0))
    for hp in hps:
        w_ref[hp][...] = ws[hp]
        z_ref[hp][...] = zn[hp]

    def body(it, c):
        kb = qi - 1 - it
        zs = [z_ref[hp][...] for hp in hps]
        zn = scores(jnp.maximum(kb - 1, 0))
        weighted_values(kb + 1)
        ws = weights(zs, False)
        for hp in hps:
            w_ref[hp][...] = ws[hp]
            z_ref[hp][...] = zn[hp]
        return c

    lax.fori_loop(0, qi, body, 0)
    weighted_values(0)
    for hp in hps:
        a = acc_ref[hp][...]
        ss = jnp.dot((a * a).astype(BF16), bd_ref[...], preferred_element_type=F32)
        o_ref[:, hp * LANES:(hp + 1) * LANES] = (
            a * lax.rsqrt(ss * (1.0 / SB_HEAD_DIM) + EPS) * go_ref[:, hp * LANES:(hp + 1) * LANES]).astype(BF16)


def _sbattn(qkv, go, u2, bd, *, tq):
    b, s, _ = qkv.shape
    return pl.pallas_call(
        functools.partial(_sbattn_kernel, tq=tq),
        grid=(b, s // tq),
        in_specs=[
            pl.BlockSpec((None, tq, SB_WIDTH), lambda bi, qi: (bi, qi, 0)),
            pl.BlockSpec((None, s, SB_WIDTH), lambda bi, qi: (bi, 0, 1)),
            pl.BlockSpec((None, s, SB_WIDTH), lambda bi, qi: (bi, 0, 2)),
            pl.BlockSpec((1, SB_WIDTH), lambda bi, qi: (0, 0)),
            pl.BlockSpec((2 * tq, 2 * tq), lambda bi, qi: (0, 0)),
            pl.BlockSpec((LANES, LANES), lambda bi, qi: (0, 0)),
        ],
        out_specs=pl.BlockSpec((None, tq, SB_WIDTH), lambda bi, qi: (bi, qi, 0)),
        out_shape=jax.ShapeDtypeStruct((b, s, SB_WIDTH), BF16),
        scratch_shapes=([pltpu.VMEM((2, s, SB_WIDTH), BF16)]
                        + [pltpu.VMEM((tq, LANES), F32)] * NPAIR
                        + [pltpu.VMEM((2 * tq, LANES), F32)] * NPAIR
                        + [pltpu.VMEM((2 * tq, tq), F32)] * NPAIR
                        + [pltpu.VMEM((tq, 2 * tq), BF16)] * NPAIR),
        compiler_params=_cparams("parallel", "arbitrary"),
        name="sbattn",
    )(qkv, qkv, qkv, go, u2, bd)


def _hgrn_kernel(q_ref, g_ref, k_ref, v_ref, sg_ref, gn_ref, tl_ref, o_ref, *state_ref):
    s = q_ref.shape[0]
    grp = LANES
    nch = grp // HG_CHUNK
    hs = range(HG_WIDTH // HG_HEAD_DIM)
    for h in hs:
        state_ref[h][...] = jnp.zeros_like(state_ref[h])
    tl = tl_ref[...]
    rr = lax.broadcasted_iota(I32, (grp, grp), 0)
    cc = lax.broadcasted_iota(I32, (grp, grp), 1)
    shift = HG_CHUNK.bit_length() - 1
    rchunk = rr >> shift
    causal = (rchunk == (cc >> shift)) & (cc <= rr)
    inchunk = [rchunk == ci for ci in range(nch)]

    def group(r, c):
        off = pl.multiple_of(r * grp, grp)
        col = lambda h: slice(h * HG_HEAD_DIM, (h + 1) * HG_HEAD_DIM)
        rows = pl.ds(off, grp)
        vs = [v_ref[rows, col(h)] for h in hs]
        bs = [sum(jnp.dot(tl, p, preferred_element_type=F32) for p in _bf16_split(g_ref[rows, col(h)], 2))
              for h in hs]
        lasts = [[b[(ci + 1) * HG_CHUNK - 1:(ci + 1) * HG_CHUNK, :] for ci in range(nch)] for b in bs]
        bls = [jnp.concatenate([jnp.broadcast_to(l, (HG_CHUNK, HG_HEAD_DIM)) for l in last], axis=0)
               for last in lasts]
        qes = [(q_ref[rows, col(h)].astype(F32) * jnp.exp(bs[h])).astype(BF16) for h in hs]
        kes = [(k_ref[rows, col(h)] * jnp.exp(-bs[h])).astype(BF16) for h in hs]
        kds = [(k_ref[rows, col(h)] * jnp.exp(bls[h] - bs[h])).astype(BF16) for h in hs]
        attns = [jnp.where(causal, lax.dot_general(qes[h], kes[h], NT_DIMS, preferred_element_type=F32), 0.0)
                 .astype(BF16) for h in hs]
        os = [jnp.dot(attns[h], vs[h], preferred_element_type=F32) for h in hs]
        vts = [vs[h].astype(F32).T.astype(BF16) for h in hs]
        uts = []
        for h in hs:
            zk = jnp.zeros_like(kds[h])
            kdx = jnp.concatenate([jnp.where(inchunk[ci], kds[h], zk) for ci in range(nch)], axis=1)
            uts.append(jnp.dot(vts[h], kdx, preferred_element_type=F32))
        for h in hs:
            st = state_ref[h][...]
            parts = []
            for ci in range(nch):
                parts.append(st.astype(BF16))
                st = st * jnp.exp(lasts[h][ci]) + uts[h][:, ci * HG_HEAD_DIM:(ci + 1) * HG_HEAD_DIM]
            state_ref[h][...] = st
            zq = jnp.zeros_like(qes[h])
            qx = jnp.concatenate([jnp.where(inchunk[ci], qes[h], zq) for ci in range(nch)], axis=1)
            o = os[h] + lax.dot_general(qx, jnp.concatenate(parts, axis=1), NT_DIMS, preferred_element_type=F32)
            ms = jnp.mean(o * o, axis=-1, keepdims=True)
            y = o * lax.rsqrt(ms + EPS) * gn_ref[:, col(h)]
            o_ref[rows, col(h)] = (y * sg_ref[rows, col(h)].astype(F32)).astype(BF16)
        return c

    lax.fori_loop(0, s // grp, group, 0)


def _hgrn(hq, hg, hk, hi, sg, gn, tl):
    b, s, w = hq.shape
    nh = HG_WIDTH // HG_HEAD_DIM
    blk = pl.BlockSpec((None, s, w), lambda bi: (bi, 0, 0))
    return pl.pallas_call(
        _hgrn_kernel,
        grid=(b,),
        in_specs=[blk, blk, blk, blk, blk,
                  pl.BlockSpec((1, w), lambda bi: (0, 0)),
                  pl.BlockSpec((LANES, LANES), lambda bi: (0, 0))],
        out_specs=blk,
        out_shape=jax.ShapeDtypeStruct((b, s, w), BF16),
        scratch_shapes=[pltpu.VMEM((HG_HEAD_DIM, HG_HEAD_DIM), F32)] * nh,
        compiler_params=_cparams("parallel"),
        name="hgrn",
    )(hq, hg, hk, hi, sg, gn, tl)


def _outproj_kernel(x_ref, a_ref, r_ref, wo_ref, g2_ref, wr_ref, br_ref, tri_ref,
                    x1_ref, hp_ref, ti_ref, tg_ref, rk_ref, cnt_ref, run_ref):
    i = pl.program_id(0)

    @pl.when(i == 0)
    def _():
        run_ref[...] = jnp.zeros_like(run_ref)

    half = a_ref.shape[1]
    x1 = (x_ref[...]
          + jnp.dot(a_ref[...], wo_ref[:half, :], preferred_element_type=F32)
          + jnp.dot(r_ref[...], wo_ref[half:, :], preferred_element_type=F32))
    x1_ref[...] = x1
    ms = jnp.mean(x1 * x1, axis=-1, keepdims=True)
    h2 = x1 * lax.rsqrt(ms + EPS) * g2_ref[...]
    d2 = h2.shape[1] // 2
    lo_bits = pltpu.bitcast(h2[:, :d2].astype(BF16).astype(F32), U32)
    hi_bits = pltpu.bitcast(h2[:, d2:].astype(BF16).astype(F32), U32)
    hp_ref[...] = (hi_bits & jnp.uint32(0xFFFF0000)) | (lo_bits >> 16)

    h_hi, h_lo = _bf16_split(h2, 2)
    w_hi, w_lo = _bf16_split(wr_ref[...], 2)
    logits = (jnp.dot(h_hi, w_hi, preferred_element_type=F32)
              + jnp.dot(h_lo, w_hi, preferred_element_type=F32)
              + jnp.dot(h_hi, w_lo, preferred_element_type=F32)) + br_ref[...]
    tm = logits.shape[0]
    lane = lax.broadcasted_iota(I32, (tm, LANES), 1).astype(F32)
    neg = jnp.float32(-jnp.inf)
    ti = jnp.zeros((tm, LANES), F32)
    tv = jnp.zeros((tm, LANES), F32)
    onehot = jnp.zeros((tm, LANES), F32)
    sels = []
    work = logits
    for k in range(TOP_K):
        m = jnp.max(work, axis=-1, keepdims=True)
        idx = jnp.min(jnp.where(work == m, lane, float(LANES)), axis=-1, keepdims=True)
        sel = lane == idx
        sels.append(sel)
        ti = jnp.where(lane == k, idx, ti)
        tv = jnp.where(lane == k, m, tv)
        onehot = jnp.where(sel, 1.0, onehot)
        work = jnp.where(sel, neg, work)
    valid = lane < TOP_K
    e = jnp.where(valid, jnp.exp(tv - jnp.max(jnp.where(valid, tv, neg), axis=-1, keepdims=True)), 0.0)
    tg_ref[...] = e / jnp.sum(e, axis=-1, keepdims=True)
    ti_ref[...] = ti.astype(I32)

    before = jnp.dot(tri_ref[...], onehot.astype(BF16), preferred_element_type=F32) + run_ref[...]
    rk = jnp.zeros((tm, LANES), F32)
    for k in range(TOP_K):
        rk = jnp.where(lane == k, jnp.sum(jnp.where(sels[k], before, 0.0), axis=-1, keepdims=True), rk)
    rk_ref[...] = rk.astype(I32)
    run_ref[...] += jnp.sum(onehot, axis=0, keepdims=True)
    cnt_ref[...] = run_ref[...]


def _outproj(x2, a2, r2, wo_bf, g2, wr, br, tri, *, tm):
    t, d = x2.shape
    half = a2.shape[1]
    row = lambda i: (i, 0)
    fix = lambda i: (0, 0)
    return pl.pallas_call(
        _outproj_kernel,
        grid=(t // tm,),
        in_specs=[
            pl.BlockSpec((tm, d), row),
            pl.BlockSpec((tm, half), row),
            pl.BlockSpec((tm, half), row),
            pl.BlockSpec((2 * half, d), fix),
            pl.BlockSpec((1, d), fix),
            pl.BlockSpec((d, LANES), fix),
            pl.BlockSpec((1, LANES), fix),
            pl.BlockSpec((tm, tm), fix),
        ],
        out_specs=[
            pl.BlockSpec((tm, d), row),
            pl.BlockSpec((tm, d // 2), row),
            pl.BlockSpec((tm, LANES), row),
            pl.BlockSpec((tm, LANES), row),
            pl.BlockSpec((tm, LANES), row),
            pl.BlockSpec((1, LANES), fix),
        ],
        out_shape=[
            jax.ShapeDtypeStruct((t, d), F32),
            jax.ShapeDtypeStruct((t, d // 2), U32),
            jax.ShapeDtypeStruct((t, LANES), I32),
            jax.ShapeDtypeStruct((t, LANES), F32),
            jax.ShapeDtypeStruct((t, LANES), I32),
            jax.ShapeDtypeStruct((1, LANES), F32),
        ],
        scratch_shapes=[pltpu.VMEM((1, LANES), F32)],
        compiler_params=_cparams("arbitrary"),
        name="outproj",
    )(x2, a2, r2, wo_bf, g2, wr, br, tri)


def _row_copy_wait(src, dst, sem, n):
    pltpu.make_async_copy(src.at[pl.ds(0, n)], dst.at[pl.ds(0, n)], sem).wait()


def _scatter_kernel(dest_ref, hp_ref, xz_ref, xb_ref, sem):
    del xz_ref
    tm = hp_ref.shape[0]

    def body(t, c):
        for k in range(TOP_K):
            r = dest_ref[0, t * TOP_K + k]
            pltpu.make_async_copy(hp_ref.at[pl.ds(t, 1)], xb_ref.at[pl.ds(r, 1)], sem).start()
        return c

    lax.fori_loop(0, tm, body, 0)
    for _ in range(TOP_K):
        _row_copy_wait(hp_ref, xb_ref, sem, tm)


def _scatter(dest2, hp, xzero, *, tm):
    t, w = hp.shape
    return pl.pallas_call(
        _scatter_kernel,
        grid=(t // tm,),
        in_specs=[
            pl.BlockSpec((None, 1, tm * TOP_K), lambda i: (i, 0, 0), memory_space=pltpu.SMEM),
            pl.BlockSpec((tm, w), lambda i: (i, 0)),
            pl.BlockSpec(memory_space=pl.ANY),
        ],
        out_specs=pl.BlockSpec(memory_space=pl.ANY),
        out_shape=jax.ShapeDtypeStruct(xzero.shape, xzero.dtype),
        scratch_shapes=[pltpu.SemaphoreType.DMA],
        input_output_aliases={2: 0},
        compiler_params=_cparams("arbitrary"),
        name="scatter",
    )(dest2, hp, xzero)


def _experts_kernel(be_ref, nu_ref, xb_ref, wgu_ref, bgu_ref, wd_ref, bd_ref, yb_ref, wgu_bf, wd_bf):
    i = pl.program_id(0)
    fresh = jnp.logical_or(i == 0, be_ref[i] != be_ref[jnp.maximum(i - 1, 0)])

    @pl.when(fresh)
    def _():
        wgu_bf[...] = wgu_ref[...].astype(BF16)
        wd_bf[...] = wd_ref[...].astype(BF16)

    @pl.when(i < nu_ref[0])
    def _():
        xp = xb_ref[...]
        x_lo = pltpu.bitcast(xp << 16, F32).astype(BF16)
        x_hi = pltpu.bitcast(xp & jnp.uint32(0xFFFF0000), F32).astype(BF16)
        x = jnp.concatenate([x_lo, x_hi], axis=1)
        gu = jnp.dot(x, wgu_bf[...], preferred_element_type=F32) + bgu_ref[...]
        f = gu.shape[1] // 2
        g_lin = jnp.minimum(gu[:, :f], SWIGLU_LIMIT)
        u_lin = jnp.clip(gu[:, f:], -SWIGLU_LIMIT, SWIGLU_LIMIT)
        act = g_lin * _sigmoid(SWIGLU_ALPHA * g_lin) * (u_lin + 1.0)
        yb_ref[...] = jnp.dot(act.astype(BF16), wd_bf[...], preferred_element_type=F32) + bd_ref[...]

    @pl.when(i >= nu_ref[0])
    def _():
        yb_ref[...] = jnp.zeros_like(yb_ref)


def _experts(block_e, n_used, xb, w_gate_up, b_gate_up3, w_down, b_down3, *, rb):
    n_rows, w = xb.shape
    d = 2 * w
    f2 = w_gate_up.shape[2]
    f = w_down.shape[1]
    nblk = n_rows // rb
    rowblk = lambda i, be, nu: (jnp.minimum(i, nu[0] - 1), 0)
    wsel = lambda i, be, nu: (be[i], 0, 0)
    grid_spec = pltpu.PrefetchScalarGridSpec(
        num_scalar_prefetch=2,
        grid=(nblk,),
        in_specs=[
            pl.BlockSpec((rb, w), rowblk),
            pl.BlockSpec((None, d, f2), wsel),
            pl.BlockSpec((None, 1, f2), wsel),
            pl.BlockSpec((None, f, d), wsel),
            pl.BlockSpec((None, 1, d), wsel),
        ],
        out_specs=pl.BlockSpec((rb, d), lambda i, be, nu: (i, 0)),
        scratch_shapes=[pltpu.VMEM((d, f2), BF16), pltpu.VMEM((f, d), BF16)],
    )
    return pl.pallas_call(
        _experts_kernel,
        grid_spec=grid_spec,
        out_shape=jax.ShapeDtypeStruct((n_rows, d), F32),
        compiler_params=_cparams("arbitrary"),
        name="experts",
    )(block_e, n_used, xb, w_gate_up, b_gate_up3, w_down, b_down3)


def _combine_kernel(dest_ref, x1_ref, tg_ref, yb_ref, o_ref, buf, sem):
    tm = x1_ref.shape[0]

    def body(t, c):
        for k in range(TOP_K):
            r = dest_ref[0, t * TOP_K + k]
            pltpu.make_async_copy(yb_ref.at[pl.ds(r, 1)], buf.at[k, pl.ds(t, 1)], sem).start()
        return c

    lax.fori_loop(0, tm, body, 0)
    for k in range(TOP_K):
        _row_copy_wait(yb_ref, buf.at[k], sem, tm)
    tg = tg_ref[...]
    out = x1_ref[...]
    for k in range(TOP_K):
        out = out + tg[:, k:k + 1] * buf[k]
    o_ref[...] = out


def _combine(dest2, x1, tg, yb, *, tm):
    t, d = x1.shape
    return pl.pallas_call(
        _combine_kernel,
        grid=(t // tm,),
        in_specs=[
            pl.BlockSpec((None, 1, tm * TOP_K), lambda i: (i, 0, 0), memory_space=pltpu.SMEM),
            pl.BlockSpec((tm, d), lambda i: (i, 0)),
            pl.BlockSpec((tm, LANES), lambda i: (i, 0)),
            pl.BlockSpec(memory_space=pl.ANY),
        ],
        out_specs=pl.BlockSpec((tm, d), lambda i: (i, 0)),
        out_shape=jax.ShapeDtypeStruct((t, d), F32),
        scratch_shapes=[pltpu.VMEM((TOP_K, tm, d), F32), pltpu.SemaphoreType.DMA],
        compiler_params=_cparams("arbitrary"),
        name="combine",
    )(dest2, x1, tg, yb)


def _block_ones(n, blk):
    i = jnp.arange(n)
    return (i[:, None] // blk == i[None, :] // blk).astype(BF16)


def _pair_suffix(tq):
    j = jnp.arange(2 * tq)
    return ((j[:, None] > j[None, :]) & (j[:, None] // tq == j[None, :] // tq)).astype(BF16)


def _chunk_tril(n, blk):
    i = jnp.arange(n)
    return ((i[:, None] // blk == i[None, :] // blk) & (i[None, :] <= i[:, None])).astype(BF16)


def _strict_tril(n):
    i = jnp.arange(n)
    return (i[None, :] < i[:, None]).astype(BF16)


def _layer(x, norm1_g, w_in, q_g, k_g, sbo_g, lb, hgo_g, w_out, norm2_g, w_router, b_router,
           w_gate_up, b_gate_up, w_down, b_down, *, tm_in, tq, tm_out, tm_row, rb):
    bsz, seq, d = x.shape
    t = bsz * seq
    x2 = x.reshape(t, d)
    nsb = SB_WIDTH // SB_HEAD_DIM

    qkv, hq, hg, hk, hi, sg = _inproj(
        x2, norm1_g.reshape(1, d), w_in.astype(BF16),
        jnp.tile(q_g, nsb).reshape(1, SB_WIDTH), jnp.tile(k_g, nsb).reshape(1, SB_WIDTH),
        lb.reshape(1, HG_WIDTH), _block_ones(256, SB_HEAD_DIM), tm=tm_in)

    a = _sbattn(qkv.reshape(bsz, seq, 3 * SB_WIDTH), sbo_g.reshape(1, SB_WIDTH),
                _pair_suffix(tq), _block_ones(LANES, SB_HEAD_DIM), tq=tq)
    sh = (bsz, seq, HG_WIDTH)
    r = _hgrn(hq.reshape(sh), hg.reshape(sh), hk.reshape(sh), hi.reshape(sh), sg.reshape(sh),
              hgo_g.reshape(1, HG_WIDTH), _chunk_tril(LANES, HG_CHUNK))

    wr = jnp.zeros((d, LANES), F32).at[:, :N_EXPERTS].set(w_router)
    br = jnp.full((1, LANES), -1e30, F32).at[0, :N_EXPERTS].set(b_router)
    x1, hp, ti, tg, rk, cnt = _outproj(
        x2, a.reshape(t, SB_WIDTH), r.reshape(t, HG_WIDTH), w_out.astype(BF16),
        norm2_g.reshape(1, d), wr, br, _strict_tril(tm_out), tm=tm_out)

    counts = cnt[0, :N_EXPERTS].astype(I32)
    padded = (counts + rb - 1) // rb * rb
    pad_ends = jnp.cumsum(padded)
    pad_starts = pad_ends - padded
    n_assign = t * TOP_K
    nblk = n_assign // rb + N_EXPERTS
    eids = jnp.arange(N_EXPERTS, dtype=I32)
    dest = rk[:, :TOP_K] + jnp.sum(jnp.where(ti[:, :TOP_K, None] == eids, pad_starts, 0), axis=-1)
    dest2 = dest.reshape(t // tm_row, 1, tm_row * TOP_K)
    n_used = (pad_ends[-1] // rb).astype(I32).reshape(1)
    blk_start = jnp.minimum(jnp.arange(nblk, dtype=I32), n_used[0] - 1) * rb
    block_e = jnp.sum((blk_start[:, None] >= pad_ends[None, :]).astype(I32), axis=1)

    xb = _scatter(dest2, hp, jnp.zeros((nblk * rb, d // 2), U32), tm=tm_row)
    yb = _experts(block_e, n_used, xb, w_gate_up, b_gate_up.reshape(N_EXPERTS, 1, -1),
                  w_down, b_down.reshape(N_EXPERTS, 1, -1), rb=rb)
    out = _combine(dest2, x1, tg, yb, tm=tm_row)
    return out.reshape(bsz, seq, d)


def kernel(x, norm1_g, w_in, sb_q_norm_g, sb_k_norm_g, sb_out_norm_g, hg_lb_logits, hg_out_norm_g,
           w_out, norm2_g, w_router, b_router, w_gate_up, b_gate_up, w_down, b_down):
    depth = w_in.shape[0]
    lb_all = jnp.cumsum(jax.nn.softmax(hg_lb_logits.astype(F32), axis=0), axis=0)
    for l in range(depth):
        x = _layer(x, norm1_g[l], w_in[l], sb_q_norm_g[l], sb_k_norm_g[l], sb_out_norm_g[l], lb_all[l],
                   hg_out_norm_g[l], w_out[l], norm2_g[l], w_router[l], b_router[l],
                   w_gate_up[l], b_gate_up[l], w_down[l], b_down[l],
                   tm_in=256, tq=128, tm_out=256, tm_row=256, rb=256)
    return x
```

```python
import functools

import jax
import jax.numpy as jnp
from jax import lax
from jax.experimental import pallas as pl
from jax.experimental.pallas import tpu as pltpu

F32 = jnp.float32
BF16 = jnp.bfloat16
I32 = jnp.int32
U32 = jnp.uint32

EPS = 1e-6
LANES = 128
SB_HEAD_DIM = 64
SB_WIDTH = 512
HG_HEAD_DIM = 128
HG_WIDTH = 512
HG_CHUNK = 32
N_EXPERTS = 32
TOP_K = 4
SWIGLU_LIMIT = 7.0
SWIGLU_ALPHA = 1.702
VMEM_LIMIT = 56 * 1024 * 1024

NT_DIMS = (((1,), (1,)), ((), ()))
NPAIR = SB_WIDTH // LANES
LOG2E = 1.4426950408889634


def _cparams(*sem):
    return pltpu.CompilerParams(dimension_semantics=sem, vmem_limit_bytes=VMEM_LIMIT)


def _bf16_split(a, n):
    parts = []
    for _ in range(n - 1):
        p = a.astype(BF16)
        parts.append(p)
        a = a - p.astype(F32)
    parts.append(a.astype(BF16))
    return parts


def _sigmoid(a):
    return 1.0 / (1.0 + jnp.exp(-a))


def _inproj_kernel(x_ref, g1_ref, w_ref, gq_ref, gk_ref, lb_ref, bd_ref,
                   qkv_ref, hq_ref, hg_ref, hk_ref, hi_ref, sg_ref):
    x = x_ref[...]
    ms = jnp.mean(x * x, axis=-1, keepdims=True)
    h = (x * lax.rsqrt(ms + EPS) * g1_ref[...]).astype(BF16)
    bd = bd_ref[...]

    def seg(j):
        return jnp.dot(h, w_ref[:, j * 512:(j + 1) * 512], preferred_element_type=F32)

    def head_norm(a, g, scale):
        outs = []
        for c in range(2):
            ac = a[:, c * 256:(c + 1) * 256]
            ss = jnp.dot((ac * ac).astype(BF16), bd, preferred_element_type=F32)
            outs.append(ac * lax.rsqrt(ss * (1.0 / SB_HEAD_DIM) + EPS))
        y = jnp.concatenate(outs, axis=1) * g
        return y * scale if scale != 1.0 else y

    qkv_ref[:, 0:512] = head_norm(seg(0), gq_ref[...], LOG2E * SB_HEAD_DIM ** -0.5).astype(BF16)
    qkv_ref[:, 512:1024] = head_norm(seg(1), gk_ref[...], 1.0).astype(BF16)
    qkv_ref[:, 1024:1536] = seg(2).astype(BF16)
    hq_ref[...] = seg(3).astype(BF16)
    f = seg(4)
    lb = lb_ref[...]
    hg_ref[...] = jnp.log(lb + (1.0 - lb) * _sigmoid(f))
    hk_ref[...] = (1.0 - lb) * _sigmoid(-f)
    hi_ref[...] = seg(5).astype(BF16)
    gate = seg(6)
    sg_ref[...] = (gate * _sigmoid(gate)).astype(BF16)


def _inproj(x2, g1, w_bf, gq, gk, lb, bd, *, tm):
    t, d = x2.shape
    n = w_bf.shape[1]
    row = lambda i: (i, 0)
    fix = lambda i: (0, 0)
    return pl.pallas_call(
        _inproj_kernel,
        grid=(t // tm,),
        in_specs=[
            pl.BlockSpec((tm, d), row),
            pl.BlockSpec((1, d), fix),
            pl.BlockSpec((d, n), fix),
            pl.BlockSpec((1, 512), fix),
            pl.BlockSpec((1, 512), fix),
            pl.BlockSpec((1, 512), fix),
            pl.BlockSpec((256, 256), fix),
        ],
        out_specs=[
            pl.BlockSpec((tm, 1536), row),
            pl.BlockSpec((tm, 512), row),
            pl.BlockSpec((tm, 512), row),
            pl.BlockSpec((tm, 512), row),
            pl.BlockSpec((tm, 512), row),
            pl.BlockSpec((tm, 512), row),
        ],
        out_shape=[
            jax.ShapeDtypeStruct((t, 1536), BF16),
            jax.ShapeDtypeStruct((t, 512), BF16),
            jax.ShapeDtypeStruct((t, 512), F32),
            jax.ShapeDtypeStruct((t, 512), F32),
            jax.ShapeDtypeStruct((t, 512), BF16),
            jax.ShapeDtypeStruct((t, 512), BF16),
        ],
        compiler_params=_cparams("parallel"),
        name="inproj",
    )(x2, g1, w_bf, gq, gk, lb, bd)


def _sbattn_kernel(q_ref, k_ref, v_ref, go_ref, u2_ref, bd_ref, o_ref, vm_ref, *scr, tq):
    acc_ref = scr[0:NPAIR]
    carry_ref = scr[NPAIR:2 * NPAIR]
    z_ref = scr[2 * NPAIR:3 * NPAIR]
    w_ref = scr[3 * NPAIR:4 * NPAIR]
    qi = pl.program_id(1)
    hps = range(NPAIR)
    lane = lax.broadcasted_iota(I32, (tq, LANES), 1)
    lo = lane < SB_HEAD_DIM

    @pl.when(qi == 0)
    def _():
        v = v_ref[...]
        lo_all = (lax.broadcasted_iota(I32, v.shape, 1) & (LANES - 1)) < SB_HEAD_DIM
        zv = jnp.zeros_like(v)
        vm_ref[0] = jnp.where(lo_all, v, zv)
        vm_ref[1] = jnp.where(lo_all, zv, v)

    u2 = u2_ref[...]
    for hp in hps:
        acc_ref[hp][...] = jnp.zeros_like(acc_ref[hp])
        carry_ref[hp][...] = jnp.zeros_like(carry_ref[hp])
    row = lax.broadcasted_iota(I32, (2 * tq, tq), 0)
    col = lax.broadcasted_iota(I32, (2 * tq, tq), 1)
    strict = col < jnp.where(row >= tq, row - tq, row)
    q2s = []
    for hp in hps:
        q = q_ref[:, hp * LANES:(hp + 1) * LANES]
        zq = jnp.zeros_like(q)
        q2s.append(jnp.concatenate([jnp.where(lo, q, zq), jnp.where(lo, zq, q)], axis=0))

    def scores(kb):
        off = pl.multiple_of(kb * tq, tq)
        return [lax.dot_general(q2s[hp], k_ref[pl.ds(off, tq), hp * LANES:(hp + 1) * LANES], NT_DIMS,
                                preferred_element_type=F32) for hp in hps]

    def weighted_values(kb):
        off = pl.multiple_of(kb * tq, tq)
        for hp in hps:
            vcat = jnp.concatenate([vm_ref[0, pl.ds(off, tq), hp * LANES:(hp + 1) * LANES],
                                    vm_ref[1, pl.ds(off, tq), hp * LANES:(hp + 1) * LANES]], axis=0)
            acc_ref[hp][...] += jnp.dot(w_ref[hp][...], vcat, preferred_element_type=F32)

    def weights(zs, masked):
        lgs = []
        for z in zs:
            nz = -z
            lg = jnp.minimum(nz, 0.0) - jnp.log2(1.0 + jnp.exp2(jnp.minimum(z, nz)))
            if masked:
                lg = jnp.where(strict, lg, 0.0)
            lgs.append(lg)
        withins = []
        for pr in range(NPAIR // 2):
            cat = jnp.concatenate([lgs[2 * pr], lgs[2 * pr + 1]], axis=1).astype(BF16)
            res = jnp.dot(cat, u2, preferred_element_type=F32)
            withins += [res[:, :tq], res[:, tq:]]
        ws = [jnp.exp2((zs[hp] + lgs[hp]) + (carry_ref[hp][...] + withins[hp])) for hp in hps]
        if masked:
            ws = [jnp.where(strict, w, 0.0) for w in ws]
        for hp in hps:
            carry_ref[hp][...] += jnp.sum(lgs[hp], axis=-1, keepdims=True)
        return [jnp.concatenate([w[:tq], w[tq:]], axis=1).astype(BF16) for w in ws]

    ws = weights(scores(qi), True)
    zn = scores(jnp.maximum(qi - 1, 0))
    for hp in hps:
        w_ref[hp][...] = ws[hp]
        z_ref[hp][...] = zn[hp]

    def body(it, c):
        kb = qi - 1 - it
        zs = [z_ref[hp][...] for hp in hps]
        zn = scores(jnp.maximum(kb - 1, 0))
        weighted_values(kb + 1)
        ws = weights(zs, False)
        for hp in hps:
            w_ref[hp][...] = ws[hp]
            z_ref[hp][...] = zn[hp]
        return c

    lax.fori_loop(0, qi, body, 0)
    weighted_values(0)
    for hp in hps:
        a = acc_ref[hp][...]
        ss = jnp.dot((a * a).astype(BF16), bd_ref[...], preferred_element_type=F32)
        o_ref[:, hp * LANES:(hp + 1) * LANES] = (
            a * lax.rsqrt(ss * (1.0 / SB_HEAD_DIM) + EPS) * go_ref[:, hp * LANES:(hp + 1) * LANES]).astype(BF16)


def _sbattn(qkv, go, u2, bd, *, tq):
    b, s, _ = qkv.shape
    return pl.pallas_call(
        functools.partial(_sbattn_kernel, tq=tq),
        grid=(b, s // tq),
        in_specs=[
            pl.BlockSpec((None, tq, SB_WIDTH), lambda bi, qi: (bi, qi, 0)),
            pl.BlockSpec((None, s, SB_WIDTH), lambda bi, qi: (bi, 0, 1)),
            pl.BlockSpec((None, s, SB_WIDTH), lambda bi, qi: (bi, 0, 2)),
            pl.BlockSpec((1, SB_WIDTH), lambda bi, qi: (0, 0)),
            pl.BlockSpec((2 * tq, 2 * tq), lambda bi, qi: (0, 0)),
            pl.BlockSpec((LANES, LANES), lambda bi, qi: (0, 0)),
        ],
        out_specs=pl.BlockSpec((None, tq, SB_WIDTH), lambda bi, qi: (bi, qi, 0)),
        out_shape=jax.ShapeDtypeStruct((b, s, SB_WIDTH), BF16),
        scratch_shapes=([pltpu.VMEM((2, s, SB_WIDTH), BF16)]
                        + [pltpu.VMEM((tq, LANES), F32)] * NPAIR
                        + [pltpu.VMEM((2 * tq, LANES), F32)] * NPAIR
                        + [pltpu.VMEM((2 * tq, tq), F32)] * NPAIR
                        + [pltpu.VMEM((tq, 2 * tq), BF16)] * NPAIR),
        compiler_params=_cparams("parallel", "arbitrary"),
        name="sbattn",
    )(qkv, qkv, qkv, go, u2, bd)


def _hgrn_kernel(q_ref, g_ref, k_ref, v_ref, sg_ref, gn_ref, tl_ref, o_ref, *state_ref):
    s = q_ref.shape[0]
    grp = LANES
    nch = grp // HG_CHUNK
    hs = range(HG_WIDTH // HG_HEAD_DIM)
    for h in hs:
        state_ref[h][...] = jnp.zeros_like(state_ref[h])
    tl = tl_ref[...]
    rr = lax.broadcasted_iota(I32, (grp, grp), 0)
    cc = lax.broadcasted_iota(I32, (grp, grp), 1)
    shift = HG_CHUNK.bit_length() - 1
    rchunk = rr >> shift
    causal = (rchunk == (cc >> shift)) & (cc <= rr)
    inchunk = [rchunk == ci for ci in range(nch)]

    def group(r, c):
        off = pl.multiple_of(r * grp, grp)
        col = lambda h: slice(h * HG_HEAD_DIM, (h + 1) * HG_HEAD_DIM)
        rows = pl.ds(off, grp)
        vs = [v_ref[rows, col(h)] for h in hs]
        bs = [sum(jnp.dot(tl, p, preferred_element_type=F32) for p in _bf16_split(g_ref[rows, col(h)], 2))
              for h in hs]
        lasts = [[b[(ci + 1) * HG_CHUNK - 1:(ci + 1) * HG_CHUNK, :] for ci in range(nch)] for b in bs]
        bls = [jnp.concatenate([jnp.broadcast_to(l, (HG_CHUNK, HG_HEAD_DIM)) for l in last], axis=0)
               for last in lasts]
        qes = [(q_ref[rows, col(h)].astype(F32) * jnp.exp(bs[h])).astype(BF16) for h in hs]
        kes = [(k_ref[rows, col(h)] * jnp.exp(-bs[h])).astype(BF16) for h in hs]
        kds = [(k_ref[rows, col(h)] * jnp.exp(bls[h] - bs[h])).astype(BF16) for h in hs]
        attns = [jnp.where(causal, lax.dot_general(qes[h], kes[h], NT_DIMS, preferred_element_type=F32), 0.0)
                 .astype(BF16) for h in hs]
        os = [jnp.dot(attns[h], vs[h], preferred_element_type=F32) for h in hs]
        vts = [vs[h].astype(F32).T.astype(BF16) for h in hs]
        uts = []
        for h in hs:
            zk = jnp.zeros_like(kds[h])
            kdx = jnp.concatenate([jnp.where(inchunk[ci], kds[h], zk) for ci in range(nch)], axis=1)
            uts.append(jnp.dot(vts[h], kdx, preferred_element_type=F32))
        for h in hs:
            st = state_ref[h][...]
            parts = []
            for ci in range(nch):
                parts.append(st.astype(BF16))
                st = st * jnp.exp(lasts[h][ci]) + uts[h][:, ci * HG_HEAD_DIM:(ci + 1) * HG_HEAD_DIM]
            state_ref[h][...] = st
            zq = jnp.zeros_like(qes[h])
            qx = jnp.concatenate([jnp.where(inchunk[ci], qes[h], zq) for ci in range(nch)], axis=1)
            o = os[h] + lax.dot_general(qx, jnp.concatenate(parts, axis=1), NT_DIMS, preferred_element_type=F32)
            ms = jnp.mean(o * o, axis=-1, keepdims=True)
            y = o * lax.rsqrt(ms + EPS) * gn_ref[:, col(h)]
            o_ref[rows, col(h)] = (y * sg_ref[rows, col(h)].astype(F32)).astype(BF16)
        return c

    lax.fori_loop(0, s // grp, group, 0)


def _hgrn(hq, hg, hk, hi, sg, gn, tl):
    b, s, w = hq.shape
    nh = HG_WIDTH // HG_HEAD_DIM
    blk = pl.BlockSpec((None, s, w), lambda bi: (bi, 0, 0))
    return pl.pallas_call(
        _hgrn_kernel,
        grid=(b,),
        in_specs=[blk, blk, blk, blk, blk,
                  pl.BlockSpec((1, w), lambda bi: (0, 0)),
                  pl.BlockSpec((LANES, LANES), lambda bi: (0, 0))],
        out_specs=blk,
        out_shape=jax.ShapeDtypeStruct((b, s, w), BF16),
        scratch_shapes=[pltpu.VMEM((HG_HEAD_DIM, HG_HEAD_DIM), F32)] * nh,
        compiler_params=_cparams("parallel"),
        name="hgrn",
    )(hq, hg, hk, hi, sg, gn, tl)


def _outproj_kernel(x_ref, a_ref, r_ref, wo_ref, g2_ref, wr_ref, br_ref, tri_ref,
                    x1_ref, hp_ref, ti_ref, tg_ref, rk_ref, cnt_ref, run_ref):
    i = pl.program_id(0)

    @pl.when(i == 0)
    def _():
        run_ref[...] = jnp.zeros_like(run_ref)

    half = a_ref.shape[1]
    x1 = (x_ref[...]
          + jnp.dot(a_ref[...], wo_ref[:half, :], preferred_element_type=F32)
          + jnp.dot(r_ref[...], wo_ref[half:, :], preferred_element_type=F32))
    x1_ref[...] = x1
    ms = jnp.mean(x1 * x1, axis=-1, keepdims=True)
    h2 = x1 * lax.rsqrt(ms + EPS) * g2_ref[...]
    d2 = h2.shape[1] // 2
    lo_bits = pltpu.bitcast(h2[:, :d2].astype(BF16).astype(F32), U32)
    hi_bits = pltpu.bitcast(h2[:, d2:].astype(BF16).astype(F32), U32)
    hp_ref[...] = (hi_bits & jnp.uint32(0xFFFF0000)) | (lo_bits >> 16)

    h_hi, h_lo = _bf16_split(h2, 2)
    w_hi, w_lo = _bf16_split(wr_ref[...], 2)
    logits = (jnp.dot(h_hi, w_hi, preferred_element_type=F32)
              + jnp.dot(h_lo, w_hi, preferred_element_type=F32)
              + jnp.dot(h_hi, w_lo, preferred_element_type=F32)) + br_ref[...]
    tm = logits.shape[0]
    lane = lax.broadcasted_iota(I32, (tm, LANES), 1).astype(F32)
    neg = jnp.float32(-jnp.inf)
    ti = jnp.zeros((tm, LANES), F32)
    tv = jnp.zeros((tm, LANES), F32)
    onehot = jnp.zeros((tm, LANES), F32)
    sels = []
    work = logits
    for k in range(TOP_K):
        m = jnp.max(work, axis=-1, keepdims=True)
        idx = jnp.min(jnp.where(work == m, lane, float(LANES)), axis=-1, keepdims=True)
        sel = lane == idx
        sels.append(sel)
        ti = jnp.where(lane == k, idx, ti)
        tv = jnp.where(lane == k, m, tv)
        onehot = jnp.where(sel, 1.0, onehot)
        work = jnp.where(sel, neg, work)
    valid = lane < TOP_K
    e = jnp.where(valid, jnp.exp(tv - jnp.max(jnp.where(valid, tv, neg), axis=-1, keepdims=True)), 0.0)
    tg_ref[...] = e / jnp.sum(e, axis=-1, keepdims=True)
    ti_ref[...] = ti.astype(I32)

    before = jnp.dot(tri_ref[...], onehot.astype(BF16), preferred_element_type=F32) + run_ref[...]
    rk = jnp.zeros((tm, LANES), F32)
    for k in range(TOP_K):
        rk = jnp.where(lane == k, jnp.sum(jnp.where(sels[k], before, 0.0), axis=-1, keepdims=True), rk)
    rk_ref[...] = rk.astype(I32)
    run_ref[...] += jnp.sum(onehot, axis=0, keepdims=True)
    cnt_ref[...] = run_ref[...]


def _outproj(x2, a2, r2, wo_bf, g2, wr, br, tri, *, tm):
    t, d = x2.shape
    half = a2.shape[1]
    row = lambda i: (i, 0)
    fix = lambda i: (0, 0)
    return pl.pallas_call(
        _outproj_kernel,
        grid=(t // tm,),
        in_specs=[
            pl.BlockSpec((tm, d), row),
            pl.BlockSpec((tm, half), row),
            pl.BlockSpec((tm, half), row),
            pl.BlockSpec((2 * half, d), fix),
            pl.BlockSpec((1, d), fix),
            pl.BlockSpec((d, LANES), fix),
            pl.BlockSpec((1, LANES), fix),
            pl.BlockSpec((tm, tm), fix),
        ],
        out_specs=[
            pl.BlockSpec((tm, d), row),
            pl.BlockSpec((tm, d // 2), row),
            pl.BlockSpec((tm, LANES), row),
            pl.BlockSpec((tm, LANES), row),
            pl.BlockSpec((tm, LANES), row),
            pl.BlockSpec((1, LANES), fix),
        ],
        out_shape=[
            jax.ShapeDtypeStruct((t, d), F32),
            jax.ShapeDtypeStruct((t, d // 2), U32),
            jax.ShapeDtypeStruct((t, LANES), I32),
            jax.ShapeDtypeStruct((t, LANES), F32),
            jax.ShapeDtypeStruct((t, LANES), I32),
            jax.ShapeDtypeStruct((1, LANES), F32),
        ],
        scratch_shapes=[pltpu.VMEM((1, LANES), F32)],
        compiler_params=_cparams("arbitrary"),
        name="outproj",
    )(x2, a2, r2, wo_bf, g2, wr, br, tri)


def _row_copy_wait(src, dst, sem, n):
    pltpu.make_async_copy(src.at[pl.ds(0, n)], dst.at[pl.ds(0, n)], sem).wait()


def _experts_kernel(be_ref, nu_ref, idx0_ref, idxn_ref, hp_ref, wgu_ref, bgu_ref, wd_ref, bd_ref,
                    yb_ref, wgu_bf, wd_bf, xbuf, sem):
    i = pl.program_id(0)
    nu = nu_ref[0]
    rb = xbuf.shape[1]
    shift = TOP_K.bit_length() - 1
    slot = i % 2

    def gather(idx_ref, s):
        for m in range(rb):
            tok = idx_ref[0, m] >> shift
            pltpu.make_async_copy(hp_ref.at[pl.ds(tok, 1)], xbuf.at[s, pl.ds(m, 1)], sem.at[s]).start()

    @pl.when(i == 0)
    def _():
        gather(idx0_ref, 0)

    fresh = jnp.logical_or(i == 0, be_ref[i] != be_ref[jnp.maximum(i - 1, 0)])

    @pl.when(fresh)
    def _():
        wgu_bf[...] = wgu_ref[...].astype(BF16)
        wd_bf[...] = wd_ref[...].astype(BF16)

    @pl.when(i < nu)
    def _():
        _row_copy_wait(hp_ref, xbuf.at[slot], sem.at[slot], rb)
        gather(idxn_ref, 1 - slot)
        xp = xbuf[slot]
        x_lo = pltpu.bitcast(xp << 16, F32).astype(BF16)
        x_hi = pltpu.bitcast(xp & jnp.uint32(0xFFFF0000), F32).astype(BF16)
        x = jnp.concatenate([x_lo, x_hi], axis=1)
        gu = jnp.dot(x, wgu_bf[...], preferred_element_type=F32) + bgu_ref[...]
        f = gu.shape[1] // 2
        g_lin = jnp.minimum(gu[:, :f], SWIGLU_LIMIT)
        u_lin = jnp.clip(gu[:, f:], -SWIGLU_LIMIT, SWIGLU_LIMIT)
        act = g_lin * _sigmoid(SWIGLU_ALPHA * g_lin) * (u_lin + 1.0)
        yb_ref[...] = jnp.dot(act.astype(BF16), wd_bf[...], preferred_element_type=F32) + bd_ref[...]

    @pl.when(i == nu)
    def _():
        _row_copy_wait(hp_ref, xbuf.at[slot], sem.at[slot], rb)

    @pl.when(i >= nu)
    def _():
        yb_ref[...] = jnp.zeros_like(yb_ref)


def _experts(block_e, n_used, inv, hp, w_gate_up, b_gate_up3, w_down, b_down3, *, rb):
    nblk = inv.shape[0]
    _, w = hp.shape
    d = 2 * w
    f2 = w_gate_up.shape[2]
    f = w_down.shape[1]
    wsel = lambda i, be, nu: (be[i], 0, 0)
    grid_spec = pltpu.PrefetchScalarGridSpec(
        num_scalar_prefetch=2,
        grid=(nblk,),
        in_specs=[
            pl.BlockSpec((None, 1, rb), lambda i, be, nu: (0, 0, 0), memory_space=pltpu.SMEM),
            pl.BlockSpec((None, 1, rb), lambda i, be, nu: (jnp.minimum(i + 1, nblk - 1), 0, 0),
                         memory_space=pltpu.SMEM),
            pl.BlockSpec(memory_space=pl.ANY),
            pl.BlockSpec((None, d, f2), wsel),
            pl.BlockSpec((None, 1, f2), wsel),
            pl.BlockSpec((None, f, d), wsel),
            pl.BlockSpec((None, 1, d), wsel),
        ],
        out_specs=pl.BlockSpec((rb, d), lambda i, be, nu: (i, 0)),
        scratch_shapes=[pltpu.VMEM((d, f2), BF16), pltpu.VMEM((f, d), BF16),
                        pltpu.VMEM((2, rb, w), U32), pltpu.SemaphoreType.DMA((2,))],
    )
    return pl.pallas_call(
        _experts_kernel,
        grid_spec=grid_spec,
        out_shape=jax.ShapeDtypeStruct((nblk * rb, d), F32),
        compiler_params=_cparams("arbitrary"),
        name="experts",
    )(block_e, n_used, inv, inv, hp, w_gate_up, b_gate_up3, w_down, b_down3)


def _combine_kernel(dest0_ref, destn_ref, x1_ref, tg_ref, yb_ref, o_ref, buf, sem):
    i = pl.program_id(0)
    n = pl.num_programs(0)
    tm = x1_ref.shape[0]
    slot = i % 2

    def gather(dest_ref, s):
        def body(t, c):
            for k in range(TOP_K):
                r = dest_ref[0, t * TOP_K + k]
                pltpu.make_async_copy(yb_ref.at[pl.ds(r, 1)], buf.at[s, k, pl.ds(t, 1)], sem.at[s]).start()
            return c
        lax.fori_loop(0, tm, body, 0, unroll=8)

    @pl.when(i == 0)
    def _():
        gather(dest0_ref, 0)

    @pl.when(i + 1 < n)
    def _():
        gather(destn_ref, 1 - slot)

    for k in range(TOP_K):
        _row_copy_wait(yb_ref, buf.at[slot, k], sem.at[slot], tm)
    tg = tg_ref[...]
    out = x1_ref[...]
    for k in range(TOP_K):
        out = out + tg[:, k:k + 1] * buf[slot, k]
    o_ref[...] = out


def _combine(dest2, x1, tg, yb, *, tm):
    t, d = x1.shape
    n = t // tm
    return pl.pallas_call(
        _combine_kernel,
        grid=(n,),
        in_specs=[
            pl.BlockSpec((None, 1, tm * TOP_K), lambda i: (0, 0, 0), memory_space=pltpu.SMEM),
            pl.BlockSpec((None, 1, tm * TOP_K), lambda i: (jnp.minimum(i + 1, n - 1), 0, 0),
                         memory_space=pltpu.SMEM),
            pl.BlockSpec((tm, d), lambda i: (i, 0)),
            pl.BlockSpec((tm, LANES), lambda i: (i, 0)),
            pl.BlockSpec(memory_space=pl.ANY),
        ],
        out_specs=pl.BlockSpec((tm, d), lambda i: (i, 0)),
        out_shape=jax.ShapeDtypeStruct((t, d), F32),
        scratch_shapes=[pltpu.VMEM((2, TOP_K, tm, d), F32), pltpu.SemaphoreType.DMA((2,))],
        compiler_params=_cparams("arbitrary"),
        name="combine",
    )(dest2, dest2, x1, tg, yb)


def _block_ones(n, blk):
    i = jnp.arange(n)
    return (i[:, None] // blk == i[None, :] // blk).astype(BF16)


def _pair_suffix(tq):
    j = jnp.arange(2 * tq)
    return ((j[:, None] > j[None, :]) & (j[:, None] // tq == j[None, :] // tq)).astype(BF16)


def _chunk_tril(n, blk):
    i = jnp.arange(n)
    return ((i[:, None] // blk == i[None, :] // blk) & (i[None, :] <= i[:, None])).astype(BF16)


def _strict_tril(n):
    i = jnp.arange(n)
    return (i[None, :] < i[:, None]).astype(BF16)


def _layer(x, norm1_g, w_in, q_g, k_g, sbo_g, lb, hgo_g, w_out, norm2_g, w_router, b_router,
           w_gate_up, b_gate_up, w_down, b_down, *, tm_in, tq, tm_out, tm_row, rb):
    bsz, seq, d = x.shape
    t = bsz * seq
    x2 = x.reshape(t, d)
    nsb = SB_WIDTH // SB_HEAD_DIM

    qkv, hq, hg, hk, hi, sg = _inproj(
        x2, norm1_g.reshape(1, d), w_in.astype(BF16),
        jnp.tile(q_g, nsb).reshape(1, SB_WIDTH), jnp.tile(k_g, nsb).reshape(1, SB_WIDTH),
        lb.reshape(1, HG_WIDTH), _block_ones(256, SB_HEAD_DIM), tm=tm_in)

    a = _sbattn(qkv.reshape(bsz, seq, 3 * SB_WIDTH), sbo_g.reshape(1, SB_WIDTH),
                _pair_suffix(tq), _block_ones(LANES, SB_HEAD_DIM), tq=tq)
    sh = (bsz, seq, HG_WIDTH)
    r = _hgrn(hq.reshape(sh), hg.reshape(sh), hk.reshape(sh), hi.reshape(sh), sg.reshape(sh),
              hgo_g.reshape(1, HG_WIDTH), _chunk_tril(LANES, HG_CHUNK))

    wr = jnp.zeros((d, LANES), F32).at[:, :N_EXPERTS].set(w_router)
    br = jnp.full((1, LANES), -1e30, F32).at[0, :N_EXPERTS].set(b_router)
    x1, hp, ti, tg, rk, cnt = _outproj(
        x2, a.reshape(t, SB_WIDTH), r.reshape(t, HG_WIDTH), w_out.astype(BF16),
        norm2_g.reshape(1, d), wr, br, _strict_tril(tm_out), tm=tm_out)

    counts = cnt[0, :N_EXPERTS].astype(I32)
    padded = (counts + rb - 1) // rb * rb
    pad_ends = jnp.cumsum(padded)
    pad_starts = pad_ends - padded
    n_assign = t * TOP_K
    nblk = n_assign // rb + N_EXPERTS
    eids = jnp.arange(N_EXPERTS, dtype=I32)
    dest = rk[:, :TOP_K] + jnp.sum(jnp.where(ti[:, :TOP_K, None] == eids, pad_starts, 0), axis=-1)
    dest2 = dest.reshape(t // tm_row, 1, tm_row * TOP_K)
    n_used = (pad_ends[-1] // rb).astype(I32).reshape(1)
    blk_start = jnp.minimum(jnp.arange(nblk, dtype=I32), n_used[0] - 1) * rb
    block_e = jnp.sum((blk_start[:, None] >= pad_ends[None, :]).astype(I32), axis=1)

    n_pad = nblk * rb - n_assign
    seg_len = jnp.concatenate([padded - counts, (nblk * rb - pad_ends[-1]).reshape(1)])
    seg_end = jnp.cumsum(seg_len)
    seg_row0 = jnp.concatenate([pad_starts + counts, pad_ends[-1:]]) - (seg_end - seg_len)
    p = jnp.arange(n_pad, dtype=I32)
    seg = jnp.sum((p[:, None] >= seg_end[None, :]).astype(I32), axis=1)
    pad_rows = p + jnp.sum(jnp.where(seg[:, None] == jnp.arange(N_EXPERTS + 1, dtype=I32), seg_row0, 0), axis=1)
    keys = jnp.concatenate([dest.reshape(-1), pad_rows])
    vals = jnp.concatenate([jnp.arange(n_assign, dtype=I32), jnp.zeros((n_pad,), I32)])
    inv = lax.sort((keys, vals), num_keys=1)[1].reshape(nblk, 1, rb)

    yb = _experts(block_e, n_used, inv, hp, w_gate_up, b_gate_up.reshape(N_EXPERTS, 1, -1),
                  w_down, b_down.reshape(N_EXPERTS, 1, -1), rb=rb)
    out = _combine(dest2, x1, tg, yb, tm=tm_row)
    return out.reshape(bsz, seq, d)


def kernel(x, norm1_g, w_in, sb_q_norm_g, sb_k_norm_g, sb_out_norm_g, hg_lb_logits, hg_out_norm_g,
           w_out, norm2_g, w_router, b_router, w_gate_up, b_gate_up, w_down, b_down):
    depth = w_in.shape[0]
    lb_all = jnp.cumsum(jax.nn.softmax(hg_lb_logits.astype(F32), axis=0), axis=0)
    for l in range(depth):
        x = _layer(x, norm1_g[l], w_in[l], sb_q_norm_g[l], sb_k_norm_g[l], sb_out_norm_g[l], lb_all[l],
                   hg_out_norm_g[l], w_out[l], norm2_g[l], w_router[l], b_router[l],
                   w_gate_up[l], b_gate_up[l], w_down[l], b_down[l],
                   tm_in=512, tq=128, tm_out=512, tm_row=256, rb=256)
    return x
```

```python
import functools

import jax
import jax.numpy as jnp
from jax import lax
from jax.experimental import pallas as pl
from jax.experimental.pallas import tpu as pltpu

F32 = jnp.float32
BF16 = jnp.bfloat16
I32 = jnp.int32
U32 = jnp.uint32

EPS = 1e-6
LANES = 128
SB_HEAD_DIM = 64
SB_WIDTH = 512
HG_HEAD_DIM = 128
HG_WIDTH = 512
HG_CHUNK = 32
N_EXPERTS = 32
TOP_K = 4
SWIGLU_LIMIT = 7.0
SWIGLU_ALPHA = 1.702
VMEM_LIMIT = 56 * 1024 * 1024

NT_DIMS = (((1,), (1,)), ((), ()))
NPAIR = SB_WIDTH // LANES
LOG2E = 1.4426950408889634


def _cparams(*sem):
    return pltpu.CompilerParams(dimension_semantics=sem, vmem_limit_bytes=VMEM_LIMIT)


def _bf16_split(a, n):
    parts = []
    for _ in range(n - 1):
        p = a.astype(BF16)
        parts.append(p)
        a = a - p.astype(F32)
    parts.append(a.astype(BF16))
    return parts


def _sigmoid(a):
    return 1.0 / (1.0 + jnp.exp(-a))


def _pack_bf16_pairs(a):
    w = a.shape[1] // 2
    lo_bits = pltpu.bitcast(a[:, :w].astype(BF16).astype(F32), U32)
    hi_bits = pltpu.bitcast(a[:, w:].astype(BF16).astype(F32), U32)
    return (hi_bits & jnp.uint32(0xFFFF0000)) | (lo_bits >> 16)


def _unpack_bf16_pairs(p):
    return pltpu.bitcast(p << 16, F32), pltpu.bitcast(p & jnp.uint32(0xFFFF0000), F32)


def _inproj_kernel(x_ref, g1_ref, w_ref, gq_ref, gk_ref, lb_ref, bd_ref,
                   qkv_ref, hq_ref, hg_ref, hk_ref, hi_ref, sg_ref):
    x = x_ref[...]
    ms = jnp.mean(x * x, axis=-1, keepdims=True)
    h = (x * lax.rsqrt(ms + EPS) * g1_ref[...]).astype(BF16)
    bd = bd_ref[...]

    def seg(j):
        return jnp.dot(h, w_ref[:, j * 512:(j + 1) * 512], preferred_element_type=F32)

    def head_norm(a, g, scale):
        outs = []
        for c in range(2):
            ac = a[:, c * 256:(c + 1) * 256]
            ss = jnp.dot((ac * ac).astype(BF16), bd, preferred_element_type=F32)
            outs.append(ac * lax.rsqrt(ss * (1.0 / SB_HEAD_DIM) + EPS))
        y = jnp.concatenate(outs, axis=1) * g
        return y * scale if scale != 1.0 else y

    qkv_ref[:, 0:512] = head_norm(seg(0), gq_ref[...], LOG2E * SB_HEAD_DIM ** -0.5).astype(BF16)
    qkv_ref[:, 512:1024] = head_norm(seg(1), gk_ref[...], 1.0).astype(BF16)
    qkv_ref[:, 1024:1536] = seg(2).astype(BF16)
    hq_ref[...] = seg(3).astype(BF16)
    f = seg(4)
    lb = lb_ref[...]
    hg_ref[...] = jnp.log(lb + (1.0 - lb) * _sigmoid(f))
    hk_ref[...] = (1.0 - lb) * _sigmoid(-f)
    hi_ref[...] = seg(5).astype(BF16)
    gate = seg(6)
    sg_ref[...] = (gate * _sigmoid(gate)).astype(BF16)


def _inproj(x2, g1, w_bf, gq, gk, lb, bd, *, tm):
    t, d = x2.shape
    n = w_bf.shape[1]
    row = lambda i: (i, 0)
    fix = lambda i: (0, 0)
    return pl.pallas_call(
        _inproj_kernel,
        grid=(t // tm,),
        in_specs=[
            pl.BlockSpec((tm, d), row),
            pl.BlockSpec((1, d), fix),
            pl.BlockSpec((d, n), fix),
            pl.BlockSpec((1, 512), fix),
            pl.BlockSpec((1, 512), fix),
            pl.BlockSpec((1, 512), fix),
            pl.BlockSpec((256, 256), fix),
        ],
        out_specs=[
            pl.BlockSpec((tm, 1536), row),
            pl.BlockSpec((tm, 512), row),
            pl.BlockSpec((tm, 512), row),
            pl.BlockSpec((tm, 512), row),
            pl.BlockSpec((tm, 512), row),
            pl.BlockSpec((tm, 512), row),
        ],
        out_shape=[
            jax.ShapeDtypeStruct((t, 1536), BF16),
            jax.ShapeDtypeStruct((t, 512), BF16),
            jax.ShapeDtypeStruct((t, 512), F32),
            jax.ShapeDtypeStruct((t, 512), F32),
            jax.ShapeDtypeStruct((t, 512), BF16),
            jax.ShapeDtypeStruct((t, 512), BF16),
        ],
        compiler_params=_cparams("parallel"),
        name="inproj",
    )(x2, g1, w_bf, gq, gk, lb, bd)


def _sbattn_kernel(q_ref, k_ref, v_ref, go_ref, u2_ref, bd_ref, o_ref, vm_ref, *scr, tq):
    acc_ref = scr[0:NPAIR]
    carry_ref = scr[NPAIR:2 * NPAIR]
    z_ref = scr[2 * NPAIR:3 * NPAIR]
    w_ref = scr[3 * NPAIR:4 * NPAIR]
    qi = pl.program_id(1)
    hps = range(NPAIR)
    lane = lax.broadcasted_iota(I32, (tq, LANES), 1)
    lo = lane < SB_HEAD_DIM

    @pl.when(qi == 0)
    def _():
        v = v_ref[...]
        lo_all = (lax.broadcasted_iota(I32, v.shape, 1) & (LANES - 1)) < SB_HEAD_DIM
        zv = jnp.zeros_like(v)
        vm_ref[0] = jnp.where(lo_all, v, zv)
        vm_ref[1] = jnp.where(lo_all, zv, v)

    u2 = u2_ref[...]
    for hp in hps:
        acc_ref[hp][...] = jnp.zeros_like(acc_ref[hp])
        carry_ref[hp][...] = jnp.zeros_like(carry_ref[hp])
    row = lax.broadcasted_iota(I32, (2 * tq, tq), 0)
    col = lax.broadcasted_iota(I32, (2 * tq, tq), 1)
    strict = col < jnp.where(row >= tq, row - tq, row)
    q2s = []
    for hp in hps:
        q = q_ref[:, hp * LANES:(hp + 1) * LANES]
        zq = jnp.zeros_like(q)
        q2s.append(jnp.concatenate([jnp.where(lo, q, zq), jnp.where(lo, zq, q)], axis=0))

    def scores(kb):
        off = pl.multiple_of(kb * tq, tq)
        return [lax.dot_general(q2s[hp], k_ref[pl.ds(off, tq), hp * LANES:(hp + 1) * LANES], NT_DIMS,
                                preferred_element_type=F32) for hp in hps]

    def weighted_values(kb):
        off = pl.multiple_of(kb * tq, tq)
        for hp in hps:
            vcat = jnp.concatenate([vm_ref[0, pl.ds(off, tq), hp * LANES:(hp + 1) * LANES],
                                    vm_ref[1, pl.ds(off, tq), hp * LANES:(hp + 1) * LANES]], axis=0)
            acc_ref[hp][...] += jnp.dot(w_ref[hp][...], vcat, preferred_element_type=F32)

    def weights(zs, masked):
        lgs = []
        for z in zs:
            nz = -z
            lg = jnp.minimum(nz, 0.0) - jnp.log2(1.0 + jnp.exp2(jnp.minimum(z, nz)))
            if masked:
                lg = jnp.where(strict, lg, 0.0)
            lgs.append(lg)
        withins = []
        for pr in range(NPAIR // 2):
            cat = jnp.concatenate([lgs[2 * pr], lgs[2 * pr + 1]], axis=1).astype(BF16)
            res = jnp.dot(cat, u2, preferred_element_type=F32)
            withins += [res[:, :tq], res[:, tq:]]
        reps = tq // LANES
        carries = [jnp.concatenate([carry_ref[hp][...]] * reps, axis=1) for hp in hps]
        ws = [jnp.exp2((zs[hp] + lgs[hp]) + (carries[hp] + withins[hp])) for hp in hps]
        if masked:
            ws = [jnp.where(strict, w, 0.0) for w in ws]
        for hp in hps:
            carry_ref[hp][...] += jnp.sum(lgs[hp], axis=-1, keepdims=True)
        return [jnp.concatenate([w[:tq], w[tq:]], axis=1).astype(BF16) for w in ws]

    ws = weights(scores(qi), True)
    zn = scores(jnp.maximum(qi - 1, 0))
    for hp in hps:
        w_ref[hp][...] = ws[hp]
        z_ref[hp][...] = zn[hp]

    def body(it, c):
        kb = qi - 1 - it
        zs = [z_ref[hp][...] for hp in hps]
        zn = scores(jnp.maximum(kb - 1, 0))
        weighted_values(kb + 1)
        ws = weights(zs, False)
        for hp in hps:
            w_ref[hp][...] = ws[hp]
            z_ref[hp][...] = zn[hp]
        return c

    lax.fori_loop(0, qi, body, 0)
    weighted_values(0)
    for hp in hps:
        a = acc_ref[hp][...]
        ss = jnp.dot((a * a).astype(BF16), bd_ref[...], preferred_element_type=F32)
        o_ref[:, hp * LANES:(hp + 1) * LANES] = (
            a * lax.rsqrt(ss * (1.0 / SB_HEAD_DIM) + EPS) * go_ref[:, hp * LANES:(hp + 1) * LANES]).astype(BF16)


def _sbattn(qkv, go, u2, bd, *, tq):
    b, s, _ = qkv.shape
    return pl.pallas_call(
        functools.partial(_sbattn_kernel, tq=tq),
        grid=(b, s // tq),
        in_specs=[
            pl.BlockSpec((None, tq, SB_WIDTH), lambda bi, qi: (bi, qi, 0)),
            pl.BlockSpec((None, s, SB_WIDTH), lambda bi, qi: (bi, 0, 1)),
            pl.BlockSpec((None, s, SB_WIDTH), lambda bi, qi: (bi, 0, 2)),
            pl.BlockSpec((1, SB_WIDTH), lambda bi, qi: (0, 0)),
            pl.BlockSpec((2 * tq, 2 * tq), lambda bi, qi: (0, 0)),
            pl.BlockSpec((LANES, LANES), lambda bi, qi: (0, 0)),
        ],
        out_specs=pl.BlockSpec((None, tq, SB_WIDTH), lambda bi, qi: (bi, qi, 0)),
        out_shape=jax.ShapeDtypeStruct((b, s, SB_WIDTH), BF16),
        scratch_shapes=([pltpu.VMEM((2, s, SB_WIDTH), BF16)]
                        + [pltpu.VMEM((tq, LANES), F32)] * NPAIR
                        + [pltpu.VMEM((2 * tq, LANES), F32)] * NPAIR
                        + [pltpu.VMEM((2 * tq, tq), F32)] * NPAIR
                        + [pltpu.VMEM((tq, 2 * tq), BF16)] * NPAIR),
        compiler_params=_cparams("parallel", "arbitrary"),
        name="sbattn",
    )(qkv, qkv, qkv, go, u2, bd)


def _hgrn_kernel(q_ref, g_ref, k_ref, v_ref, sg_ref, gn_ref, tl_ref, o_ref, *state_ref):
    s = q_ref.shape[0]
    grp = LANES
    nch = grp // HG_CHUNK
    hs = range(HG_WIDTH // HG_HEAD_DIM)
    for h in hs:
        state_ref[h][...] = jnp.zeros_like(state_ref[h])
    tl = tl_ref[...]
    rr = lax.broadcasted_iota(I32, (grp, grp), 0)
    cc = lax.broadcasted_iota(I32, (grp, grp), 1)
    shift = HG_CHUNK.bit_length() - 1
    rchunk = rr >> shift
    causal = (rchunk == (cc >> shift)) & (cc <= rr)
    inchunk = [rchunk == ci for ci in range(nch)]

    def group(r, c):
        off = pl.multiple_of(r * grp, grp)
        col = lambda h: slice(h * HG_HEAD_DIM, (h + 1) * HG_HEAD_DIM)
        rows = pl.ds(off, grp)
        vs = [v_ref[rows, col(h)] for h in hs]
        bs = [sum(jnp.dot(tl, p, preferred_element_type=F32) for p in _bf16_split(g_ref[rows, col(h)], 2))
              for h in hs]
        lasts = [[b[(ci + 1) * HG_CHUNK - 1:(ci + 1) * HG_CHUNK, :] for ci in range(nch)] for b in bs]
        bls = [jnp.concatenate([jnp.broadcast_to(l, (HG_CHUNK, HG_HEAD_DIM)) for l in last], axis=0)
               for last in lasts]
        qes = [(q_ref[rows, col(h)].astype(F32) * jnp.exp(bs[h])).astype(BF16) for h in hs]
        kes = [(k_ref[rows, col(h)] * jnp.exp(-bs[h])).astype(BF16) for h in hs]
        kds = [(k_ref[rows, col(h)] * jnp.exp(bls[h] - bs[h])).astype(BF16) for h in hs]
        attns = [jnp.where(causal, lax.dot_general(qes[h], kes[h], NT_DIMS, preferred_element_type=F32), 0.0)
                 .astype(BF16) for h in hs]
        os = [jnp.dot(attns[h], vs[h], preferred_element_type=F32) for h in hs]
        vts = [vs[h].astype(F32).T.astype(BF16) for h in hs]
        uts = []
        for h in hs:
            zk = jnp.zeros_like(kds[h])
            kdx = jnp.concatenate([jnp.where(inchunk[ci], kds[h], zk) for ci in range(nch)], axis=1)
            uts.append(jnp.dot(vts[h], kdx, preferred_element_type=F32))
        for h in hs:
            st = state_ref[h][...]
            parts = []
            for ci in range(nch):
                parts.append(st.astype(BF16))
                st = st * jnp.exp(lasts[h][ci]) + uts[h][:, ci * HG_HEAD_DIM:(ci + 1) * HG_HEAD_DIM]
            state_ref[h][...] = st
            zq = jnp.zeros_like(qes[h])
            qx = jnp.concatenate([jnp.where(inchunk[ci], qes[h], zq) for ci in range(nch)], axis=1)
            o = os[h] + lax.dot_general(qx, jnp.concatenate(parts, axis=1), NT_DIMS, preferred_element_type=F32)
            ms = jnp.mean(o * o, axis=-1, keepdims=True)
            y = o * lax.rsqrt(ms + EPS) * gn_ref[:, col(h)]
            o_ref[rows, col(h)] = (y * sg_ref[rows, col(h)].astype(F32)).astype(BF16)
        return c

    lax.fori_loop(0, s // grp, group, 0)


def _hgrn(hq, hg, hk, hi, sg, gn, tl):
    b, s, w = hq.shape
    nh = HG_WIDTH // HG_HEAD_DIM
    blk = pl.BlockSpec((None, s, w), lambda bi: (bi, 0, 0))
    return pl.pallas_call(
        _hgrn_kernel,
        grid=(b,),
        in_specs=[blk, blk, blk, blk, blk,
                  pl.BlockSpec((1, w), lambda bi: (0, 0)),
                  pl.BlockSpec((LANES, LANES), lambda bi: (0, 0))],
        out_specs=blk,
        out_shape=jax.ShapeDtypeStruct((b, s, w), BF16),
        scratch_shapes=[pltpu.VMEM((HG_HEAD_DIM, HG_HEAD_DIM), F32)] * nh,
        compiler_params=_cparams("parallel"),
        name="hgrn",
    )(hq, hg, hk, hi, sg, gn, tl)


def _outproj_kernel(x_ref, a_ref, r_ref, wo_ref, g2_ref, wr_ref, br_ref, tri_ref,
                    x1_ref, hp_ref, ti_ref, tg_ref, rk_ref, cnt_ref, run_ref):
    i = pl.program_id(0)

    @pl.when(i == 0)
    def _():
        run_ref[...] = jnp.zeros_like(run_ref)

    half = a_ref.shape[1]
    x1 = (x_ref[...]
          + jnp.dot(a_ref[...], wo_ref[:half, :], preferred_element_type=F32)
          + jnp.dot(r_ref[...], wo_ref[half:, :], preferred_element_type=F32))
    x1_ref[...] = x1
    ms = jnp.mean(x1 * x1, axis=-1, keepdims=True)
    h2 = x1 * lax.rsqrt(ms + EPS) * g2_ref[...]
    hp_ref[...] = _pack_bf16_pairs(h2)

    h_hi, h_lo = _bf16_split(h2, 2)
    w_hi, w_lo = _bf16_split(wr_ref[...], 2)
    logits = (jnp.dot(h_hi, w_hi, preferred_element_type=F32)
              + jnp.dot(h_lo, w_hi, preferred_element_type=F32)
              + jnp.dot(h_hi, w_lo, preferred_element_type=F32)) + br_ref[...]
    tm = logits.shape[0]
    lane = lax.broadcasted_iota(I32, (tm, LANES), 1).astype(F32)
    neg = jnp.float32(-jnp.inf)
    ti = jnp.zeros((tm, LANES), F32)
    tv = jnp.zeros((tm, LANES), F32)
    onehot = jnp.zeros((tm, LANES), F32)
    sels = []
    work = logits
    for k in range(TOP_K):
        m = jnp.max(work, axis=-1, keepdims=True)
        idx = jnp.min(jnp.where(work == m, lane, float(LANES)), axis=-1, keepdims=True)
        sel = lane == idx
        sels.append(sel)
        ti = jnp.where(lane == k, idx, ti)
        tv = jnp.where(lane == k, m, tv)
        onehot = jnp.where(sel, 1.0, onehot)
        work = jnp.where(sel, neg, work)
    valid = lane < TOP_K
    e = jnp.where(valid, jnp.exp(tv - jnp.max(jnp.where(valid, tv, neg), axis=-1, keepdims=True)), 0.0)
    tg_ref[...] = e / jnp.sum(e, axis=-1, keepdims=True)
    ti_ref[...] = ti.astype(I32)

    before = jnp.dot(tri_ref[...], onehot.astype(BF16), preferred_element_type=F32) + run_ref[...]
    rk = jnp.zeros((tm, LANES), F32)
    for k in range(TOP_K):
        rk = jnp.where(lane == k, jnp.sum(jnp.where(sels[k], before, 0.0), axis=-1, keepdims=True), rk)
    rk_ref[...] = rk.astype(I32)
    run_ref[...] += jnp.sum(onehot, axis=0, keepdims=True)
    cnt_ref[...] = run_ref[...]


def _outproj(x2, a2, r2, wo_bf, g2, wr, br, tri, *, tm):
    t, d = x2.shape
    half = a2.shape[1]
    row = lambda i: (i, 0)
    fix = lambda i: (0, 0)
    return pl.pallas_call(
        _outproj_kernel,
        grid=(t // tm,),
        in_specs=[
            pl.BlockSpec((tm, d), row),
            pl.BlockSpec((tm, half), row),
            pl.BlockSpec((tm, half), row),
            pl.BlockSpec((2 * half, d), fix),
            pl.BlockSpec((1, d), fix),
            pl.BlockSpec((d, LANES), fix),
            pl.BlockSpec((1, LANES), fix),
            pl.BlockSpec((tm, tm), fix),
        ],
        out_specs=[
            pl.BlockSpec((tm, d), row),
            pl.BlockSpec((tm, d // 2), row),
            pl.BlockSpec((tm, LANES), row),
            pl.BlockSpec((tm, LANES), row),
            pl.BlockSpec((tm, LANES), row),
            pl.BlockSpec((1, LANES), fix),
        ],
        out_shape=[
            jax.ShapeDtypeStruct((t, d), F32),
            jax.ShapeDtypeStruct((t, d // 2), U32),
            jax.ShapeDtypeStruct((t, LANES), I32),
            jax.ShapeDtypeStruct((t, LANES), F32),
            jax.ShapeDtypeStruct((t, LANES), I32),
            jax.ShapeDtypeStruct((1, LANES), F32),
        ],
        scratch_shapes=[pltpu.VMEM((1, LANES), F32)],
        compiler_params=_cparams("arbitrary"),
        name="outproj",
    )(x2, a2, r2, wo_bf, g2, wr, br, tri)


def _row_copy_wait(src, dst, sem, n):
    pltpu.make_async_copy(src.at[pl.ds(0, n)], dst.at[pl.ds(0, n)], sem).wait()


def _scatter_kernel(dest_ref, hp_ref, xz_ref, xb_ref, sem):
    del xz_ref
    tm = hp_ref.shape[0]

    def body(t, c):
        for k in range(TOP_K):
            r = dest_ref[0, t * TOP_K + k]
            pltpu.make_async_copy(hp_ref.at[pl.ds(t, 1)], xb_ref.at[pl.ds(r, 1)], sem).start()
        return c

    lax.fori_loop(0, tm, body, 0)
    for _ in range(TOP_K):
        _row_copy_wait(hp_ref, xb_ref, sem, tm)


def _scatter(dest2, hp, xzero, *, tm):
    t, w = hp.shape
    return pl.pallas_call(
        _scatter_kernel,
        grid=(t // tm,),
        in_specs=[
            pl.BlockSpec((None, 1, tm * TOP_K), lambda i: (i, 0, 0), memory_space=pltpu.SMEM),
            pl.BlockSpec((tm, w), lambda i: (i, 0)),
            pl.BlockSpec(memory_space=pl.ANY),
        ],
        out_specs=pl.BlockSpec(memory_space=pl.ANY),
        out_shape=jax.ShapeDtypeStruct(xzero.shape, xzero.dtype),
        scratch_shapes=[pltpu.SemaphoreType.DMA],
        input_output_aliases={2: 0},
        compiler_params=_cparams("arbitrary"),
        name="scatter",
    )(dest2, hp, xzero)


def _experts_kernel(be_ref, nu_ref, xb_ref, wgu_ref, bgu_ref, wd_ref, bd_ref, yb_ref, wgu_bf, wd_bf):
    i = pl.program_id(0)
    fresh = jnp.logical_or(i == 0, be_ref[i] != be_ref[jnp.maximum(i - 1, 0)])

    @pl.when(fresh)
    def _():
        wgu_bf[...] = wgu_ref[...].astype(BF16)
        wd_bf[...] = wd_ref[...].astype(BF16)

    @pl.when(i < nu_ref[0])
    def _():
        x_lo, x_hi = _unpack_bf16_pairs(xb_ref[...])
        x = jnp.concatenate([x_lo.astype(BF16), x_hi.astype(BF16)], axis=1)
        gu = jnp.dot(x, wgu_bf[...], preferred_element_type=F32) + bgu_ref[...]
        f = gu.shape[1] // 2
        g_lin = jnp.minimum(gu[:, :f], SWIGLU_LIMIT)
        u_lin = jnp.clip(gu[:, f:], -SWIGLU_LIMIT, SWIGLU_LIMIT)
        act = g_lin * _sigmoid(SWIGLU_ALPHA * g_lin) * (u_lin + 1.0)
        y = jnp.dot(act.astype(BF16), wd_bf[...], preferred_element_type=F32) + bd_ref[...]
        yb_ref[...] = _pack_bf16_pairs(y)

    @pl.when(i >= nu_ref[0])
    def _():
        yb_ref[...] = jnp.zeros_like(yb_ref)


def _experts(block_e, n_used, xb, w_gate_up, b_gate_up3, w_down, b_down3, *, rb):
    n_rows, w = xb.shape
    d = 2 * w
    f2 = w_gate_up.shape[2]
    f = w_down.shape[1]
    nblk = n_rows // rb
    rowblk = lambda i, be, nu: (jnp.minimum(i, nu[0] - 1), 0)
    wsel = lambda i, be, nu: (be[i], 0, 0)
    grid_spec = pltpu.PrefetchScalarGridSpec(
        num_scalar_prefetch=2,
        grid=(nblk,),
        in_specs=[
            pl.BlockSpec((rb, w), rowblk),
            pl.BlockSpec((None, d, f2), wsel),
            pl.BlockSpec((None, 1, f2), wsel),
            pl.BlockSpec((None, f, d), wsel),
            pl.BlockSpec((None, 1, d), wsel),
        ],
        out_specs=pl.BlockSpec((rb, w), lambda i, be, nu: (i, 0)),
        scratch_shapes=[pltpu.VMEM((d, f2), BF16), pltpu.VMEM((f, d), BF16)],
    )
    return pl.pallas_call(
        _experts_kernel,
        grid_spec=grid_spec,
        out_shape=jax.ShapeDtypeStruct((n_rows, w), U32),
        compiler_params=_cparams("arbitrary"),
        name="experts",
    )(block_e, n_used, xb, w_gate_up, b_gate_up3, w_down, b_down3)


def _combine_kernel(dest0_ref, destn_ref, x1_ref, tg_ref, yb_ref, o_ref, buf, sem):
    i = pl.program_id(0)
    n = pl.num_programs(0)
    tm = x1_ref.shape[0]
    slot = i % 2

    def gather(dest_ref, s):
        def body(t, c):
            for k in range(TOP_K):
                r = dest_ref[0, t * TOP_K + k]
                pltpu.make_async_copy(yb_ref.at[pl.ds(r, 1)], buf.at[s, k, pl.ds(t, 1)], sem.at[s]).start()
            return c
        lax.fori_loop(0, tm, body, 0, unroll=8)

    @pl.when(i == 0)
    def _():
        gather(dest0_ref, 0)

    @pl.when(i + 1 < n)
    def _():
        gather(destn_ref, 1 - slot)

    for k in range(TOP_K):
        _row_copy_wait(yb_ref, buf.at[slot, k], sem.at[slot], tm)
    tg = tg_ref[...]
    w = buf.shape[3]
    out_lo = x1_ref[:, :w]
    out_hi = x1_ref[:, w:]
    for k in range(TOP_K):
        y_lo, y_hi = _unpack_bf16_pairs(buf[slot, k])
        out_lo = out_lo + tg[:, k:k + 1] * y_lo
        out_hi = out_hi + tg[:, k:k + 1] * y_hi
    o_ref[:, :w] = out_lo
    o_ref[:, w:] = out_hi


def _combine(dest2, x1, tg, yb, *, tm):
    t, d = x1.shape
    n = t // tm
    return pl.pallas_call(
        _combine_kernel,
        grid=(n,),
        in_specs=[
            pl.BlockSpec((None, 1, tm * TOP_K), lambda i: (0, 0, 0), memory_space=pltpu.SMEM),
            pl.BlockSpec((None, 1, tm * TOP_K), lambda i: (jnp.minimum(i + 1, n - 1), 0, 0),
                         memory_space=pltpu.SMEM),
            pl.BlockSpec((tm, d), lambda i: (i, 0)),
            pl.BlockSpec((tm, LANES), lambda i: (i, 0)),
            pl.BlockSpec(memory_space=pl.ANY),
        ],
        out_specs=pl.BlockSpec((tm, d), lambda i: (i, 0)),
        out_shape=jax.ShapeDtypeStruct((t, d), F32),
        scratch_shapes=[pltpu.VMEM((2, TOP_K, tm, yb.shape[1]), yb.dtype), pltpu.SemaphoreType.DMA((2,))],
        compiler_params=_cparams("arbitrary"),
        name="combine",
    )(dest2, dest2, x1, tg, yb)


def _block_ones(n, blk):
    i = jnp.arange(n)
    return (i[:, None] // blk == i[None, :] // blk).astype(BF16)


def _pair_suffix(tq):
    j = jnp.arange(2 * tq)
    return ((j[:, None] > j[None, :]) & (j[:, None] // tq == j[None, :] // tq)).astype(BF16)


def _chunk_tril(n, blk):
    i = jnp.arange(n)
    return ((i[:, None] // blk == i[None, :] // blk) & (i[None, :] <= i[:, None])).astype(BF16)


def _strict_tril(n):
    i = jnp.arange(n)
    return (i[None, :] < i[:, None]).astype(BF16)


def _layer(x, norm1_g, w_in, q_g, k_g, sbo_g, lb, hgo_g, w_out, norm2_g, w_router, b_router,
           w_gate_up, b_gate_up, w_down, b_down, *, tm_in, tq, tm_out, tm_row, rb):
    bsz, seq, d = x.shape
    t = bsz * seq
    x2 = x.reshape(t, d)
    nsb = SB_WIDTH // SB_HEAD_DIM

    qkv, hq, hg, hk, hi, sg = _inproj(
        x2, norm1_g.reshape(1, d), w_in.astype(BF16),
        jnp.tile(q_g, nsb).reshape(1, SB_WIDTH), jnp.tile(k_g, nsb).reshape(1, SB_WIDTH),
        lb.reshape(1, HG_WIDTH), _block_ones(256, SB_HEAD_DIM), tm=tm_in)

    a = _sbattn(qkv.reshape(bsz, seq, 3 * SB_WIDTH), sbo_g.reshape(1, SB_WIDTH),
                _pair_suffix(tq), _block_ones(LANES, SB_HEAD_DIM), tq=tq)
    sh = (bsz, seq, HG_WIDTH)
    r = _hgrn(hq.reshape(sh), hg.reshape(sh), hk.reshape(sh), hi.reshape(sh), sg.reshape(sh),
              hgo_g.reshape(1, HG_WIDTH), _chunk_tril(LANES, HG_CHUNK))

    wr = jnp.zeros((d, LANES), F32).at[:, :N_EXPERTS].set(w_router)
    br = jnp.full((1, LANES), -1e30, F32).at[0, :N_EXPERTS].set(b_router)
    x1, hp, ti, tg, rk, cnt = _outproj(
        x2, a.reshape(t, SB_WIDTH), r.reshape(t, HG_WIDTH), w_out.astype(BF16),
        norm2_g.reshape(1, d), wr, br, _strict_tril(tm_out), tm=tm_out)

    counts = cnt[0, :N_EXPERTS].astype(I32)
    padded = (counts + rb - 1) // rb * rb
    pad_ends = jnp.cumsum(padded)
    pad_starts = pad_ends - padded
    n_assign = t * TOP_K
    nblk = n_assign // rb + N_EXPERTS
    eids = jnp.arange(N_EXPERTS, dtype=I32)
    dest = rk[:, :TOP_K] + jnp.sum(jnp.where(ti[:, :TOP_K, None] == eids, pad_starts, 0), axis=-1)
    dest2 = dest.reshape(t // tm_row, 1, tm_row * TOP_K)
    n_used = (pad_ends[-1] // rb).astype(I32).reshape(1)
    blk_start = jnp.minimum(jnp.arange(nblk, dtype=I32), n_used[0] - 1) * rb
    block_e = jnp.sum((blk_start[:, None] >= pad_ends[None, :]).astype(I32), axis=1)

    xb = _scatter(dest2, hp, jnp.zeros((nblk * rb, d // 2), U32), tm=tm_row)
    yb = _experts(block_e, n_used, xb, w_gate_up, b_gate_up.reshape(N_EXPERTS, 1, -1),
                  w_down, b_down.reshape(N_EXPERTS, 1, -1), rb=rb)
    out = _combine(dest2, x1, tg, yb, tm=tm_row)
    return out.reshape(bsz, seq, d)


def kernel(x, norm1_g, w_in, sb_q_norm_g, sb_k_norm_g, sb_out_norm_g, hg_lb_logits, hg_out_norm_g,
           w_out, norm2_g, w_router, b_router, w_gate_up, b_gate_up, w_down, b_down):
    depth = w_in.shape[0]
    lb_all = jnp.cumsum(jax.nn.softmax(hg_lb_logits.astype(F32), axis=0), axis=0)
    for l in range(depth):
        x = _layer(x, norm1_g[l], w_in[l], sb_q_norm_g[l], sb_k_norm_g[l], sb_out_norm_g[l], lb_all[l],
                   hg_out_norm_g[l], w_out[l], norm2_g[l], w_router[l], b_router[l],
                   w_gate_up[l], b_gate_up[l], w_down[l], b_down[l],
                   tm_in=512, tq=256, tm_out=512, tm_row=256, rb=512)
    return x
```

```python
import functools

import jax
import jax.numpy as jnp
from jax import lax
from jax.experimental import pallas as pl
from jax.experimental.pallas import tpu as pltpu

F32 = jnp.float32
BF16 = jnp.bfloat16
I32 = jnp.int32
U32 = jnp.uint32

EPS = 1e-6
LANES = 128
SB_HEAD_DIM = 64
SB_WIDTH = 512
HG_HEAD_DIM = 128
HG_WIDTH = 512
HG_CHUNK = 32
N_EXPERTS = 32
TOP_K = 4
SWIGLU_LIMIT = 7.0
SWIGLU_ALPHA = 1.702
VMEM_LIMIT = 56 * 1024 * 1024

NT_DIMS = (((1,), (1,)), ((), ()))
NPAIR = SB_WIDTH // LANES
LOG2E = 1.4426950408889634


def _cparams(*sem):
    return pltpu.CompilerParams(dimension_semantics=sem, vmem_limit_bytes=VMEM_LIMIT)


def _bf16_split(a, n):
    parts = []
    for _ in range(n - 1):
        p = a.astype(BF16)
        parts.append(p)
        a = a - p.astype(F32)
    parts.append(a.astype(BF16))
    return parts


def _sigmoid(a):
    return 1.0 / (1.0 + jnp.exp(-a))


def _pack_bf16_pairs(a):
    w = a.shape[1] // 2
    lo_bits = pltpu.bitcast(a[:, :w].astype(BF16).astype(F32), U32)
    hi_bits = pltpu.bitcast(a[:, w:].astype(BF16).astype(F32), U32)
    return (hi_bits & jnp.uint32(0xFFFF0000)) | (lo_bits >> 16)


def _unpack_bf16_pairs(p):
    return pltpu.bitcast(p << 16, F32), pltpu.bitcast(p & jnp.uint32(0xFFFF0000), F32)


ROW_GROUPS = 4


def _store_interleaved(ref, a):
    rows = a.shape[0]
    for j in range(ROW_GROUPS):
        ref[pl.ds(j, rows, stride=ROW_GROUPS), :] = a[:, j * LANES:(j + 1) * LANES]


def _load_interleaved(ref):
    rows = ref.shape[0] // ROW_GROUPS
    return jnp.concatenate([ref[pl.ds(j, rows, stride=ROW_GROUPS), :] for j in range(ROW_GROUPS)], axis=1)


def _inproj_kernel(x_ref, g1_ref, w_ref, gq_ref, gk_ref, lb_ref, bd_ref,
                   qkv_ref, hq_ref, hg_ref, hk_ref, hi_ref, sg_ref):
    x = x_ref[...]
    ms = jnp.mean(x * x, axis=-1, keepdims=True)
    h = (x * lax.rsqrt(ms + EPS) * g1_ref[...]).astype(BF16)
    bd = bd_ref[...]

    def seg(j):
        return jnp.dot(h, w_ref[:, j * 512:(j + 1) * 512], preferred_element_type=F32)

    def head_norm(a, g, scale):
        outs = []
        for c in range(2):
            ac = a[:, c * 256:(c + 1) * 256]
            ss = jnp.dot((ac * ac).astype(BF16), bd, preferred_element_type=F32)
            outs.append(ac * lax.rsqrt(ss * (1.0 / SB_HEAD_DIM) + EPS))
        y = jnp.concatenate(outs, axis=1) * g
        return y * scale if scale != 1.0 else y

    qkv_ref[:, 0:512] = head_norm(seg(0), gq_ref[...], LOG2E * SB_HEAD_DIM ** -0.5).astype(BF16)
    qkv_ref[:, 512:1024] = head_norm(seg(1), gk_ref[...], 1.0).astype(BF16)
    qkv_ref[:, 1024:1536] = seg(2).astype(BF16)
    hq_ref[...] = seg(3).astype(BF16)
    f = seg(4)
    lb = lb_ref[...]
    hg_ref[...] = jnp.log(lb + (1.0 - lb) * _sigmoid(f))
    hk_ref[...] = (1.0 - lb) * _sigmoid(-f)
    hi_ref[...] = seg(5).astype(BF16)
    gate = seg(6)
    sg_ref[...] = (gate * _sigmoid(gate)).astype(BF16)


def _inproj(x2, g1, w_bf, gq, gk, lb, bd, *, tm):
    t, d = x2.shape
    n = w_bf.shape[1]
    row = lambda i: (i, 0)
    fix = lambda i: (0, 0)
    return pl.pallas_call(
        _inproj_kernel,
        grid=(t // tm,),
        in_specs=[
            pl.BlockSpec((tm, d), row),
            pl.BlockSpec((1, d), fix),
            pl.BlockSpec((d, n), fix),
            pl.BlockSpec((1, 512), fix),
            pl.BlockSpec((1, 512), fix),
            pl.BlockSpec((1, 512), fix),
            pl.BlockSpec((256, 256), fix),
        ],
        out_specs=[
            pl.BlockSpec((tm, 1536), row),
            pl.BlockSpec((tm, 512), row),
            pl.BlockSpec((tm, 512), row),
            pl.BlockSpec((tm, 512), row),
            pl.BlockSpec((tm, 512), row),
            pl.BlockSpec((tm, 512), row),
        ],
        out_shape=[
            jax.ShapeDtypeStruct((t, 1536), BF16),
            jax.ShapeDtypeStruct((t, 512), BF16),
            jax.ShapeDtypeStruct((t, 512), F32),
            jax.ShapeDtypeStruct((t, 512), F32),
            jax.ShapeDtypeStruct((t, 512), BF16),
            jax.ShapeDtypeStruct((t, 512), BF16),
        ],
        compiler_params=_cparams("parallel"),
        name="inproj",
    )(x2, g1, w_bf, gq, gk, lb, bd)


def _sbattn_kernel(q_ref, k_ref, v_ref, go_ref, u2_ref, bd_ref, o_ref, vm_ref, *scr, tq):
    acc_ref = scr[0:NPAIR]
    carry_ref = scr[NPAIR:2 * NPAIR]
    z_ref = scr[2 * NPAIR:3 * NPAIR]
    w_ref = scr[3 * NPAIR:4 * NPAIR]
    qi = pl.program_id(1)
    hps = range(NPAIR)
    lane = lax.broadcasted_iota(I32, (tq, LANES), 1)
    lo = lane < SB_HEAD_DIM

    @pl.when(qi == 0)
    def _():
        v = v_ref[...]
        lo_all = (lax.broadcasted_iota(I32, v.shape, 1) & (LANES - 1)) < SB_HEAD_DIM
        zv = jnp.zeros_like(v)
        vm_ref[0] = jnp.where(lo_all, v, zv)
        vm_ref[1] = jnp.where(lo_all, zv, v)

    u2 = u2_ref[...]
    for hp in hps:
        acc_ref[hp][...] = jnp.zeros_like(acc_ref[hp])
        carry_ref[hp][...] = jnp.zeros_like(carry_ref[hp])
    row = lax.broadcasted_iota(I32, (2 * tq, tq), 0)
    col = lax.broadcasted_iota(I32, (2 * tq, tq), 1)
    strict = col < jnp.where(row >= tq, row - tq, row)
    q2s = []
    for hp in hps:
        q = q_ref[:, hp * LANES:(hp + 1) * LANES]
        zq = jnp.zeros_like(q)
        q2s.append(jnp.concatenate([jnp.where(lo, q, zq), jnp.where(lo, zq, q)], axis=0))

    def scores(kb):
        off = pl.multiple_of(kb * tq, tq)
        return [lax.dot_general(q2s[hp], k_ref[pl.ds(off, tq), hp * LANES:(hp + 1) * LANES], NT_DIMS,
                                preferred_element_type=F32) for hp in hps]

    def weighted_values(kb):
        off = pl.multiple_of(kb * tq, tq)
        for hp in hps:
            vcat = jnp.concatenate([vm_ref[0, pl.ds(off, tq), hp * LANES:(hp + 1) * LANES],
                                    vm_ref[1, pl.ds(off, tq), hp * LANES:(hp + 1) * LANES]], axis=0)
            acc_ref[hp][...] += jnp.dot(w_ref[hp][...], vcat, preferred_element_type=F32)

    def weights(zs, masked):
        lgs = []
        for z in zs:
            nz = -z
            lg = jnp.minimum(nz, 0.0) - jnp.log2(1.0 + jnp.exp2(jnp.minimum(z, nz)))
            if masked:
                lg = jnp.where(strict, lg, 0.0)
            lgs.append(lg)
        withins = []
        for pr in range(NPAIR // 2):
            cat = jnp.concatenate([lgs[2 * pr], lgs[2 * pr + 1]], axis=1).astype(BF16)
            res = jnp.dot(cat, u2, preferred_element_type=F32)
            withins += [res[:, :tq], res[:, tq:]]
        reps = tq // LANES
        carries = [jnp.concatenate([carry_ref[hp][...]] * reps, axis=1) for hp in hps]
        ws = [jnp.exp2((zs[hp] + lgs[hp]) + (carries[hp] + withins[hp])) for hp in hps]
        if masked:
            ws = [jnp.where(strict, w, 0.0) for w in ws]
        for hp in hps:
            carry_ref[hp][...] += jnp.sum(lgs[hp], axis=-1, keepdims=True)
        return [jnp.concatenate([w[:tq], w[tq:]], axis=1).astype(BF16) for w in ws]

    ws = weights(scores(qi), True)
    zn = scores(jnp.maximum(qi - 1, 0))
    for hp in hps:
        w_ref[hp][...] = ws[hp]
        z_ref[hp][...] = zn[hp]

    def body(it, c):
        kb = qi - 1 - it
        zs = [z_ref[hp][...] for hp in hps]
        zn = scores(jnp.maximum(kb - 1, 0))
        weighted_values(kb + 1)
        ws = weights(zs, False)
        for hp in hps:
            w_ref[hp][...] = ws[hp]
            z_ref[hp][...] = zn[hp]
        return c

    lax.fori_loop(0, qi, body, 0)
    weighted_values(0)
    for hp in hps:
        a = acc_ref[hp][...]
        ss = jnp.dot((a * a).astype(BF16), bd_ref[...], preferred_element_type=F32)
        o_ref[:, hp * LANES:(hp + 1) * LANES] = (
            a * lax.rsqrt(ss * (1.0 / SB_HEAD_DIM) + EPS) * go_ref[:, hp * LANES:(hp + 1) * LANES]).astype(BF16)


def _sbattn(qkv, go, u2, bd, *, tq):
    b, s, _ = qkv.shape
    return pl.pallas_call(
        functools.partial(_sbattn_kernel, tq=tq),
        grid=(b, s // tq),
        in_specs=[
            pl.BlockSpec((None, tq, SB_WIDTH), lambda bi, qi: (bi, qi, 0)),
            pl.BlockSpec((None, s, SB_WIDTH), lambda bi, qi: (bi, 0, 1)),
            pl.BlockSpec((None, s, SB_WIDTH), lambda bi, qi: (bi, 0, 2)),
            pl.BlockSpec((1, SB_WIDTH), lambda bi, qi: (0, 0)),
            pl.BlockSpec((2 * tq, 2 * tq), lambda bi, qi: (0, 0)),
            pl.BlockSpec((LANES, LANES), lambda bi, qi: (0, 0)),
        ],
        out_specs=pl.BlockSpec((None, tq, SB_WIDTH), lambda bi, qi: (bi, qi, 0)),
        out_shape=jax.ShapeDtypeStruct((b, s, SB_WIDTH), BF16),
        scratch_shapes=([pltpu.VMEM((2, s, SB_WIDTH), BF16)]
                        + [pltpu.VMEM((tq, LANES), F32)] * NPAIR
                        + [pltpu.VMEM((2 * tq, LANES), F32)] * NPAIR
                        + [pltpu.VMEM((2 * tq, tq), F32)] * NPAIR
                        + [pltpu.VMEM((tq, 2 * tq), BF16)] * NPAIR),
        compiler_params=_cparams("parallel", "arbitrary"),
        name="sbattn",
    )(qkv, qkv, qkv, go, u2, bd)


def _hgrn_kernel(q_ref, g_ref, k_ref, v_ref, sg_ref, gn_ref, tl_ref, o_ref, *state_ref):
    s = q_ref.shape[0]
    grp = LANES
    nch = grp // HG_CHUNK
    hs = range(HG_WIDTH // HG_HEAD_DIM)
    for h in hs:
        state_ref[h][...] = jnp.zeros_like(state_ref[h])
    tl = tl_ref[...]
    rr = lax.broadcasted_iota(I32, (grp, grp), 0)
    cc = lax.broadcasted_iota(I32, (grp, grp), 1)
    shift = HG_CHUNK.bit_length() - 1
    rchunk = rr >> shift
    causal = (rchunk == (cc >> shift)) & (cc <= rr)
    inchunk = [rchunk == ci for ci in range(nch)]

    def group(r, c):
        off = pl.multiple_of(r * grp, grp)
        col = lambda h: slice(h * HG_HEAD_DIM, (h + 1) * HG_HEAD_DIM)
        rows = pl.ds(off, grp)
        vs = [v_ref[rows, col(h)] for h in hs]
        bs = [sum(jnp.dot(tl, p, preferred_element_type=F32) for p in _bf16_split(g_ref[rows, col(h)], 2))
              for h in hs]
        lasts = [[b[(ci + 1) * HG_CHUNK - 1:(ci + 1) * HG_CHUNK, :] for ci in range(nch)] for b in bs]
        bls = [jnp.concatenate([jnp.broadcast_to(l, (HG_CHUNK, HG_HEAD_DIM)) for l in last], axis=0)
               for last in lasts]
        qes = [(q_ref[rows, col(h)].astype(F32) * jnp.exp(bs[h])).astype(BF16) for h in hs]
        kes = [(k_ref[rows, col(h)] * jnp.exp(-bs[h])).astype(BF16) for h in hs]
        kds = [(k_ref[rows, col(h)] * jnp.exp(bls[h] - bs[h])).astype(BF16) for h in hs]
        attns = [jnp.where(causal, lax.dot_general(qes[h], kes[h], NT_DIMS, preferred_element_type=F32), 0.0)
                 .astype(BF16) for h in hs]
        os = [jnp.dot(attns[h], vs[h], preferred_element_type=F32) for h in hs]
        vts = [vs[h].astype(F32).T.astype(BF16) for h in hs]
        uts = []
        for h in hs:
            zk = jnp.zeros_like(kds[h])
            kdx = jnp.concatenate([jnp.where(inchunk[ci], kds[h], zk) for ci in range(nch)], axis=1)
            uts.append(jnp.dot(vts[h], kdx, preferred_element_type=F32))
        for h in hs:
            st = state_ref[h][...]
            parts = []
            for ci in range(nch):
                parts.append(st.astype(BF16))
                st = st * jnp.exp(lasts[h][ci]) + uts[h][:, ci * HG_HEAD_DIM:(ci + 1) * HG_HEAD_DIM]
            state_ref[h][...] = st
            zq = jnp.zeros_like(qes[h])
            qx = jnp.concatenate([jnp.where(inchunk[ci], qes[h], zq) for ci in range(nch)], axis=1)
            o = os[h] + lax.dot_general(qx, jnp.concatenate(parts, axis=1), NT_DIMS, preferred_element_type=F32)
            ms = jnp.mean(o * o, axis=-1, keepdims=True)
            y = o * lax.rsqrt(ms + EPS) * gn_ref[:, col(h)]
            o_ref[rows, col(h)] = (y * sg_ref[rows, col(h)].astype(F32)).astype(BF16)
        return c

    lax.fori_loop(0, s // grp, group, 0, unroll=2)


def _hgrn(hq, hg, hk, hi, sg, gn, tl):
    b, s, w = hq.shape
    nh = HG_WIDTH // HG_HEAD_DIM
    blk = pl.BlockSpec((None, s, w), lambda bi: (bi, 0, 0))
    return pl.pallas_call(
        _hgrn_kernel,
        grid=(b,),
        in_specs=[blk, blk, blk, blk, blk,
                  pl.BlockSpec((1, w), lambda bi: (0, 0)),
                  pl.BlockSpec((LANES, LANES), lambda bi: (0, 0))],
        out_specs=blk,
        out_shape=jax.ShapeDtypeStruct((b, s, w), BF16),
        scratch_shapes=[pltpu.VMEM((HG_HEAD_DIM, HG_HEAD_DIM), F32)] * nh,
        compiler_params=_cparams("parallel"),
        name="hgrn",
    )(hq, hg, hk, hi, sg, gn, tl)


def _outproj_kernel(x_ref, a_ref, r_ref, wo_ref, g2_ref, wr_ref, br_ref, tri_ref,
                    x1_ref, hp_ref, ti_ref, tg_ref, rk_ref, cnt_ref, run_ref):
    i = pl.program_id(0)

    @pl.when(i == 0)
    def _():
        run_ref[...] = jnp.zeros_like(run_ref)

    half = a_ref.shape[1]
    x1 = (x_ref[...]
          + jnp.dot(a_ref[...], wo_ref[:half, :], preferred_element_type=F32)
          + jnp.dot(r_ref[...], wo_ref[half:, :], preferred_element_type=F32))
    x1_ref[...] = x1
    ms = jnp.mean(x1 * x1, axis=-1, keepdims=True)
    h2 = x1 * lax.rsqrt(ms + EPS) * g2_ref[...]
    _store_interleaved(hp_ref, _pack_bf16_pairs(h2))

    h_hi, h_lo = _bf16_split(h2, 2)
    w_hi, w_lo = _bf16_split(wr_ref[...], 2)
    logits = (jnp.dot(h_hi, w_hi, preferred_element_type=F32)
              + jnp.dot(h_lo, w_hi, preferred_element_type=F32)
              + jnp.dot(h_hi, w_lo, preferred_element_type=F32)) + br_ref[...]
    tm = logits.shape[0]
    lane = lax.broadcasted_iota(I32, (tm, LANES), 1).astype(F32)
    neg = jnp.float32(-jnp.inf)
    ti = jnp.zeros((tm, LANES), F32)
    tv = jnp.zeros((tm, LANES), F32)
    onehot = jnp.zeros((tm, LANES), F32)
    sels = []
    work = logits
    for k in range(TOP_K):
        m = jnp.max(work, axis=-1, keepdims=True)
        idx = jnp.min(jnp.where(work == m, lane, float(LANES)), axis=-1, keepdims=True)
        sel = lane == idx
        sels.append(sel)
        ti = jnp.where(lane == k, idx, ti)
        tv = jnp.where(lane == k, m, tv)
        onehot = jnp.where(sel, 1.0, onehot)
        work = jnp.where(sel, neg, work)
    valid = lane < TOP_K
    e = jnp.where(valid, jnp.exp(tv - jnp.max(jnp.where(valid, tv, neg), axis=-1, keepdims=True)), 0.0)
    tg_ref[...] = e / jnp.sum(e, axis=-1, keepdims=True)
    ti_ref[...] = ti.astype(I32)

    before = jnp.dot(tri_ref[...], onehot.astype(BF16), preferred_element_type=F32) + run_ref[...]
    rk = jnp.zeros((tm, LANES), F32)
    for k in range(TOP_K):
        rk = jnp.where(lane == k, jnp.sum(jnp.where(sels[k], before, 0.0), axis=-1, keepdims=True), rk)
    rk_ref[...] = rk.astype(I32)
    run_ref[...] += jnp.sum(onehot, axis=0, keepdims=True)
    cnt_ref[...] = run_ref[...]


def _outproj(x2, a2, r2, wo_bf, g2, wr, br, tri, *, tm):
    t, d = x2.shape
    half = a2.shape[1]
    row = lambda i: (i, 0)
    fix = lambda i: (0, 0)
    return pl.pallas_call(
        _outproj_kernel,
        grid=(t // tm,),
        in_specs=[
            pl.BlockSpec((tm, d), row),
            pl.BlockSpec((tm, half), row),
            pl.BlockSpec((tm, half), row),
            pl.BlockSpec((2 * half, d), fix),
            pl.BlockSpec((1, d), fix),
            pl.BlockSpec((d, LANES), fix),
            pl.BlockSpec((1, LANES), fix),
            pl.BlockSpec((tm, tm), fix),
        ],
        out_specs=[
            pl.BlockSpec((tm, d), row),
            pl.BlockSpec((ROW_GROUPS * tm, LANES), row),
            pl.BlockSpec((tm, LANES), row),
            pl.BlockSpec((tm, LANES), row),
            pl.BlockSpec((tm, LANES), row),
            pl.BlockSpec((1, LANES), fix),
        ],
        out_shape=[
            jax.ShapeDtypeStruct((t, d), F32),
            jax.ShapeDtypeStruct((ROW_GROUPS * t, LANES), U32),
            jax.ShapeDtypeStruct((t, LANES), I32),
            jax.ShapeDtypeStruct((t, LANES), F32),
            jax.ShapeDtypeStruct((t, LANES), I32),
            jax.ShapeDtypeStruct((1, LANES), F32),
        ],
        scratch_shapes=[pltpu.VMEM((1, LANES), F32)],
        compiler_params=_cparams("arbitrary"),
        name="outproj",
    )(x2, a2, r2, wo_bf, g2, wr, br, tri)


def _slab(ref, r):
    return ref.at[pl.ds(pl.multiple_of(ROW_GROUPS * r, ROW_GROUPS), ROW_GROUPS)]


def _slab_copy_wait(src, dst, sem, n):
    pltpu.make_async_copy(src.at[pl.ds(0, ROW_GROUPS * n)], dst.at[pl.ds(0, ROW_GROUPS * n)], sem).wait()


def _scatter_kernel(dest_ref, hp_ref, xz_ref, xb_ref, sem):
    del xz_ref
    tm = hp_ref.shape[0] // ROW_GROUPS
    for t in range(tm):
        for k in range(TOP_K):
            pltpu.make_async_copy(_slab(hp_ref, t), _slab(xb_ref, dest_ref[0, t * TOP_K + k]), sem).start()
    for _ in range(TOP_K):
        _slab_copy_wait(hp_ref, xb_ref, sem, tm)


def _scatter(dest2, hp, xzero, *, tm):
    t = hp.shape[0] // ROW_GROUPS
    return pl.pallas_call(
        _scatter_kernel,
        grid=(t // tm,),
        in_specs=[
            pl.BlockSpec((None, 1, tm * TOP_K), lambda i: (i, 0, 0), memory_space=pltpu.SMEM),
            pl.BlockSpec((ROW_GROUPS * tm, LANES), lambda i: (i, 0)),
            pl.BlockSpec(memory_space=pl.ANY),
        ],
        out_specs=pl.BlockSpec(memory_space=pl.ANY),
        out_shape=jax.ShapeDtypeStruct(xzero.shape, xzero.dtype),
        scratch_shapes=[pltpu.SemaphoreType.DMA],
        input_output_aliases={2: 0},
        compiler_params=_cparams("arbitrary"),
        name="scatter",
    )(dest2, hp, xzero)


def _experts_kernel(be_ref, nu_ref, xb_ref, wgu_ref, bgu_ref, wd_ref, bd_ref, yb_ref, wgu_bf, wd_bf):
    i = pl.program_id(0)
    fresh = jnp.logical_or(i == 0, be_ref[i] != be_ref[jnp.maximum(i - 1, 0)])

    @pl.when(fresh)
    def _():
        wgu_bf[...] = wgu_ref[...].astype(BF16)
        wd_bf[...] = wd_ref[...].astype(BF16)

    @pl.when(i < nu_ref[0])
    def _():
        x_lo, x_hi = _unpack_bf16_pairs(_load_interleaved(xb_ref))
        x = jnp.concatenate([x_lo.astype(BF16), x_hi.astype(BF16)], axis=1)
        gu = jnp.dot(x, wgu_bf[...], preferred_element_type=F32) + bgu_ref[...]
        f = gu.shape[1] // 2
        g_lin = jnp.minimum(gu[:, :f], SWIGLU_LIMIT)
        u_lin = jnp.clip(gu[:, f:], -SWIGLU_LIMIT, SWIGLU_LIMIT)
        act = g_lin * _sigmoid(SWIGLU_ALPHA * g_lin) * (u_lin + 1.0)
        y = jnp.dot(act.astype(BF16), wd_bf[...], preferred_element_type=F32) + bd_ref[...]
        _store_interleaved(yb_ref, _pack_bf16_pairs(y))

    @pl.when(i >= nu_ref[0])
    def _():
        yb_ref[...] = jnp.zeros_like(yb_ref)


def _experts(block_e, n_used, xb, w_gate_up, b_gate_up3, w_down, b_down3, *, rb):
    n_rows = xb.shape[0] // ROW_GROUPS
    d = w_gate_up.shape[1]
    f2 = w_gate_up.shape[2]
    f = w_down.shape[1]
    nblk = n_rows // rb
    rowblk = lambda i, be, nu: (jnp.minimum(i, nu[0] - 1), 0)
    wsel = lambda i, be, nu: (be[i], 0, 0)
    grid_spec = pltpu.PrefetchScalarGridSpec(
        num_scalar_prefetch=2,
        grid=(nblk,),
        in_specs=[
            pl.BlockSpec((ROW_GROUPS * rb, LANES), rowblk),
            pl.BlockSpec((None, d, f2), wsel),
            pl.BlockSpec((None, 1, f2), wsel),
            pl.BlockSpec((None, f, d), wsel),
            pl.BlockSpec((None, 1, d), wsel),
        ],
        out_specs=pl.BlockSpec((ROW_GROUPS * rb, LANES), lambda i, be, nu: (i, 0)),
        scratch_shapes=[pltpu.VMEM((d, f2), BF16), pltpu.VMEM((f, d), BF16)],
    )
    return pl.pallas_call(
        _experts_kernel,
        grid_spec=grid_spec,
        out_shape=jax.ShapeDtypeStruct(xb.shape, U32),
        compiler_params=_cparams("arbitrary"),
        name="experts",
    )(block_e, n_used, xb, w_gate_up, b_gate_up3, w_down, b_down3)


def _combine_kernel(dest0_ref, destn_ref, x1_ref, tg_ref, yb_ref, o_ref, buf, sem):
    i = pl.program_id(0)
    n = pl.num_programs(0)
    tm = x1_ref.shape[0]
    slot = i % 2

    def gather_one(dest_ref, s, t):
        for k in range(TOP_K):
            pltpu.make_async_copy(_slab(yb_ref, dest_ref[0, t * TOP_K + k]), _slab(buf.at[s, k], t),
                                  sem.at[s]).start()

    @pl.when(i == 0)
    def _():
        def body(t, c):
            gather_one(dest0_ref, 0, t)
            return c
        lax.fori_loop(0, tm, body, 0)

    @pl.when(i + 1 < n)
    def _():
        for t in range(tm):
            gather_one(destn_ref, 1 - slot, t)

    for k in range(TOP_K):
        _slab_copy_wait(yb_ref, buf.at[slot, k], sem.at[slot], tm)
    tg = tg_ref[...]
    half = x1_ref.shape[1] // 2
    for j in range(ROW_GROUPS):
        lo_cols = slice(j * LANES, (j + 1) * LANES)
        hi_cols = slice(half + j * LANES, half + (j + 1) * LANES)
        out_lo = x1_ref[:, lo_cols]
        out_hi = x1_ref[:, hi_cols]
        for k in range(TOP_K):
            y_lo, y_hi = _unpack_bf16_pairs(buf[slot, k, pl.ds(j, tm, stride=ROW_GROUPS), :])
            out_lo = out_lo + tg[:, k:k + 1] * y_lo
            out_hi = out_hi + tg[:, k:k + 1] * y_hi
        o_ref[:, lo_cols] = out_lo
        o_ref[:, hi_cols] = out_hi


def _combine(dest2, x1, tg, yb, *, tm):
    t, d = x1.shape
    n = t // tm
    return pl.pallas_call(
        _combine_kernel,
        grid=(n,),
        in_specs=[
            pl.BlockSpec((None, 1, tm * TOP_K), lambda i: (0, 0, 0), memory_space=pltpu.SMEM),
            pl.BlockSpec((None, 1, tm * TOP_K), lambda i: (jnp.minimum(i + 1, n - 1), 0, 0),
                         memory_space=pltpu.SMEM),
            pl.BlockSpec((tm, d), lambda i: (i, 0)),
            pl.BlockSpec((tm, LANES), lambda i: (i, 0)),
            pl.BlockSpec(memory_space=pl.ANY),
        ],
        out_specs=pl.BlockSpec((tm, d), lambda i: (i, 0)),
        out_shape=jax.ShapeDtypeStruct((t, d), F32),
        scratch_shapes=[pltpu.VMEM((2, TOP_K, ROW_GROUPS * tm, LANES), yb.dtype), pltpu.SemaphoreType.DMA((2,))],
        compiler_params=_cparams("arbitrary"),
        name="combine",
    )(dest2, dest2, x1, tg, yb)


def _block_ones(n, blk):
    i = jnp.arange(n)
    return (i[:, None] // blk == i[None, :] // blk).astype(BF16)


def _pair_suffix(tq):
    j = jnp.arange(2 * tq)
    return ((j[:, None] > j[None, :]) & (j[:, None] // tq == j[None, :] // tq)).astype(BF16)


def _chunk_tril(n, blk):
    i = jnp.arange(n)
    return ((i[:, None] // blk == i[None, :] // blk) & (i[None, :] <= i[:, None])).astype(BF16)


def _strict_tril(n):
    i = jnp.arange(n)
    return (i[None, :] < i[:, None]).astype(BF16)


def _layer(x, norm1_g, w_in, q_g, k_g, sbo_g, lb, hgo_g, w_out, norm2_g, w_router, b_router,
           w_gate_up, b_gate_up, w_down, b_down, *, tm_in, tq, tm_out, tm_row, rb):
    bsz, seq, d = x.shape
    t = bsz * seq
    x2 = x.reshape(t, d)
    nsb = SB_WIDTH // SB_HEAD_DIM

    qkv, hq, hg, hk, hi, sg = _inproj(
        x2, norm1_g.reshape(1, d), w_in.astype(BF16),
        jnp.tile(q_g, nsb).reshape(1, SB_WIDTH), jnp.tile(k_g, nsb).reshape(1, SB_WIDTH),
        lb.reshape(1, HG_WIDTH), _block_ones(256, SB_HEAD_DIM), tm=tm_in)

    a = _sbattn(qkv.reshape(bsz, seq, 3 * SB_WIDTH), sbo_g.reshape(1, SB_WIDTH),
                _pair_suffix(tq), _block_ones(LANES, SB_HEAD_DIM), tq=tq)
    sh = (bsz, seq, HG_WIDTH)
    r = _hgrn(hq.reshape(sh), hg.reshape(sh), hk.reshape(sh), hi.reshape(sh), sg.reshape(sh),
              hgo_g.reshape(1, HG_WIDTH), _chunk_tril(LANES, HG_CHUNK))

    wr = jnp.zeros((d, LANES), F32).at[:, :N_EXPERTS].set(w_router)
    br = jnp.full((1, LANES), -1e30, F32).at[0, :N_EXPERTS].set(b_router)
    x1, hp, ti, tg, rk, cnt = _outproj(
        x2, a.reshape(t, SB_WIDTH), r.reshape(t, HG_WIDTH), w_out.astype(BF16),
        norm2_g.reshape(1, d), wr, br, _strict_tril(tm_out), tm=tm_out)

    counts = cnt[0, :N_EXPERTS].astype(I32)
    padded = (counts + rb - 1) // rb * rb
    pad_ends = jnp.cumsum(padded)
    pad_starts = pad_ends - padded
    n_assign = t * TOP_K
    nblk = n_assign // rb + N_EXPERTS
    eids = jnp.arange(N_EXPERTS, dtype=I32)
    dest = rk[:, :TOP_K] + jnp.sum(jnp.where(ti[:, :TOP_K, None] == eids, pad_starts, 0), axis=-1)
    dest2 = dest.reshape(t // tm_row, 1, tm_row * TOP_K)
    n_used = (pad_ends[-1] // rb).astype(I32).reshape(1)
    blk_start = jnp.minimum(jnp.arange(nblk, dtype=I32), n_used[0] - 1) * rb
    block_e = jnp.sum((blk_start[:, None] >= pad_ends[None, :]).astype(I32), axis=1)

    xb = _scatter(dest2, hp, jnp.zeros((ROW_GROUPS * nblk * rb, LANES), U32), tm=tm_row)
    yb = _experts(block_e, n_used, xb, w_gate_up, b_gate_up.reshape(N_EXPERTS, 1, -1),
                  w_down, b_down.reshape(N_EXPERTS, 1, -1), rb=rb)
    out = _combine(dest2, x1, tg, yb, tm=tm_row)
    return out.reshape(bsz, seq, d)


def kernel(x, norm1_g, w_in, sb_q_norm_g, sb_k_norm_g, sb_out_norm_g, hg_lb_logits, hg_out_norm_g,
           w_out, norm2_g, w_router, b_router, w_gate_up, b_gate_up, w_down, b_down):
    depth = w_in.shape[0]
    lb_all = jnp.cumsum(jax.nn.softmax(hg_lb_logits.astype(F32), axis=0), axis=0)
    for l in range(depth):
        x = _layer(x, norm1_g[l], w_in[l], sb_q_norm_g[l], sb_k_norm_g[l], sb_out_norm_g[l], lb_all[l],
                   hg_out_norm_g[l], w_out[l], norm2_g[l], w_router[l], b_router[l],
                   w_gate_up[l], b_gate_up[l], w_down[l], b_down[l],
                   tm_in=512, tq=256, tm_out=512, tm_row=256, rb=512)
    return x
```

```python
import functools

import jax
import jax.numpy as jnp
from jax import lax
from jax.experimental import pallas as pl
from jax.experimental.pallas import tpu as pltpu

F32 = jnp.float32
BF16 = jnp.bfloat16
I32 = jnp.int32
U32 = jnp.uint32

EPS = 1e-6
LANES = 128
SB_HEAD_DIM = 64
SB_WIDTH = 512
HG_HEAD_DIM = 128
HG_WIDTH = 512
HG_CHUNK = 32
N_EXPERTS = 32
TOP_K = 4
SWIGLU_LIMIT = 7.0
SWIGLU_ALPHA = 1.702
VMEM_LIMIT = 56 * 1024 * 1024

NT_DIMS = (((1,), (1,)), ((), ()))
NPAIR = SB_WIDTH // LANES
LOG2E = 1.4426950408889634


def _cparams(*sem):
    return pltpu.CompilerParams(dimension_semantics=sem, vmem_limit_bytes=VMEM_LIMIT)


def _bf16_split(a, n):
    parts = []
    for _ in range(n - 1):
        p = a.astype(BF16)
        parts.append(p)
        a = a - p.astype(F32)
    parts.append(a.astype(BF16))
    return parts


def _sigmoid(a):
    return 1.0 / (1.0 + jnp.exp(-a))


def _pack_bf16_pairs(a):
    w = a.shape[1] // 2
    lo_bits = pltpu.bitcast(a[:, :w].astype(BF16).astype(F32), U32)
    hi_bits = pltpu.bitcast(a[:, w:].astype(BF16).astype(F32), U32)
    return (hi_bits & jnp.uint32(0xFFFF0000)) | (lo_bits >> 16)


def _unpack_bf16_pairs(p):
    return pltpu.bitcast(p << 16, F32), pltpu.bitcast(p & jnp.uint32(0xFFFF0000), F32)


ROW_GROUPS = 4


def _store_interleaved(ref, a):
    rows = a.shape[0]
    for j in range(ROW_GROUPS):
        ref[pl.ds(j, rows, stride=ROW_GROUPS), :] = a[:, j * LANES:(j + 1) * LANES]


def _load_interleaved(ref):
    rows = ref.shape[0] // ROW_GROUPS
    return jnp.concatenate([ref[pl.ds(j, rows, stride=ROW_GROUPS), :] for j in range(ROW_GROUPS)], axis=1)


def _inproj_kernel(x_ref, g1_ref, w_ref, gq_ref, gk_ref, lb_ref, bd_ref,
                   qkv_ref, hq_ref, hg_ref, hk_ref, hi_ref, sg_ref):
    x = x_ref[...]
    ms = jnp.mean(x * x, axis=-1, keepdims=True)
    h = (x * lax.rsqrt(ms + EPS) * g1_ref[...]).astype(BF16)
    bd = bd_ref[...]

    def seg(j):
        return jnp.dot(h, w_ref[:, j * 512:(j + 1) * 512], preferred_element_type=F32)

    def head_norm(a, g, scale):
        outs = []
        for c in range(2):
            ac = a[:, c * 256:(c + 1) * 256]
            ss = jnp.dot((ac * ac).astype(BF16), bd, preferred_element_type=F32)
            outs.append(ac * lax.rsqrt(ss * (1.0 / SB_HEAD_DIM) + EPS))
        y = jnp.concatenate(outs, axis=1) * g
        return y * scale if scale != 1.0 else y

    qkv_ref[:, 0:512] = head_norm(seg(0), gq_ref[...], LOG2E * SB_HEAD_DIM ** -0.5).astype(BF16)
    qkv_ref[:, 512:1024] = head_norm(seg(1), gk_ref[...], 1.0).astype(BF16)
    qkv_ref[:, 1024:1536] = seg(2).astype(BF16)
    hq_ref[...] = seg(3).astype(BF16)
    f = seg(4)
    lb = lb_ref[...]
    hg_ref[...] = jnp.log(lb + (1.0 - lb) * _sigmoid(f))
    hk_ref[...] = (1.0 - lb) * _sigmoid(-f)
    hi_ref[...] = seg(5).astype(BF16)
    gate = seg(6)
    sg_ref[...] = (gate * _sigmoid(gate)).astype(BF16)


def _inproj(x2, g1, w_bf, gq, gk, lb, bd, *, tm):
    t, d = x2.shape
    n = w_bf.shape[1]
    row = lambda i: (i, 0)
    fix = lambda i: (0, 0)
    return pl.pallas_call(
        _inproj_kernel,
        grid=(t // tm,),
        in_specs=[
            pl.BlockSpec((tm, d), row),
            pl.BlockSpec((1, d), fix),
            pl.BlockSpec((d, n), fix),
            pl.BlockSpec((1, 512), fix),
            pl.BlockSpec((1, 512), fix),
            pl.BlockSpec((1, 512), fix),
            pl.BlockSpec((256, 256), fix),
        ],
        out_specs=[
            pl.BlockSpec((tm, 1536), row),
            pl.BlockSpec((tm, 512), row),
            pl.BlockSpec((tm, 512), row),
            pl.BlockSpec((tm, 512), row),
            pl.BlockSpec((tm, 512), row),
            pl.BlockSpec((tm, 512), row),
        ],
        out_shape=[
            jax.ShapeDtypeStruct((t, 1536), BF16),
            jax.ShapeDtypeStruct((t, 512), BF16),
            jax.ShapeDtypeStruct((t, 512), F32),
            jax.ShapeDtypeStruct((t, 512), F32),
            jax.ShapeDtypeStruct((t, 512), BF16),
            jax.ShapeDtypeStruct((t, 512), BF16),
        ],
        compiler_params=_cparams("parallel"),
        name="inproj",
    )(x2, g1, w_bf, gq, gk, lb, bd)


def _sbattn_kernel(q_ref, k_ref, v_ref, go_ref, u2_ref, bd_ref, o_ref, vm_ref, *scr, tq):
    acc_ref = scr[0:NPAIR]
    carry_ref = scr[NPAIR:2 * NPAIR]
    z_ref = scr[2 * NPAIR:3 * NPAIR]
    w_ref = scr[3 * NPAIR:4 * NPAIR]
    qi = pl.program_id(1)
    hps = range(NPAIR)
    lane = lax.broadcasted_iota(I32, (tq, LANES), 1)
    lo = lane < SB_HEAD_DIM

    @pl.when(qi == 0)
    def _():
        v = v_ref[...]
        lo_all = (lax.broadcasted_iota(I32, v.shape, 1) & (LANES - 1)) < SB_HEAD_DIM
        zv = jnp.zeros_like(v)
        vm_ref[0] = jnp.where(lo_all, v, zv)
        vm_ref[1] = jnp.where(lo_all, zv, v)

    u2 = u2_ref[...]
    for hp in hps:
        acc_ref[hp][...] = jnp.zeros_like(acc_ref[hp])
        carry_ref[hp][...] = jnp.zeros_like(carry_ref[hp])
    row = lax.broadcasted_iota(I32, (2 * tq, tq), 0)
    col = lax.broadcasted_iota(I32, (2 * tq, tq), 1)
    strict = col < jnp.where(row >= tq, row - tq, row)
    q2s = []
    for hp in hps:
        q = q_ref[:, hp * LANES:(hp + 1) * LANES]
        zq = jnp.zeros_like(q)
        q2s.append(jnp.concatenate([jnp.where(lo, q, zq), jnp.where(lo, zq, q)], axis=0))

    def scores(kb):
        off = pl.multiple_of(kb * tq, tq)
        return [lax.dot_general(q2s[hp], k_ref[pl.ds(off, tq), hp * LANES:(hp + 1) * LANES], NT_DIMS,
                                preferred_element_type=F32) for hp in hps]

    def weighted_values(kb):
        off = pl.multiple_of(kb * tq, tq)
        for hp in hps:
            vcat = jnp.concatenate([vm_ref[0, pl.ds(off, tq), hp * LANES:(hp + 1) * LANES],
                                    vm_ref[1, pl.ds(off, tq), hp * LANES:(hp + 1) * LANES]], axis=0)
            acc_ref[hp][...] += jnp.dot(w_ref[hp][...], vcat, preferred_element_type=F32)

    def weights(zs, masked):
        lgs = []
        for z in zs:
            nz = -z
            lg = jnp.minimum(nz, 0.0) - jnp.log2(1.0 + jnp.exp2(jnp.minimum(z, nz)))
            if masked:
                lg = jnp.where(strict, lg, 0.0)
            lgs.append(lg)
        withins = []
        for pr in range(NPAIR // 2):
            cat = jnp.concatenate([lgs[2 * pr], lgs[2 * pr + 1]], axis=1).astype(BF16)
            res = jnp.dot(cat, u2, preferred_element_type=F32)
            withins += [res[:, :tq], res[:, tq:]]
        reps = tq // LANES
        carries = [jnp.concatenate([carry_ref[hp][...]] * reps, axis=1) for hp in hps]
        ws = [jnp.exp2((zs[hp] + lgs[hp]) + (carries[hp] + withins[hp])) for hp in hps]
        if masked:
            ws = [jnp.where(strict, w, 0.0) for w in ws]
        for hp in hps:
            carry_ref[hp][...] += jnp.sum(lgs[hp], axis=-1, keepdims=True)
        return [jnp.concatenate([w[:tq], w[tq:]], axis=1).astype(BF16) for w in ws]

    ws = weights(scores(qi), True)
    zn = scores(jnp.maximum(qi - 1, 0))
    for hp in hps:
        w_ref[hp][...] = ws[hp]
        z_ref[hp][...] = zn[hp]

    def body(it, c):
        kb = qi - 1 - it
        zs = [z_ref[hp][...] for hp in hps]
        zn = scores(jnp.maximum(kb - 1, 0))
        weighted_values(kb + 1)
        ws = weights(zs, False)
        for hp in hps:
            w_ref[hp][...] = ws[hp]
            z_ref[hp][...] = zn[hp]
        return c

    lax.fori_loop(0, qi, body, 0)
    weighted_values(0)
    for hp in hps:
        a = acc_ref[hp][...]
        ss = jnp.dot((a * a).astype(BF16), bd_ref[...], preferred_element_type=F32)
        o_ref[:, hp * LANES:(hp + 1) * LANES] = (
            a * lax.rsqrt(ss * (1.0 / SB_HEAD_DIM) + EPS) * go_ref[:, hp * LANES:(hp + 1) * LANES]).astype(BF16)


def _sbattn(qkv, go, u2, bd, *, tq):
    b, s, _ = qkv.shape
    return pl.pallas_call(
        functools.partial(_sbattn_kernel, tq=tq),
        grid=(b, s // tq),
        in_specs=[
            pl.BlockSpec((None, tq, SB_WIDTH), lambda bi, qi: (bi, qi, 0)),
            pl.BlockSpec((None, s, SB_WIDTH), lambda bi, qi: (bi, 0, 1)),
            pl.BlockSpec((None, s, SB_WIDTH), lambda bi, qi: (bi, 0, 2)),
            pl.BlockSpec((1, SB_WIDTH), lambda bi, qi: (0, 0)),
            pl.BlockSpec((2 * tq, 2 * tq), lambda bi, qi: (0, 0)),
            pl.BlockSpec((LANES, LANES), lambda bi, qi: (0, 0)),
        ],
        out_specs=pl.BlockSpec((None, tq, SB_WIDTH), lambda bi, qi: (bi, qi, 0)),
        out_shape=jax.ShapeDtypeStruct((b, s, SB_WIDTH), BF16),
        scratch_shapes=([pltpu.VMEM((2, s, SB_WIDTH), BF16)]
                        + [pltpu.VMEM((tq, LANES), F32)] * NPAIR
                        + [pltpu.VMEM((2 * tq, LANES), F32)] * NPAIR
                        + [pltpu.VMEM((2 * tq, tq), F32)] * NPAIR
                        + [pltpu.VMEM((tq, 2 * tq), BF16)] * NPAIR),
        compiler_params=_cparams("parallel", "arbitrary"),
        name="sbattn",
    )(qkv, qkv, qkv, go, u2, bd)


def _hgrn_kernel(q_ref, g_ref, k_ref, v_ref, sg_ref, gn_ref, tl_ref, o_ref, *state_ref):
    s = q_ref.shape[0]
    grp = LANES
    nch = grp // HG_CHUNK
    hs = range(HG_WIDTH // HG_HEAD_DIM)
    for h in hs:
        state_ref[h][...] = jnp.zeros_like(state_ref[h])
    tl = tl_ref[...]
    rr = lax.broadcasted_iota(I32, (grp, grp), 0)
    cc = lax.broadcasted_iota(I32, (grp, grp), 1)
    shift = HG_CHUNK.bit_length() - 1
    rchunk = rr >> shift
    causal = (rchunk == (cc >> shift)) & (cc <= rr)
    inchunk = [rchunk == ci for ci in range(nch)]

    def group(r, c):
        off = pl.multiple_of(r * grp, grp)
        col = lambda h: slice(h * HG_HEAD_DIM, (h + 1) * HG_HEAD_DIM)
        rows = pl.ds(off, grp)
        vs = [v_ref[rows, col(h)] for h in hs]
        bs = [sum(jnp.dot(tl, p, preferred_element_type=F32) for p in _bf16_split(g_ref[rows, col(h)], 2))
              for h in hs]
        lasts = [[b[(ci + 1) * HG_CHUNK - 1:(ci + 1) * HG_CHUNK, :] for ci in range(nch)] for b in bs]
        bls = [jnp.concatenate([jnp.broadcast_to(l, (HG_CHUNK, HG_HEAD_DIM)) for l in last], axis=0)
               for last in lasts]
        qes = [(q_ref[rows, col(h)].astype(F32) * jnp.exp(bs[h])).astype(BF16) for h in hs]
        kes = [(k_ref[rows, col(h)] * jnp.exp(-bs[h])).astype(BF16) for h in hs]
        kds = [(k_ref[rows, col(h)] * jnp.exp(bls[h] - bs[h])).astype(BF16) for h in hs]
        attns = [jnp.where(causal, lax.dot_general(qes[h], kes[h], NT_DIMS, preferred_element_type=F32), 0.0)
                 .astype(BF16) for h in hs]
        os = [jnp.dot(attns[h], vs[h], preferred_element_type=F32) for h in hs]
        vts = [vs[h].astype(F32).T.astype(BF16) for h in hs]
        uts = []
        for h in hs:
            zk = jnp.zeros_like(kds[h])
            kdx = jnp.concatenate([jnp.where(inchunk[ci], kds[h], zk) for ci in range(nch)], axis=1)
            uts.append(jnp.dot(vts[h], kdx, preferred_element_type=F32))
        for h in hs:
            st = state_ref[h][...]
            parts = []
            for ci in range(nch):
                parts.append(st.astype(BF16))
                st = st * jnp.exp(lasts[h][ci]) + uts[h][:, ci * HG_HEAD_DIM:(ci + 1) * HG_HEAD_DIM]
            state_ref[h][...] = st
            zq = jnp.zeros_like(qes[h])
            qx = jnp.concatenate([jnp.where(inchunk[ci], qes[h], zq) for ci in range(nch)], axis=1)
            o = os[h] + lax.dot_general(qx, jnp.concatenate(parts, axis=1), NT_DIMS, preferred_element_type=F32)
            ms = jnp.mean(o * o, axis=-1, keepdims=True)
            y = o * lax.rsqrt(ms + EPS) * gn_ref[:, col(h)]
            o_ref[rows, col(h)] = (y * sg_ref[rows, col(h)].astype(F32)).astype(BF16)
        return c

    lax.fori_loop(0, s // grp, group, 0, unroll=2)


def _hgrn(hq, hg, hk, hi, sg, gn, tl):
    b, s, w = hq.shape
    nh = HG_WIDTH // HG_HEAD_DIM
    blk = pl.BlockSpec((None, s, w), lambda bi: (bi, 0, 0))
    return pl.pallas_call(
        _hgrn_kernel,
        grid=(b,),
        in_specs=[blk, blk, blk, blk, blk,
                  pl.BlockSpec((1, w), lambda bi: (0, 0)),
                  pl.BlockSpec((LANES, LANES), lambda bi: (0, 0))],
        out_specs=blk,
        out_shape=jax.ShapeDtypeStruct((b, s, w), BF16),
        scratch_shapes=[pltpu.VMEM((HG_HEAD_DIM, HG_HEAD_DIM), F32)] * nh,
        compiler_params=_cparams("parallel"),
        name="hgrn",
    )(hq, hg, hk, hi, sg, gn, tl)


def _outproj_kernel(x_ref, a_ref, r_ref, wo_ref, g2_ref, wr_ref, br_ref, tri_ref,
                    x1_ref, hp_ref, ti_ref, tg_ref, rk_ref, cnt_ref, run_ref):
    i = pl.program_id(0)

    @pl.when(i == 0)
    def _():
        run_ref[...] = jnp.zeros_like(run_ref)

    half = a_ref.shape[1]
    x1 = (x_ref[...]
          + jnp.dot(a_ref[...], wo_ref[:half, :], preferred_element_type=F32)
          + jnp.dot(r_ref[...], wo_ref[half:, :], preferred_element_type=F32))
    x1_ref[...] = x1
    ms = jnp.mean(x1 * x1, axis=-1, keepdims=True)
    h2 = x1 * lax.rsqrt(ms + EPS) * g2_ref[...]
    _store_interleaved(hp_ref, _pack_bf16_pairs(h2))

    h_hi, h_lo = _bf16_split(h2, 2)
    w_hi, w_lo = _bf16_split(wr_ref[...], 2)
    logits = (jnp.dot(h_hi, w_hi, preferred_element_type=F32)
              + jnp.dot(h_lo, w_hi, preferred_element_type=F32)
              + jnp.dot(h_hi, w_lo, preferred_element_type=F32)) + br_ref[...]
    tm = logits.shape[0]
    lane = lax.broadcasted_iota(I32, (tm, LANES), 1).astype(F32)
    neg = jnp.float32(-jnp.inf)
    ti = jnp.zeros((tm, LANES), F32)
    tv = jnp.zeros((tm, LANES), F32)
    onehot = jnp.zeros((tm, LANES), F32)
    sels = []
    work = logits
    for k in range(TOP_K):
        m = jnp.max(work, axis=-1, keepdims=True)
        idx = jnp.min(jnp.where(work == m, lane, float(LANES)), axis=-1, keepdims=True)
        sel = lane == idx
        sels.append(sel)
        ti = jnp.where(lane == k, idx, ti)
        tv = jnp.where(lane == k, m, tv)
        onehot = jnp.where(sel, 1.0, onehot)
        work = jnp.where(sel, neg, work)
    valid = lane < TOP_K
    e = jnp.where(valid, jnp.exp(tv - jnp.max(jnp.where(valid, tv, neg), axis=-1, keepdims=True)), 0.0)
    tg_ref[...] = e / jnp.sum(e, axis=-1, keepdims=True)
    ti_ref[...] = ti.astype(I32)

    before = jnp.dot(tri_ref[...], onehot.astype(BF16), preferred_element_type=F32) + run_ref[...]
    rk = jnp.zeros((tm, LANES), F32)
    for k in range(TOP_K):
        rk = jnp.where(lane == k, jnp.sum(jnp.where(sels[k], before, 0.0), axis=-1, keepdims=True), rk)
    rk_ref[...] = rk.astype(I32)
    run_ref[...] += jnp.sum(onehot, axis=0, keepdims=True)
    cnt_ref[...] = run_ref[...]


def _outproj(x2, a2, r2, wo_bf, g2, wr, br, tri, *, tm):
    t, d = x2.shape
    half = a2.shape[1]
    row = lambda i: (i, 0)
    fix = lambda i: (0, 0)
    return pl.pallas_call(
        _outproj_kernel,
        grid=(t // tm,),
        in_specs=[
            pl.BlockSpec((tm, d), row),
            pl.BlockSpec((tm, half), row),
            pl.BlockSpec((tm, half), row),
            pl.BlockSpec((2 * half, d), fix),
            pl.BlockSpec((1, d), fix),
            pl.BlockSpec((d, LANES), fix),
            pl.BlockSpec((1, LANES), fix),
            pl.BlockSpec((tm, tm), fix),
        ],
        out_specs=[
            pl.BlockSpec((tm, d), row),
            pl.BlockSpec((ROW_GROUPS * tm, LANES), row),
            pl.BlockSpec((tm, LANES), row),
            pl.BlockSpec((tm, LANES), row),
            pl.BlockSpec((tm, LANES), row),
            pl.BlockSpec((1, LANES), fix),
        ],
        out_shape=[
            jax.ShapeDtypeStruct((t, d), F32),
            jax.ShapeDtypeStruct((ROW_GROUPS * t, LANES), U32),
            jax.ShapeDtypeStruct((t, LANES), I32),
            jax.ShapeDtypeStruct((t, LANES), F32),
            jax.ShapeDtypeStruct((t, LANES), I32),
            jax.ShapeDtypeStruct((1, LANES), F32),
        ],
        scratch_shapes=[pltpu.VMEM((1, LANES), F32)],
        compiler_params=_cparams("arbitrary"),
        name="outproj",
    )(x2, a2, r2, wo_bf, g2, wr, br, tri)


def _slab(ref, r):
    return ref.at[pl.ds(pl.multiple_of(ROW_GROUPS * r, ROW_GROUPS), ROW_GROUPS)]


def _slab_copy_wait(src, dst, sem, n):
    pltpu.make_async_copy(src.at[pl.ds(0, ROW_GROUPS * n)], dst.at[pl.ds(0, ROW_GROUPS * n)], sem).wait()


def _scatter_kernel(dest_ref, hp_ref, xz_ref, xb_ref, sem):
    del xz_ref
    tm = hp_ref.shape[0] // ROW_GROUPS
    for t in range(tm):
        for k in range(TOP_K):
            pltpu.make_async_copy(_slab(hp_ref, t), _slab(xb_ref, dest_ref[0, t * TOP_K + k]),
                                  sem).start(priority=k % 2)
    for _ in range(TOP_K):
        _slab_copy_wait(hp_ref, xb_ref, sem, tm)


def _scatter(dest2, hp, xzero, *, tm):
    t = hp.shape[0] // ROW_GROUPS
    return pl.pallas_call(
        _scatter_kernel,
        grid=(t // tm,),
        in_specs=[
            pl.BlockSpec((None, 1, tm * TOP_K), lambda i: (i, 0, 0), memory_space=pltpu.SMEM),
            pl.BlockSpec((ROW_GROUPS * tm, LANES), lambda i: (i, 0)),
            pl.BlockSpec(memory_space=pl.ANY),
        ],
        out_specs=pl.BlockSpec(memory_space=pl.ANY),
        out_shape=jax.ShapeDtypeStruct(xzero.shape, xzero.dtype),
        scratch_shapes=[pltpu.SemaphoreType.DMA],
        input_output_aliases={2: 0},
        compiler_params=_cparams("arbitrary"),
        name="scatter",
    )(dest2, hp, xzero)


def _experts_kernel(be_ref, nu_ref, xb_ref, wgu_ref, bgu_ref, wd_ref, bd_ref, yb_ref, wgu_bf, wd_bf):
    i = pl.program_id(0)
    fresh = jnp.logical_or(i == 0, be_ref[i] != be_ref[jnp.maximum(i - 1, 0)])

    @pl.when(fresh)
    def _():
        wgu_bf[...] = wgu_ref[...].astype(BF16)
        wd_bf[...] = wd_ref[...].astype(BF16)

    @pl.when(i < nu_ref[0])
    def _():
        x_lo, x_hi = _unpack_bf16_pairs(_load_interleaved(xb_ref))
        x = jnp.concatenate([x_lo.astype(BF16), x_hi.astype(BF16)], axis=1)
        gu = jnp.dot(x, wgu_bf[...], preferred_element_type=F32) + bgu_ref[...]
        f = gu.shape[1] // 2
        g_lin = jnp.minimum(gu[:, :f], SWIGLU_LIMIT)
        u_lin = jnp.clip(gu[:, f:], -SWIGLU_LIMIT, SWIGLU_LIMIT)
        act = g_lin * _sigmoid(SWIGLU_ALPHA * g_lin) * (u_lin + 1.0)
        y = jnp.dot(act.astype(BF16), wd_bf[...], preferred_element_type=F32) + bd_ref[...]
        _store_interleaved(yb_ref, _pack_bf16_pairs(y))

    @pl.when(i >= nu_ref[0])
    def _():
        yb_ref[...] = jnp.zeros_like(yb_ref)


def _experts(block_e, n_used, xb, w_gate_up, b_gate_up3, w_down, b_down3, *, rb):
    n_rows = xb.shape[0] // ROW_GROUPS
    d = w_gate_up.shape[1]
    f2 = w_gate_up.shape[2]
    f = w_down.shape[1]
    nblk = n_rows // rb
    rowblk = lambda i, be, nu: (jnp.minimum(i, nu[0] - 1), 0)
    wsel = lambda i, be, nu: (be[i], 0, 0)
    grid_spec = pltpu.PrefetchScalarGridSpec(
        num_scalar_prefetch=2,
        grid=(nblk,),
        in_specs=[
            pl.BlockSpec((ROW_GROUPS * rb, LANES), rowblk),
            pl.BlockSpec((None, d, f2), wsel),
            pl.BlockSpec((None, 1, f2), wsel),
            pl.BlockSpec((None, f, d), wsel),
            pl.BlockSpec((None, 1, d), wsel),
        ],
        out_specs=pl.BlockSpec((ROW_GROUPS * rb, LANES), lambda i, be, nu: (i, 0)),
        scratch_shapes=[pltpu.VMEM((d, f2), BF16), pltpu.VMEM((f, d), BF16)],
    )
    return pl.pallas_call(
        _experts_kernel,
        grid_spec=grid_spec,
        out_shape=jax.ShapeDtypeStruct(xb.shape, U32),
        compiler_params=_cparams("arbitrary"),
        name="experts",
    )(block_e, n_used, xb, w_gate_up, b_gate_up3, w_down, b_down3)


def _combine_kernel(dest0_ref, destn_ref, x1_ref, tg_ref, yb_ref, o_ref, buf, sem):
    i = pl.program_id(0)
    n = pl.num_programs(0)
    tm = x1_ref.shape[0]
    slot = i % 2

    def gather_one(dest_ref, s, t):
        for k in range(TOP_K):
            pltpu.make_async_copy(_slab(yb_ref, dest_ref[0, t * TOP_K + k]), _slab(buf.at[s, k], t),
                                  sem.at[s]).start(priority=k % 2)

    @pl.when(i == 0)
    def _():
        def body(t, c):
            gather_one(dest0_ref, 0, t)
            return c
        lax.fori_loop(0, tm, body, 0)

    @pl.when(i + 1 < n)
    def _():
        for t in range(tm):
            gather_one(destn_ref, 1 - slot, t)

    for k in range(TOP_K):
        _slab_copy_wait(yb_ref, buf.at[slot, k], sem.at[slot], tm)
    tg = tg_ref[...]
    half = x1_ref.shape[1] // 2
    for j in range(ROW_GROUPS):
        lo_cols = slice(j * LANES, (j + 1) * LANES)
        hi_cols = slice(half + j * LANES, half + (j + 1) * LANES)
        out_lo = x1_ref[:, lo_cols]
        out_hi = x1_ref[:, hi_cols]
        for k in range(TOP_K):
            y_lo, y_hi = _unpack_bf16_pairs(buf[slot, k, pl.ds(j, tm, stride=ROW_GROUPS), :])
            out_lo = out_lo + tg[:, k:k + 1] * y_lo
            out_hi = out_hi + tg[:, k:k + 1] * y_hi
        o_ref[:, lo_cols] = out_lo
        o_ref[:, hi_cols] = out_hi


def _combine(dest2, x1, tg, yb, *, tm):
    t, d = x1.shape
    n = t // tm
    return pl.pallas_call(
        _combine_kernel,
        grid=(n,),
        in_specs=[
            pl.BlockSpec((None, 1, tm * TOP_K), lambda i: (0, 0, 0), memory_space=pltpu.SMEM),
            pl.BlockSpec((None, 1, tm * TOP_K), lambda i: (jnp.minimum(i + 1, n - 1), 0, 0),
                         memory_space=pltpu.SMEM),
            pl.BlockSpec((tm, d), lambda i: (i, 0)),
            pl.BlockSpec((tm, LANES), lambda i: (i, 0)),
            pl.BlockSpec(memory_space=pl.ANY),
        ],
        out_specs=pl.BlockSpec((tm, d), lambda i: (i, 0)),
        out_shape=jax.ShapeDtypeStruct((t, d), F32),
        scratch_shapes=[pltpu.VMEM((2, TOP_K, ROW_GROUPS * tm, LANES), yb.dtype), pltpu.SemaphoreType.DMA((2,))],
        compiler_params=_cparams("arbitrary"),
        name="combine",
    )(dest2, dest2, x1, tg, yb)


def _block_ones(n, blk):
    i = jnp.arange(n)
    return (i[:, None] // blk == i[None, :] // blk).astype(BF16)


def _pair_suffix(tq):
    j = jnp.arange(2 * tq)
    return ((j[:, None] > j[None, :]) & (j[:, None] // tq == j[None, :] // tq)).astype(BF16)


def _chunk_tril(n, blk):
    i = jnp.arange(n)
    return ((i[:, None] // blk == i[None, :] // blk) & (i[None, :] <= i[:, None])).astype(BF16)


def _strict_tril(n):
    i = jnp.arange(n)
    return (i[None, :] < i[:, None]).astype(BF16)


def _layer(x, norm1_g, w_in, q_g, k_g, sbo_g, lb, hgo_g, w_out, norm2_g, w_router, b_router,
           w_gate_up, b_gate_up, w_down, b_down, *, tm_in, tq, tm_out, tm_row, rb):
    bsz, seq, d = x.shape
    t = bsz * seq
    x2 = x.reshape(t, d)
    nsb = SB_WIDTH // SB_HEAD_DIM

    qkv, hq, hg, hk, hi, sg = _inproj(
        x2, norm1_g.reshape(1, d), w_in.astype(BF16),
        jnp.tile(q_g, nsb).reshape(1, SB_WIDTH), jnp.tile(k_g, nsb).reshape(1, SB_WIDTH),
        lb.reshape(1, HG_WIDTH), _block_ones(256, SB_HEAD_DIM), tm=tm_in)

    a = _sbattn(qkv.reshape(bsz, seq, 3 * SB_WIDTH), sbo_g.reshape(1, SB_WIDTH),
                _pair_suffix(tq), _block_ones(LANES, SB_HEAD_DIM), tq=tq)
    sh = (bsz, seq, HG_WIDTH)
    r = _hgrn(hq.reshape(sh), hg.reshape(sh), hk.reshape(sh), hi.reshape(sh), sg.reshape(sh),
              hgo_g.reshape(1, HG_WIDTH), _chunk_tril(LANES, HG_CHUNK))

    wr = jnp.zeros((d, LANES), F32).at[:, :N_EXPERTS].set(w_router)
    br = jnp.full((1, LANES), -1e30, F32).at[0, :N_EXPERTS].set(b_router)
    x1, hp, ti, tg, rk, cnt = _outproj(
        x2, a.reshape(t, SB_WIDTH), r.reshape(t, HG_WIDTH), w_out.astype(BF16),
        norm2_g.reshape(1, d), wr, br, _strict_tril(tm_out), tm=tm_out)

    counts = cnt[0, :N_EXPERTS].astype(I32)
    padded = (counts + rb - 1) // rb * rb
    pad_ends = jnp.cumsum(padded)
    pad_starts = pad_ends - padded
    n_assign = t * TOP_K
    nblk = n_assign // rb + N_EXPERTS
    eids = jnp.arange(N_EXPERTS, dtype=I32)
    dest = rk[:, :TOP_K] + jnp.sum(jnp.where(ti[:, :TOP_K, None] == eids, pad_starts, 0), axis=-1)
    dest2 = dest.reshape(t // tm_row, 1, tm_row * TOP_K)
    n_used = (pad_ends[-1] // rb).astype(I32).reshape(1)
    blk_start = jnp.minimum(jnp.arange(nblk, dtype=I32), n_used[0] - 1) * rb
    block_e = jnp.sum((blk_start[:, None] >= pad_ends[None, :]).astype(I32), axis=1)

    xb = _scatter(dest2, hp, jnp.zeros((ROW_GROUPS * nblk * rb, LANES), U32), tm=tm_row)
    yb = _experts(block_e, n_used, xb, w_gate_up, b_gate_up.reshape(N_EXPERTS, 1, -1),
                  w_down, b_down.reshape(N_EXPERTS, 1, -1), rb=rb)
    out = _combine(dest2, x1, tg, yb, tm=tm_row)
    return out.reshape(bsz, seq, d)


def kernel(x, norm1_g, w_in, sb_q_norm_g, sb_k_norm_g, sb_out_norm_g, hg_lb_logits, hg_out_norm_g,
           w_out, norm2_g, w_router, b_router, w_gate_up, b_gate_up, w_down, b_down):
    depth = w_in.shape[0]
    lb_all = jnp.cumsum(jax.nn.softmax(hg_lb_logits.astype(F32), axis=0), axis=0)
    for l in range(depth):
        x = _layer(x, norm1_g[l], w_in[l], sb_q_norm_g[l], sb_k_norm_g[l], sb_out_norm_g[l], lb_all[l],
                   hg_out_norm_g[l], w_out[l], norm2_g[l], w_router[l], b_router[l],
                   w_gate_up[l], b_gate_up[l], w_down[l], b_down[l],
                   tm_in=512, tq=256, tm_out=512, tm_row=256, rb=512)
    return x
```

```python
import functools

import jax
import jax.numpy as jnp
from jax import lax
from jax.experimental import pallas as pl
from jax.experimental.pallas import tpu as pltpu

F32 = jnp.float32
BF16 = jnp.bfloat16
I32 = jnp.int32
U32 = jnp.uint32

EPS = 1e-6
LANES = 128
SB_HEAD_DIM = 64
SB_WIDTH = 512
HG_HEAD_DIM = 128
HG_WIDTH = 512
HG_CHUNK = 32
N_EXPERTS = 32
TOP_K = 4
SWIGLU_LIMIT = 7.0
SWIGLU_ALPHA = 1.702
VMEM_LIMIT = 56 * 1024 * 1024

NT_DIMS = (((1,), (1,)), ((), ()))
NPAIR = SB_WIDTH // LANES
LOG2E = 1.4426950408889634


def _cparams(*sem):
    return pltpu.CompilerParams(dimension_semantics=sem, vmem_limit_bytes=VMEM_LIMIT)


def _bf16_split(a, n):
    parts = []
    for _ in range(n - 1):
        p = a.astype(BF16)
        parts.append(p)
        a = a - p.astype(F32)
    parts.append(a.astype(BF16))
    return parts


def _sigmoid(a):
    return 1.0 / (1.0 + jnp.exp(-a))


def _pack_bf16_pairs(a):
    w = a.shape[1] // 2
    lo_bits = pltpu.bitcast(a[:, :w].astype(BF16).astype(F32), U32)
    hi_bits = pltpu.bitcast(a[:, w:].astype(BF16).astype(F32), U32)
    return (hi_bits & jnp.uint32(0xFFFF0000)) | (lo_bits >> 16)


def _unpack_bf16_pairs(p):
    return pltpu.bitcast(p << 16, F32), pltpu.bitcast(p & jnp.uint32(0xFFFF0000), F32)


ROW_GROUPS = 4


def _store_interleaved(ref, a):
    rows = a.shape[0]
    for j in range(ROW_GROUPS):
        ref[pl.ds(j, rows, stride=ROW_GROUPS), :] = a[:, j * LANES:(j + 1) * LANES]


def _load_interleaved(ref):
    rows = ref.shape[0] // ROW_GROUPS
    return jnp.concatenate([ref[pl.ds(j, rows, stride=ROW_GROUPS), :] for j in range(ROW_GROUPS)], axis=1)


def _inproj_kernel(x_ref, g1_ref, w_ref, gq_ref, gk_ref, lb_ref, bd_ref,
                   qkv_ref, hq_ref, hg_ref, hk_ref, hi_ref, sg_ref):
    x = x_ref[...]
    ms = jnp.mean(x * x, axis=-1, keepdims=True)
    h = (x * lax.rsqrt(ms + EPS) * g1_ref[...]).astype(BF16)
    bd = bd_ref[...]

    def seg(j):
        return jnp.dot(h, w_ref[:, j * 512:(j + 1) * 512], preferred_element_type=F32)

    def head_norm(a, g, scale):
        outs = []
        for c in range(2):
            ac = a[:, c * 256:(c + 1) * 256]
            ss = jnp.dot((ac * ac).astype(BF16), bd, preferred_element_type=F32)
            outs.append(ac * lax.rsqrt(ss * (1.0 / SB_HEAD_DIM) + EPS))
        y = jnp.concatenate(outs, axis=1) * g
        return y * scale if scale != 1.0 else y

    qkv_ref[:, 0:512] = head_norm(seg(0), gq_ref[...], LOG2E * SB_HEAD_DIM ** -0.5).astype(BF16)
    qkv_ref[:, 512:1024] = head_norm(seg(1), gk_ref[...], 1.0).astype(BF16)
    qkv_ref[:, 1024:1536] = seg(2).astype(BF16)
    hq_ref[...] = seg(3).astype(BF16)
    f = seg(4)
    lb = lb_ref[...]
    hg_ref[...] = jnp.log(lb + (1.0 - lb) * _sigmoid(f))
    hk_ref[...] = (1.0 - lb) * _sigmoid(-f)
    hi_ref[...] = seg(5).astype(BF16)
    gate = seg(6)
    sg_ref[...] = (gate * _sigmoid(gate)).astype(BF16)


def _inproj(x2, g1, w_bf, gq, gk, lb, bd, *, tm):
    t, d = x2.shape
    n = w_bf.shape[1]
    row = lambda i: (i, 0)
    fix = lambda i: (0, 0)
    return pl.pallas_call(
        _inproj_kernel,
        grid=(t // tm,),
        in_specs=[
            pl.BlockSpec((tm, d), row),
            pl.BlockSpec((1, d), fix),
            pl.BlockSpec((d, n), fix),
            pl.BlockSpec((1, 512), fix),
            pl.BlockSpec((1, 512), fix),
            pl.BlockSpec((1, 512), fix),
            pl.BlockSpec((256, 256), fix),
        ],
        out_specs=[
            pl.BlockSpec((tm, 1536), row),
            pl.BlockSpec((tm, 512), row),
            pl.BlockSpec((tm, 512), row),
            pl.BlockSpec((tm, 512), row),
            pl.BlockSpec((tm, 512), row),
            pl.BlockSpec((tm, 512), row),
        ],
        out_shape=[
            jax.ShapeDtypeStruct((t, 1536), BF16),
            jax.ShapeDtypeStruct((t, 512), BF16),
            jax.ShapeDtypeStruct((t, 512), F32),
            jax.ShapeDtypeStruct((t, 512), F32),
            jax.ShapeDtypeStruct((t, 512), BF16),
            jax.ShapeDtypeStruct((t, 512), BF16),
        ],
        compiler_params=_cparams("parallel"),
        name="inproj",
    )(x2, g1, w_bf, gq, gk, lb, bd)


def _sbattn_kernel(q_ref, k_ref, v_ref, go_ref, u2_ref, bd_ref, o_ref, vm_ref, *scr, tq):
    acc_ref = scr[0:NPAIR]
    carry_ref = scr[NPAIR:2 * NPAIR]
    z_ref = scr[2 * NPAIR:3 * NPAIR]
    w_ref = scr[3 * NPAIR:4 * NPAIR]
    qi = pl.program_id(1)
    hps = range(NPAIR)
    lane = lax.broadcasted_iota(I32, (tq, LANES), 1)
    lo = lane < SB_HEAD_DIM

    @pl.when(qi == 0)
    def _():
        v = v_ref[...]
        lo_all = (lax.broadcasted_iota(I32, v.shape, 1) & (LANES - 1)) < SB_HEAD_DIM
        zv = jnp.zeros_like(v)
        vm_ref[0] = jnp.where(lo_all, v, zv)
        vm_ref[1] = jnp.where(lo_all, zv, v)

    u2 = u2_ref[...]
    for hp in hps:
        acc_ref[hp][...] = jnp.zeros_like(acc_ref[hp])
        carry_ref[hp][...] = jnp.zeros_like(carry_ref[hp])
    row = lax.broadcasted_iota(I32, (2 * tq, tq), 0)
    col = lax.broadcasted_iota(I32, (2 * tq, tq), 1)
    strict = col < jnp.where(row >= tq, row - tq, row)
    q2s = []
    for hp in hps:
        q = q_ref[:, hp * LANES:(hp + 1) * LANES]
        zq = jnp.zeros_like(q)
        q2s.append(jnp.concatenate([jnp.where(lo, q, zq), jnp.where(lo, zq, q)], axis=0))

    def scores(kb):
        off = pl.multiple_of(kb * tq, tq)
        return [lax.dot_general(q2s[hp], k_ref[pl.ds(off, tq), hp * LANES:(hp + 1) * LANES], NT_DIMS,
                                preferred_element_type=F32) for hp in hps]

    def weighted_values(kb):
        off = pl.multiple_of(kb * tq, tq)
        for hp in hps:
            vcat = jnp.concatenate([vm_ref[0, pl.ds(off, tq), hp * LANES:(hp + 1) * LANES],
                                    vm_ref[1, pl.ds(off, tq), hp * LANES:(hp + 1) * LANES]], axis=0)
            acc_ref[hp][...] += jnp.dot(w_ref[hp][...], vcat, preferred_element_type=F32)

    def weights(zs, masked):
        lgs = []
        for z in zs:
            nz = -z
            lg = jnp.minimum(nz, 0.0) - jnp.log2(1.0 + jnp.exp2(jnp.minimum(z, nz)))
            if masked:
                lg = jnp.where(strict, lg, 0.0)
            lgs.append(lg)
        withins = []
        for pr in range(NPAIR // 2):
            cat = jnp.concatenate([lgs[2 * pr], lgs[2 * pr + 1]], axis=1).astype(BF16)
            res = jnp.dot(cat, u2, preferred_element_type=F32)
            withins += [res[:, :tq], res[:, tq:]]
        reps = tq // LANES
        carries = [jnp.concatenate([carry_ref[hp][...]] * reps, axis=1) for hp in hps]
        ws = [jnp.exp2((zs[hp] + lgs[hp]) + (carries[hp] + withins[hp])) for hp in hps]
        if masked:
            ws = [jnp.where(strict, w, 0.0) for w in ws]
        for hp in hps:
            carry_ref[hp][...] += jnp.sum(lgs[hp], axis=-1, keepdims=True)
        return [jnp.concatenate([w[:tq], w[tq:]], axis=1).astype(BF16) for w in ws]

    ws = weights(scores(qi), True)
    zn = scores(jnp.maximum(qi - 1, 0))
    for hp in hps:
        w_ref[hp][...] = ws[hp]
        z_ref[hp][...] = zn[hp]

    def body(it, c):
        kb = qi - 1 - it
        zs = [z_ref[hp][...] for hp in hps]
        zn = scores(jnp.maximum(kb - 1, 0))
        weighted_values(kb + 1)
        ws = weights(zs, False)
        for hp in hps:
            w_ref[hp][...] = ws[hp]
            z_ref[hp][...] = zn[hp]
        return c

    lax.fori_loop(0, qi, body, 0)
    weighted_values(0)
    for hp in hps:
        a = acc_ref[hp][...]
        ss = jnp.dot((a * a).astype(BF16), bd_ref[...], preferred_element_type=F32)
        o_ref[:, hp * LANES:(hp + 1) * LANES] = (
            a * lax.rsqrt(ss * (1.0 / SB_HEAD_DIM) + EPS) * go_ref[:, hp * LANES:(hp + 1) * LANES]).astype(BF16)


def _sbattn(qkv, go, u2, bd, *, tq):
    b, s, _ = qkv.shape
    return pl.pallas_call(
        functools.partial(_sbattn_kernel, tq=tq),
        grid=(b, s // tq),
        in_specs=[
            pl.BlockSpec((None, tq, SB_WIDTH), lambda bi, qi: (bi, qi, 0)),
            pl.BlockSpec((None, s, SB_WIDTH), lambda bi, qi: (bi, 0, 1)),
            pl.BlockSpec((None, s, SB_WIDTH), lambda bi, qi: (bi, 0, 2)),
            pl.BlockSpec((1, SB_WIDTH), lambda bi, qi: (0, 0)),
            pl.BlockSpec((2 * tq, 2 * tq), lambda bi, qi: (0, 0)),
            pl.BlockSpec((LANES, LANES), lambda bi, qi: (0, 0)),
        ],
        out_specs=pl.BlockSpec((None, tq, SB_WIDTH), lambda bi, qi: (bi, qi, 0)),
        out_shape=jax.ShapeDtypeStruct((b, s, SB_WIDTH), BF16),
        scratch_shapes=([pltpu.VMEM((2, s, SB_WIDTH), BF16)]
                        + [pltpu.VMEM((tq, LANES), F32)] * NPAIR
                        + [pltpu.VMEM((2 * tq, LANES), F32)] * NPAIR
                        + [pltpu.VMEM((2 * tq, tq), F32)] * NPAIR
                        + [pltpu.VMEM((tq, 2 * tq), BF16)] * NPAIR),
        compiler_params=_cparams("parallel", "arbitrary"),
        name="sbattn",
    )(qkv, qkv, qkv, go, u2, bd)


def _hgrn_kernel(q_ref, g_ref, k_ref, v_ref, sg_ref, gn_ref, tl_ref, o_ref, *state_ref):
    s = q_ref.shape[0]
    grp = LANES
    nch = grp // HG_CHUNK
    hs = range(HG_WIDTH // HG_HEAD_DIM)
    for h in hs:
        state_ref[h][...] = jnp.zeros_like(state_ref[h])
    tl = tl_ref[...]
    rr = lax.broadcasted_iota(I32, (grp, grp), 0)
    cc = lax.broadcasted_iota(I32, (grp, grp), 1)
    shift = HG_CHUNK.bit_length() - 1
    rchunk = rr >> shift
    causal = (rchunk == (cc >> shift)) & (cc <= rr)
    inchunk = [rchunk == ci for ci in range(nch)]

    def group(r, c):
        off = pl.multiple_of(r * grp, grp)
        col = lambda h: slice(h * HG_HEAD_DIM, (h + 1) * HG_HEAD_DIM)
        rows = pl.ds(off, grp)
        vs = [v_ref[rows, col(h)] for h in hs]
        bs = [sum(jnp.dot(tl, p, preferred_element_type=F32) for p in _bf16_split(g_ref[rows, col(h)], 2))
              for h in hs]
        lasts = [[b[(ci + 1) * HG_CHUNK - 1:(ci + 1) * HG_CHUNK, :] for ci in range(nch)] for b in bs]
        bls = [jnp.concatenate([jnp.broadcast_to(l, (HG_CHUNK, HG_HEAD_DIM)) for l in last], axis=0)
               for last in lasts]
        qes = [(q_ref[rows, col(h)].astype(F32) * jnp.exp(bs[h])).astype(BF16) for h in hs]
        kes = [(k_ref[rows, col(h)] * jnp.exp(-bs[h])).astype(BF16) for h in hs]
        kds = [(k_ref[rows, col(h)] * jnp.exp(bls[h] - bs[h])).astype(BF16) for h in hs]
        attns = [jnp.where(causal, lax.dot_general(qes[h], kes[h], NT_DIMS, preferred_element_type=F32), 0.0)
                 .astype(BF16) for h in hs]
        os = [jnp.dot(attns[h], vs[h], preferred_element_type=F32) for h in hs]
        vts = [vs[h].astype(F32).T.astype(BF16) for h in hs]
        uts = []
        for h in hs:
            zk = jnp.zeros_like(kds[h])
            kdx = jnp.concatenate([jnp.where(inchunk[ci], kds[h], zk) for ci in range(nch)], axis=1)
            uts.append(jnp.dot(vts[h], kdx, preferred_element_type=F32))
        for h in hs:
            st = state_ref[h][...]
            parts = []
            for ci in range(nch):
                parts.append(st.astype(BF16))
                st = st * jnp.exp(lasts[h][ci]) + uts[h][:, ci * HG_HEAD_DIM:(ci + 1) * HG_HEAD_DIM]
            state_ref[h][...] = st
            zq = jnp.zeros_like(qes[h])
            qx = jnp.concatenate([jnp.where(inchunk[ci], qes[h], zq) for ci in range(nch)], axis=1)
            o = os[h] + lax.dot_general(qx, jnp.concatenate(parts, axis=1), NT_DIMS, preferred_element_type=F32)
            ms = jnp.mean(o * o, axis=-1, keepdims=True)
            y = o * lax.rsqrt(ms + EPS) * gn_ref[:, col(h)]
            o_ref[rows, col(h)] = (y * sg_ref[rows, col(h)].astype(F32)).astype(BF16)
        return c

    lax.fori_loop(0, s // grp, group, 0, unroll=4)


def _hgrn(hq, hg, hk, hi, sg, gn, tl):
    b, s, w = hq.shape
    nh = HG_WIDTH // HG_HEAD_DIM
    blk = pl.BlockSpec((None, s, w), lambda bi: (bi, 0, 0))
    return pl.pallas_call(
        _hgrn_kernel,
        grid=(b,),
        in_specs=[blk, blk, blk, blk, blk,
                  pl.BlockSpec((1, w), lambda bi: (0, 0)),
                  pl.BlockSpec((LANES, LANES), lambda bi: (0, 0))],
        out_specs=blk,
        out_shape=jax.ShapeDtypeStruct((b, s, w), BF16),
        scratch_shapes=[pltpu.VMEM((HG_HEAD_DIM, HG_HEAD_DIM), F32)] * nh,
        compiler_params=_cparams("parallel"),
        name="hgrn",
    )(hq, hg, hk, hi, sg, gn, tl)


def _outproj_kernel(x_ref, a_ref, r_ref, wo_ref, g2_ref, wr_ref, br_ref, tri_ref,
                    x1_ref, hp_ref, ti_ref, tg_ref, rk_ref, cnt_ref, run_ref):
    i = pl.program_id(0)

    @pl.when(i == 0)
    def _():
        run_ref[...] = jnp.zeros_like(run_ref)

    half = a_ref.shape[1]
    x1 = (x_ref[...]
          + jnp.dot(a_ref[...], wo_ref[:half, :], preferred_element_type=F32)
          + jnp.dot(r_ref[...], wo_ref[half:, :], preferred_element_type=F32))
    x1_ref[...] = x1
    ms = jnp.mean(x1 * x1, axis=-1, keepdims=True)
    h2 = x1 * lax.rsqrt(ms + EPS) * g2_ref[...]
    _store_interleaved(hp_ref, _pack_bf16_pairs(h2))

    h_hi, h_lo = _bf16_split(h2, 2)
    w_hi, w_lo = _bf16_split(wr_ref[...], 2)
    hi_both = jnp.dot(h_hi, jnp.concatenate([w_hi, w_lo], axis=1), preferred_element_type=F32)
    logits = ((hi_both[:, :LANES] + jnp.dot(h_lo, w_hi, preferred_element_type=F32))
              + hi_both[:, LANES:]) + br_ref[...]
    tm = logits.shape[0]
    lane = lax.broadcasted_iota(I32, (tm, LANES), 1).astype(F32)
    neg = jnp.float32(-jnp.inf)
    ti = jnp.zeros((tm, LANES), F32)
    tv = jnp.zeros((tm, LANES), F32)
    onehot = jnp.zeros((tm, LANES), F32)
    sels = []
    work = logits
    for k in range(TOP_K):
        m = jnp.max(work, axis=-1, keepdims=True)
        idx = jnp.min(jnp.where(work == m, lane, float(LANES)), axis=-1, keepdims=True)
        sel = lane == idx
        sels.append(sel)
        ti = jnp.where(lane == k, idx, ti)
        tv = jnp.where(lane == k, m, tv)
        onehot = jnp.where(sel, 1.0, onehot)
        work = jnp.where(sel, neg, work)
    valid = lane < TOP_K
    e = jnp.where(valid, jnp.exp(tv - jnp.max(jnp.where(valid, tv, neg), axis=-1, keepdims=True)), 0.0)
    tg_ref[...] = e / jnp.sum(e, axis=-1, keepdims=True)
    ti_ref[...] = ti.astype(I32)

    before = jnp.dot(tri_ref[...], onehot.astype(BF16), preferred_element_type=F32) + run_ref[...]
    rk = jnp.zeros((tm, LANES), F32)
    for k in range(TOP_K):
        rk = jnp.where(lane == k, jnp.sum(jnp.where(sels[k], before, 0.0), axis=-1, keepdims=True), rk)
    rk_ref[...] = rk.astype(I32)
    run_ref[...] += jnp.sum(onehot, axis=0, keepdims=True)
    cnt_ref[...] = run_ref[...]


def _outproj(x2, a2, r2, wo_bf, g2, wr, br, tri, *, tm):
    t, d = x2.shape
    half = a2.shape[1]
    row = lambda i: (i, 0)
    fix = lambda i: (0, 0)
    return pl.pallas_call(
        _outproj_kernel,
        grid=(t // tm,),
        in_specs=[
            pl.BlockSpec((tm, d), row),
            pl.BlockSpec((tm, half), row),
            pl.BlockSpec((tm, half), row),
            pl.BlockSpec((2 * half, d), fix),
            pl.BlockSpec((1, d), fix),
            pl.BlockSpec((d, LANES), fix),
            pl.BlockSpec((1, LANES), fix),
            pl.BlockSpec((tm, tm), fix),
        ],
        out_specs=[
            pl.BlockSpec((tm, d), row),
            pl.BlockSpec((ROW_GROUPS * tm, LANES), row),
            pl.BlockSpec((tm, LANES), row),
            pl.BlockSpec((tm, LANES), row),
            pl.BlockSpec((tm, LANES), row),
            pl.BlockSpec((1, LANES), fix),
        ],
        out_shape=[
            jax.ShapeDtypeStruct((t, d), F32),
            jax.ShapeDtypeStruct((ROW_GROUPS * t, LANES), U32),
            jax.ShapeDtypeStruct((t, LANES), I32),
            jax.ShapeDtypeStruct((t, LANES), F32),
            jax.ShapeDtypeStruct((t, LANES), I32),
            jax.ShapeDtypeStruct((1, LANES), F32),
        ],
        scratch_shapes=[pltpu.VMEM((1, LANES), F32)],
        compiler_params=_cparams("arbitrary"),
        name="outproj",
    )(x2, a2, r2, wo_bf, g2, wr, br, tri)


def _slab(ref, r):
    return ref.at[pl.ds(pl.multiple_of(ROW_GROUPS * r, ROW_GROUPS), ROW_GROUPS)]


def _slab_copy_wait(src, dst, sem, n):
    pltpu.make_async_copy(src.at[pl.ds(0, ROW_GROUPS * n)], dst.at[pl.ds(0, ROW_GROUPS * n)], sem).wait()


def _scatter_kernel(dest_ref, hp_ref, xz_ref, xb_ref, sem):
    del xz_ref
    tm = hp_ref.shape[0] // ROW_GROUPS
    for t in range(tm):
        for k in range(TOP_K):
            pltpu.make_async_copy(_slab(hp_ref, t), _slab(xb_ref, dest_ref[0, t * TOP_K + k]),
                                  sem).start(priority=k % 2)
    for _ in range(TOP_K):
        _slab_copy_wait(hp_ref, xb_ref, sem, tm)


def _scatter(dest2, hp, xzero, *, tm):
    t = hp.shape[0] // ROW_GROUPS
    return pl.pallas_call(
        _scatter_kernel,
        grid=(t // tm,),
        in_specs=[
            pl.BlockSpec((None, 1, tm * TOP_K), lambda i: (i, 0, 0), memory_space=pltpu.SMEM),
            pl.BlockSpec((ROW_GROUPS * tm, LANES), lambda i: (i, 0)),
            pl.BlockSpec(memory_space=pl.ANY),
        ],
        out_specs=pl.BlockSpec(memory_space=pl.ANY),
        out_shape=jax.ShapeDtypeStruct(xzero.shape, xzero.dtype),
        scratch_shapes=[pltpu.SemaphoreType.DMA],
        input_output_aliases={2: 0},
        compiler_params=_cparams("arbitrary"),
        name="scatter",
    )(dest2, hp, xzero)


def _experts_kernel(be_ref, nu_ref, xb_ref, wgu_ref, bgu_ref, wd_ref, bd_ref, yb_ref, wgu_bf, wd_bf):
    i = pl.program_id(0)
    fresh = jnp.logical_or(i == 0, be_ref[i] != be_ref[jnp.maximum(i - 1, 0)])

    @pl.when(fresh)
    def _():
        wgu_bf[...] = wgu_ref[...].astype(BF16)
        wd_bf[...] = wd_ref[...].astype(BF16)

    @pl.when(i < nu_ref[0])
    def _():
        x_lo, x_hi = _unpack_bf16_pairs(_load_interleaved(xb_ref))
        x = jnp.concatenate([x_lo.astype(BF16), x_hi.astype(BF16)], axis=1)
        gu = jnp.dot(x, wgu_bf[...], preferred_element_type=F32) + bgu_ref[...]
        f = gu.shape[1] // 2
        g_lin = jnp.minimum(gu[:, :f], SWIGLU_LIMIT)
        u_lin = jnp.clip(gu[:, f:], -SWIGLU_LIMIT, SWIGLU_LIMIT)
        act = g_lin * _sigmoid(SWIGLU_ALPHA * g_lin) * (u_lin + 1.0)
        y = jnp.dot(act.astype(BF16), wd_bf[...], preferred_element_type=F32) + bd_ref[...]
        _store_interleaved(yb_ref, _pack_bf16_pairs(y))

    @pl.when(i >= nu_ref[0])
    def _():
        yb_ref[...] = jnp.zeros_like(yb_ref)


def _experts(block_e, n_used, xb, w_gate_up, b_gate_up3, w_down, b_down3, *, rb):
    n_rows = xb.shape[0] // ROW_GROUPS
    d = w_gate_up.shape[1]
    f2 = w_gate_up.shape[2]
    f = w_down.shape[1]
    nblk = n_rows // rb
    rowblk = lambda i, be, nu: (jnp.minimum(i, nu[0] - 1), 0)
    wsel = lambda i, be, nu: (be[i], 0, 0)
    grid_spec = pltpu.PrefetchScalarGridSpec(
        num_scalar_prefetch=2,
        grid=(nblk,),
        in_specs=[
            pl.BlockSpec((ROW_GROUPS * rb, LANES), rowblk),
            pl.BlockSpec((None, d, f2), wsel),
            pl.BlockSpec((None, 1, f2), wsel),
            pl.BlockSpec((None, f, d), wsel),
            pl.BlockSpec((None, 1, d), wsel),
        ],
        out_specs=pl.BlockSpec((ROW_GROUPS * rb, LANES), lambda i, be, nu: (i, 0)),
        scratch_shapes=[pltpu.VMEM((d, f2), BF16), pltpu.VMEM((f, d), BF16)],
    )
    return pl.pallas_call(
        _experts_kernel,
        grid_spec=grid_spec,
        out_shape=jax.ShapeDtypeStruct(xb.shape, U32),
        compiler_params=_cparams("arbitrary"),
        name="experts",
    )(block_e, n_used, xb, w_gate_up, b_gate_up3, w_down, b_down3)


def _combine_kernel(dest0_ref, destn_ref, x1_ref, tg_ref, yb_ref, o_ref, buf, sem):
    i = pl.program_id(0)
    n = pl.num_programs(0)
    tm = x1_ref.shape[0]
    slot = i % 2

    def gather_one(dest_ref, s, t):
        for k in range(TOP_K):
            pltpu.make_async_copy(_slab(yb_ref, dest_ref[0, t * TOP_K + k]), _slab(buf.at[s, k], t),
                                  sem.at[s]).start(priority=k % 2)

    @pl.when(i == 0)
    def _():
        def body(t, c):
            gather_one(dest0_ref, 0, t)
            return c
        lax.fori_loop(0, tm, body, 0)

    @pl.when(i + 1 < n)
    def _():
        for t in range(tm):
            gather_one(destn_ref, 1 - slot, t)

    for k in range(TOP_K):
        _slab_copy_wait(yb_ref, buf.at[slot, k], sem.at[slot], tm)
    tg = tg_ref[...]
    half = x1_ref.shape[1] // 2
    for j in range(ROW_GROUPS):
        lo_cols = slice(j * LANES, (j + 1) * LANES)
        hi_cols = slice(half + j * LANES, half + (j + 1) * LANES)
        out_lo = x1_ref[:, lo_cols]
        out_hi = x1_ref[:, hi_cols]
        for k in range(TOP_K):
            y_lo, y_hi = _unpack_bf16_pairs(buf[slot, k, pl.ds(j, tm, stride=ROW_GROUPS), :])
            out_lo = out_lo + tg[:, k:k + 1] * y_lo
            out_hi = out_hi + tg[:, k:k + 1] * y_hi
        o_ref[:, lo_cols] = out_lo
        o_ref[:, hi_cols] = out_hi


def _combine(dest2, x1, tg, yb, *, tm):
    t, d = x1.shape
    n = t // tm
    return pl.pallas_call(
        _combine_kernel,
        grid=(n,),
        in_specs=[
            pl.BlockSpec((None, 1, tm * TOP_K), lambda i: (0, 0, 0), memory_space=pltpu.SMEM),
            pl.BlockSpec((None, 1, tm * TOP_K), lambda i: (jnp.minimum(i + 1, n - 1), 0, 0),
                         memory_space=pltpu.SMEM),
            pl.BlockSpec((tm, d), lambda i: (i, 0)),
            pl.BlockSpec((tm, LANES), lambda i: (i, 0)),
            pl.BlockSpec(memory_space=pl.ANY),
        ],
        out_specs=pl.BlockSpec((tm, d), lambda i: (i, 0)),
        out_shape=jax.ShapeDtypeStruct((t, d), F32),
        scratch_shapes=[pltpu.VMEM((2, TOP_K, ROW_GROUPS * tm, LANES), yb.dtype), pltpu.SemaphoreType.DMA((2,))],
        compiler_params=_cparams("arbitrary"),
        name="combine",
    )(dest2, dest2, x1, tg, yb)


def _block_ones(n, blk):
    i = jnp.arange(n)
    return (i[:, None] // blk == i[None, :] // blk).astype(BF16)


def _pair_suffix(tq):
    j = jnp.arange(2 * tq)
    return ((j[:, None] > j[None, :]) & (j[:, None] // tq == j[None, :] // tq)).astype(BF16)


def _chunk_tril(n, blk):
    i = jnp.arange(n)
    return ((i[:, None] // blk == i[None, :] // blk) & (i[None, :] <= i[:, None])).astype(BF16)


def _strict_tril(n):
    i = jnp.arange(n)
    return (i[None, :] < i[:, None]).astype(BF16)


def _layer(x, norm1_g, w_in, q_g, k_g, sbo_g, lb, hgo_g, w_out, norm2_g, w_router, b_router,
           w_gate_up, b_gate_up, w_down, b_down, *, tm_in, tq, tm_out, tm_row, rb):
    bsz, seq, d = x.shape
    t = bsz * seq
    x2 = x.reshape(t, d)
    nsb = SB_WIDTH // SB_HEAD_DIM

    qkv, hq, hg, hk, hi, sg = _inproj(
        x2, norm1_g.reshape(1, d), w_in.astype(BF16),
        jnp.tile(q_g, nsb).reshape(1, SB_WIDTH), jnp.tile(k_g, nsb).reshape(1, SB_WIDTH),
        lb.reshape(1, HG_WIDTH), _block_ones(256, SB_HEAD_DIM), tm=tm_in)

    a = _sbattn(qkv.reshape(bsz, seq, 3 * SB_WIDTH), sbo_g.reshape(1, SB_WIDTH),
                _pair_suffix(tq), _block_ones(LANES, SB_HEAD_DIM), tq=tq)
    sh = (bsz, seq, HG_WIDTH)
    r = _hgrn(hq.reshape(sh), hg.reshape(sh), hk.reshape(sh), hi.reshape(sh), sg.reshape(sh),
              hgo_g.reshape(1, HG_WIDTH), _chunk_tril(LANES, HG_CHUNK))

    wr = jnp.zeros((d, LANES), F32).at[:, :N_EXPERTS].set(w_router)
    br = jnp.full((1, LANES), -1e30, F32).at[0, :N_EXPERTS].set(b_router)
    x1, hp, ti, tg, rk, cnt = _outproj(
        x2, a.reshape(t, SB_WIDTH), r.reshape(t, HG_WIDTH), w_out.astype(BF16),
        norm2_g.reshape(1, d), wr, br, _strict_tril(tm_out), tm=tm_out)

    counts = cnt[0, :N_EXPERTS].astype(I32)
    padded = (counts + rb - 1) // rb * rb
    pad_ends = jnp.cumsum(padded)
    pad_starts = pad_ends - padded
    n_assign = t * TOP_K
    nblk = n_assign // rb + N_EXPERTS
    eids = jnp.arange(N_EXPERTS, dtype=I32)
    dest = rk[:, :TOP_K] + jnp.sum(jnp.where(ti[:, :TOP_K, None] == eids, pad_starts, 0), axis=-1)
    dest2 = dest.reshape(t // tm_row, 1, tm_row * TOP_K)
    n_used = (pad_ends[-1] // rb).astype(I32).reshape(1)
    blk_start = jnp.minimum(jnp.arange(nblk, dtype=I32), n_used[0] - 1) * rb
    block_e = jnp.sum((blk_start[:, None] >= pad_ends[None, :]).astype(I32), axis=1)

    xb = _scatter(dest2, hp, jnp.zeros((ROW_GROUPS * nblk * rb, LANES), U32), tm=tm_row)
    yb = _experts(block_e, n_used, xb, w_gate_up, b_gate_up.reshape(N_EXPERTS, 1, -1),
                  w_down, b_down.reshape(N_EXPERTS, 1, -1), rb=rb)
    out = _combine(dest2, x1, tg, yb, tm=tm_row)
    return out.reshape(bsz, seq, d)


def kernel(x, norm1_g, w_in, sb_q_norm_g, sb_k_norm_g, sb_out_norm_g, hg_lb_logits, hg_out_norm_g,
           w_out, norm2_g, w_router, b_router, w_gate_up, b_gate_up, w_down, b_down):
    depth = w_in.shape[0]
    lb_all = jnp.cumsum(jax.nn.softmax(hg_lb_logits.astype(F32), axis=0), axis=0)
    for l in range(depth):
        x = _layer(x, norm1_g[l], w_in[l], sb_q_norm_g[l], sb_k_norm_g[l], sb_out_norm_g[l], lb_all[l],
                   hg_out_norm_g[l], w_out[l], norm2_g[l], w_router[l], b_router[l],
                   w_gate_up[l], b_gate_up[l], w_down[l], b_down[l],
                   tm_in=512, tq=256, tm_out=512, tm_row=512, rb=512)
    return x
```

```python
import functools

import jax
import jax.numpy as jnp
from jax import lax
from jax.experimental import pallas as pl
from jax.experimental.pallas import tpu as pltpu

F32 = jnp.float32
BF16 = jnp.bfloat16
I32 = jnp.int32
U32 = jnp.uint32

EPS = 1e-6
LANES = 128
SB_HEAD_DIM = 64
SB_WIDTH = 512
HG_HEAD_DIM = 128
HG_WIDTH = 512
HG_CHUNK = 32
N_EXPERTS = 32
TOP_K = 4
SWIGLU_LIMIT = 7.0
SWIGLU_ALPHA = 1.702
VMEM_LIMIT = 56 * 1024 * 1024

TM_INPROJ = 512
TQ_ATTN = 256
TM_OUTPROJ = 512
TM_SCATTER = 512
TM_COMBINE = 256
RB_EXPERTS = 512

NT_DIMS = (((1,), (1,)), ((), ()))
NPAIR = SB_WIDTH // LANES
LOG2E = 1.4426950408889634


def _cparams(*sem):
    return pltpu.CompilerParams(dimension_semantics=sem, vmem_limit_bytes=VMEM_LIMIT)


def _bf16_split(a, n):
    parts = []
    for _ in range(n - 1):
        p = a.astype(BF16)
        parts.append(p)
        a = a - p.astype(F32)
    parts.append(a.astype(BF16))
    return parts


def _sigmoid(a):
    return 1.0 / (1.0 + jnp.exp(-a))


def _pack_bf16_pairs(a):
    w = a.shape[1] // 2
    lo_bits = pltpu.bitcast(a[:, :w].astype(BF16).astype(F32), U32)
    hi_bits = pltpu.bitcast(a[:, w:].astype(BF16).astype(F32), U32)
    return (hi_bits & jnp.uint32(0xFFFF0000)) | (lo_bits >> 16)


def _unpack_bf16_pairs(p):
    return pltpu.bitcast(p << 16, F32), pltpu.bitcast(p & jnp.uint32(0xFFFF0000), F32)


ROW_GROUPS = 4


def _store_interleaved(ref, a):
    rows = a.shape[0]
    for j in range(ROW_GROUPS):
        ref[pl.ds(j, rows, stride=ROW_GROUPS), :] = a[:, j * LANES:(j + 1) * LANES]


def _load_interleaved(ref):
    rows = ref.shape[0] // ROW_GROUPS
    return jnp.concatenate([ref[pl.ds(j, rows, stride=ROW_GROUPS), :] for j in range(ROW_GROUPS)], axis=1)


def _inproj_kernel(x_ref, g1_ref, w_ref, gq_ref, gk_ref, lb_ref, bd_ref,
                   qkv_ref, hq_ref, hg_ref, hk_ref, hi_ref, sg_ref):
    x = x_ref[...]
    ms = jnp.mean(x * x, axis=-1, keepdims=True)
    h = (x * lax.rsqrt(ms + EPS) * g1_ref[...]).astype(BF16)
    bd = bd_ref[...]

    def seg(j):
        return jnp.dot(h, w_ref[:, j * 512:(j + 1) * 512], preferred_element_type=F32)

    def head_norm(a, g, scale):
        outs = []
        for c in range(2):
            ac = a[:, c * 256:(c + 1) * 256]
            ss = jnp.dot((ac * ac).astype(BF16), bd, preferred_element_type=F32)
            outs.append(ac * lax.rsqrt(ss * (1.0 / SB_HEAD_DIM) + EPS))
        y = jnp.concatenate(outs, axis=1) * g
        return y * scale if scale != 1.0 else y

    qkv_ref[:, 0:512] = head_norm(seg(0), gq_ref[...], LOG2E * SB_HEAD_DIM ** -0.5).astype(BF16)
    qkv_ref[:, 512:1024] = head_norm(seg(1), gk_ref[...], 1.0).astype(BF16)
    qkv_ref[:, 1024:1536] = seg(2).astype(BF16)
    hq_ref[...] = seg(3).astype(BF16)
    f = seg(4)
    lb = lb_ref[...]
    hg_ref[...] = jnp.log(lb + (1.0 - lb) * _sigmoid(f))
    hk_ref[...] = (1.0 - lb) * _sigmoid(-f)
    hi_ref[...] = seg(5).astype(BF16)
    gate = seg(6)
    sg_ref[...] = (gate * _sigmoid(gate)).astype(BF16)


def _inproj(x2, g1, w_bf, gq, gk, lb, bd, *, tm):
    t, d = x2.shape
    n = w_bf.shape[1]
    row = lambda i: (i, 0)
    fix = lambda i: (0, 0)
    return pl.pallas_call(
        _inproj_kernel,
        grid=(t // tm,),
        in_specs=[
            pl.BlockSpec((tm, d), row),
            pl.BlockSpec((1, d), fix),
            pl.BlockSpec((d, n), fix),
            pl.BlockSpec((1, 512), fix),
            pl.BlockSpec((1, 512), fix),
            pl.BlockSpec((1, 512), fix),
            pl.BlockSpec((256, 256), fix),
        ],
        out_specs=[
            pl.BlockSpec((tm, 1536), row),
            pl.BlockSpec((tm, 512), row),
            pl.BlockSpec((tm, 512), row),
            pl.BlockSpec((tm, 512), row),
            pl.BlockSpec((tm, 512), row),
            pl.BlockSpec((tm, 512), row),
        ],
        out_shape=[
            jax.ShapeDtypeStruct((t, 1536), BF16),
            jax.ShapeDtypeStruct((t, 512), BF16),
            jax.ShapeDtypeStruct((t, 512), F32),
            jax.ShapeDtypeStruct((t, 512), F32),
            jax.ShapeDtypeStruct((t, 512), BF16),
            jax.ShapeDtypeStruct((t, 512), BF16),
        ],
        compiler_params=_cparams("parallel"),
        name="inproj",
    )(x2, g1, w_bf, gq, gk, lb, bd)


def _sbattn_kernel(q_ref, k_ref, v_ref, go_ref, u2_ref, bd_ref, o_ref, vm_ref, *scr, tq):
    acc_ref = scr[0:NPAIR]
    carry_ref = scr[NPAIR:2 * NPAIR]
    z_ref = scr[2 * NPAIR:3 * NPAIR]
    w_ref = scr[3 * NPAIR:4 * NPAIR]
    qi = pl.program_id(1)
    hps = range(NPAIR)
    lane = lax.broadcasted_iota(I32, (tq, LANES), 1)
    lo = lane < SB_HEAD_DIM

    @pl.when(qi == 0)
    def _():
        v = v_ref[...]
        lo_all = (lax.broadcasted_iota(I32, v.shape, 1) & (LANES - 1)) < SB_HEAD_DIM
        zv = jnp.zeros_like(v)
        vm_ref[0] = jnp.where(lo_all, v, zv)
        vm_ref[1] = jnp.where(lo_all, zv, v)

    u2 = u2_ref[...]
    for hp in hps:
        acc_ref[hp][...] = jnp.zeros_like(acc_ref[hp])
        carry_ref[hp][...] = jnp.zeros_like(carry_ref[hp])
    row = lax.broadcasted_iota(I32, (2 * tq, tq), 0)
    col = lax.broadcasted_iota(I32, (2 * tq, tq), 1)
    strict = col < jnp.where(row >= tq, row - tq, row)
    q2s = []
    for hp in hps:
        q = q_ref[:, hp * LANES:(hp + 1) * LANES]
        zq = jnp.zeros_like(q)
        q2s.append(jnp.concatenate([jnp.where(lo, q, zq), jnp.where(lo, zq, q)], axis=0))

    def scores(kb):
        off = pl.multiple_of(kb * tq, tq)
        return [lax.dot_general(q2s[hp], k_ref[pl.ds(off, tq), hp * LANES:(hp + 1) * LANES], NT_DIMS,
                                preferred_element_type=F32) for hp in hps]

    def weighted_values(kb):
        off = pl.multiple_of(kb * tq, tq)
        for hp in hps:
            vcat = jnp.concatenate([vm_ref[0, pl.ds(off, tq), hp * LANES:(hp + 1) * LANES],
                                    vm_ref[1, pl.ds(off, tq), hp * LANES:(hp + 1) * LANES]], axis=0)
            acc_ref[hp][...] += jnp.dot(w_ref[hp][...], vcat, preferred_element_type=F32)

    def weights(zs, masked):
        lgs = []
        for z in zs:
            nz = -z
            lg = jnp.minimum(nz, 0.0) - jnp.log2(1.0 + jnp.exp2(jnp.minimum(z, nz)))
            if masked:
                lg = jnp.where(strict, lg, 0.0)
            lgs.append(lg)
        withins = []
        for pr in range(NPAIR // 2):
            cat = jnp.concatenate([lgs[2 * pr], lgs[2 * pr + 1]], axis=1).astype(BF16)
            res = jnp.dot(cat, u2, preferred_element_type=F32)
            withins += [res[:, :tq], res[:, tq:]]
        reps = tq // LANES
        carries = [jnp.concatenate([carry_ref[hp][...]] * reps, axis=1) for hp in hps]
        ws = [jnp.exp2((zs[hp] + lgs[hp]) + (carries[hp] + withins[hp])) for hp in hps]
        if masked:
            ws = [jnp.where(strict, w, 0.0) for w in ws]
        for hp in hps:
            carry_ref[hp][...] += jnp.sum(lgs[hp], axis=-1, keepdims=True)
        return [jnp.concatenate([w[:tq], w[tq:]], axis=1).astype(BF16) for w in ws]

    ws = weights(scores(qi), True)
    zn = scores(jnp.maximum(qi - 1, 0))
    for hp in hps:
        w_ref[hp][...] = ws[hp]
        z_ref[hp][...] = zn[hp]

    def body(it, c):
        kb = qi - 1 - it
        zs = [z_ref[hp][...] for hp in hps]
        zn = scores(jnp.maximum(kb - 1, 0))
        weighted_values(kb + 1)
        ws = weights(zs, False)
        for hp in hps:
            w_ref[hp][...] = ws[hp]
            z_ref[hp][...] = zn[hp]
        return c

    lax.fori_loop(0, qi, body, 0)
    weighted_values(0)
    for hp in hps:
        a = acc_ref[hp][...]
        ss = jnp.dot((a * a).astype(BF16), bd_ref[...], preferred_element_type=F32)
        o_ref[:, hp * LANES:(hp + 1) * LANES] = (
            a * lax.rsqrt(ss * (1.0 / SB_HEAD_DIM) + EPS) * go_ref[:, hp * LANES:(hp + 1) * LANES]).astype(BF16)


def _sbattn(qkv, go, u2, bd, *, tq):
    b, s, _ = qkv.shape
    return pl.pallas_call(
        functools.partial(_sbattn_kernel, tq=tq),
        grid=(b, s // tq),
        in_specs=[
            pl.BlockSpec((None, tq, SB_WIDTH), lambda bi, qi: (bi, qi, 0)),
            pl.BlockSpec((None, s, SB_WIDTH), lambda bi, qi: (bi, 0, 1)),
            pl.BlockSpec((None, s, SB_WIDTH), lambda bi, qi: (bi, 0, 2)),
            pl.BlockSpec((1, SB_WIDTH), lambda bi, qi: (0, 0)),
            pl.BlockSpec((2 * tq, 2 * tq), lambda bi, qi: (0, 0)),
            pl.BlockSpec((LANES, LANES), lambda bi, qi: (0, 0)),
        ],
        out_specs=pl.BlockSpec((None, tq, SB_WIDTH), lambda bi, qi: (bi, qi, 0)),
        out_shape=jax.ShapeDtypeStruct((b, s, SB_WIDTH), BF16),
        scratch_shapes=([pltpu.VMEM((2, s, SB_WIDTH), BF16)]
                        + [pltpu.VMEM((tq, LANES), F32)] * NPAIR
                        + [pltpu.VMEM((2 * tq, LANES), F32)] * NPAIR
                        + [pltpu.VMEM((2 * tq, tq), F32)] * NPAIR
                        + [pltpu.VMEM((tq, 2 * tq), BF16)] * NPAIR),
        compiler_params=_cparams("parallel", "arbitrary"),
        name="sbattn",
    )(qkv, qkv, qkv, go, u2, bd)


def _hgrn_kernel(q_ref, g_ref, k_ref, v_ref, sg_ref, gn_ref, tl_ref, o_ref, *state_ref):
    s = q_ref.shape[0]
    grp = LANES
    nch = grp // HG_CHUNK
    hs = range(HG_WIDTH // HG_HEAD_DIM)
    for h in hs:
        state_ref[h][...] = jnp.zeros_like(state_ref[h])
    tl = tl_ref[...]
    rr = lax.broadcasted_iota(I32, (grp, grp), 0)
    cc = lax.broadcasted_iota(I32, (grp, grp), 1)
    shift = HG_CHUNK.bit_length() - 1
    rchunk = rr >> shift
    causal = (rchunk == (cc >> shift)) & (cc <= rr)
    inchunk = [rchunk == ci for ci in range(nch)]

    def group(r, c):
        off = pl.multiple_of(r * grp, grp)
        col = lambda h: slice(h * HG_HEAD_DIM, (h + 1) * HG_HEAD_DIM)
        rows = pl.ds(off, grp)
        vs = [v_ref[rows, col(h)] for h in hs]
        bs = [sum(jnp.dot(tl, p, preferred_element_type=F32) for p in _bf16_split(g_ref[rows, col(h)], 2))
              for h in hs]
        lasts = [[b[(ci + 1) * HG_CHUNK - 1:(ci + 1) * HG_CHUNK, :] for ci in range(nch)] for b in bs]
        bls = [jnp.concatenate([jnp.broadcast_to(l, (HG_CHUNK, HG_HEAD_DIM)) for l in last], axis=0)
               for last in lasts]
        qes = [(q_ref[rows, col(h)].astype(F32) * jnp.exp(bs[h])).astype(BF16) for h in hs]
        kes = [(k_ref[rows, col(h)] * jnp.exp(-bs[h])).astype(BF16) for h in hs]
        kds = [(k_ref[rows, col(h)] * jnp.exp(bls[h] - bs[h])).astype(BF16) for h in hs]
        attns = [jnp.where(causal, lax.dot_general(qes[h], kes[h], NT_DIMS, preferred_element_type=F32), 0.0)
                 .astype(BF16) for h in hs]
        os = [jnp.dot(attns[h], vs[h], preferred_element_type=F32) for h in hs]
        vts = [vs[h].astype(F32).T.astype(BF16) for h in hs]
        uts = []
        for h in hs:
            zk = jnp.zeros_like(kds[h])
            kdx = jnp.concatenate([jnp.where(inchunk[ci], kds[h], zk) for ci in range(nch)], axis=1)
            uts.append(jnp.dot(vts[h], kdx, preferred_element_type=F32))
        for h in hs:
            st = state_ref[h][...]
            parts = []
            for ci in range(nch):
                parts.append(st.astype(BF16))
                st = st * jnp.exp(lasts[h][ci]) + uts[h][:, ci * HG_HEAD_DIM:(ci + 1) * HG_HEAD_DIM]
            state_ref[h][...] = st
            zq = jnp.zeros_like(qes[h])
            qx = jnp.concatenate([jnp.where(inchunk[ci], qes[h], zq) for ci in range(nch)], axis=1)
            o = os[h] + lax.dot_general(qx, jnp.concatenate(parts, axis=1), NT_DIMS, preferred_element_type=F32)
            ms = jnp.mean(o * o, axis=-1, keepdims=True)
            y = o * lax.rsqrt(ms + EPS) * gn_ref[:, col(h)]
            o_ref[rows, col(h)] = (y * sg_ref[rows, col(h)].astype(F32)).astype(BF16)
        return c

    lax.fori_loop(0, s // grp, group, 0, unroll=4)


def _hgrn(hq, hg, hk, hi, sg, gn, tl):
    b, s, w = hq.shape
    nh = HG_WIDTH // HG_HEAD_DIM
    blk = pl.BlockSpec((None, s, w), lambda bi: (bi, 0, 0))
    return pl.pallas_call(
        _hgrn_kernel,
        grid=(b,),
        in_specs=[blk, blk, blk, blk, blk,
                  pl.BlockSpec((1, w), lambda bi: (0, 0)),
                  pl.BlockSpec((LANES, LANES), lambda bi: (0, 0))],
        out_specs=blk,
        out_shape=jax.ShapeDtypeStruct((b, s, w), BF16),
        scratch_shapes=[pltpu.VMEM((HG_HEAD_DIM, HG_HEAD_DIM), F32)] * nh,
        compiler_params=_cparams("parallel"),
        name="hgrn",
    )(hq, hg, hk, hi, sg, gn, tl)


def _outproj_kernel(x_ref, a_ref, r_ref, wo_ref, g2_ref, wr_ref, br_ref, tri_ref,
                    x1_ref, hp_ref, ti_ref, tg_ref, rk_ref, cnt_ref, run_ref):
    i = pl.program_id(0)

    @pl.when(i == 0)
    def _():
        run_ref[...] = jnp.zeros_like(run_ref)

    half = a_ref.shape[1]
    x1 = (x_ref[...]
          + jnp.dot(a_ref[...], wo_ref[:half, :], preferred_element_type=F32)
          + jnp.dot(r_ref[...], wo_ref[half:, :], preferred_element_type=F32))
    x1_ref[...] = x1
    ms = jnp.mean(x1 * x1, axis=-1, keepdims=True)
    h2 = x1 * lax.rsqrt(ms + EPS) * g2_ref[...]
    _store_interleaved(hp_ref, _pack_bf16_pairs(h2))

    h_hi, h_lo = _bf16_split(h2, 2)
    w_hi, w_lo = _bf16_split(wr_ref[...], 2)
    hi_both = jnp.dot(h_hi, jnp.concatenate([w_hi, w_lo], axis=1), preferred_element_type=F32)
    logits = ((hi_both[:, :LANES] + jnp.dot(h_lo, w_hi, preferred_element_type=F32))
              + hi_both[:, LANES:]) + br_ref[...]
    tm = logits.shape[0]
    lane = lax.broadcasted_iota(I32, (tm, LANES), 1).astype(F32)
    neg = jnp.float32(-jnp.inf)
    ti = jnp.zeros((tm, LANES), F32)
    tv = jnp.zeros((tm, LANES), F32)
    onehot = jnp.zeros((tm, LANES), F32)
    sels = []
    work = logits
    for k in range(TOP_K):
        m = jnp.max(work, axis=-1, keepdims=True)
        idx = jnp.min(jnp.where(work == m, lane, float(LANES)), axis=-1, keepdims=True)
        sel = lane == idx
        sels.append(sel)
        ti = jnp.where(lane == k, idx, ti)
        tv = jnp.where(lane == k, m, tv)
        onehot = jnp.where(sel, 1.0, onehot)
        work = jnp.where(sel, neg, work)
    valid = lane < TOP_K
    e = jnp.where(valid, jnp.exp(tv - jnp.max(jnp.where(valid, tv, neg), axis=-1, keepdims=True)), 0.0)
    tg_ref[...] = e / jnp.sum(e, axis=-1, keepdims=True)
    ti_ref[...] = ti.astype(I32)

    before = jnp.dot(tri_ref[...], onehot.astype(BF16), preferred_element_type=F32) + run_ref[...]
    rk = jnp.zeros((tm, LANES), F32)
    for k in range(TOP_K):
        rk = jnp.where(lane == k, jnp.sum(jnp.where(sels[k], before, 0.0), axis=-1, keepdims=True), rk)
    rk_ref[...] = rk.astype(I32)
    run_ref[...] += jnp.sum(onehot, axis=0, keepdims=True)
    cnt_ref[...] = run_ref[...]


def _outproj(x2, a2, r2, wo_bf, g2, wr, br, tri, *, tm):
    t, d = x2.shape
    half = a2.shape[1]
    row = lambda i: (i, 0)
    fix = lambda i: (0, 0)
    return pl.pallas_call(
        _outproj_kernel,
        grid=(t // tm,),
        in_specs=[
            pl.BlockSpec((tm, d), row),
            pl.BlockSpec((tm, half), row),
            pl.BlockSpec((tm, half), row),
            pl.BlockSpec((2 * half, d), fix),
            pl.BlockSpec((1, d), fix),
            pl.BlockSpec((d, LANES), fix),
            pl.BlockSpec((1, LANES), fix),
            pl.BlockSpec((tm, tm), fix),
        ],
        out_specs=[
            pl.BlockSpec((tm, d), row),
            pl.BlockSpec((ROW_GROUPS * tm, LANES), row),
            pl.BlockSpec((tm, LANES), row),
            pl.BlockSpec((tm, LANES), row),
            pl.BlockSpec((tm, LANES), row),
            pl.BlockSpec((1, LANES), fix),
        ],
        out_shape=[
            jax.ShapeDtypeStruct((t, d), F32),
            jax.ShapeDtypeStruct((ROW_GROUPS * t, LANES), U32),
            jax.ShapeDtypeStruct((t, LANES), I32),
            jax.ShapeDtypeStruct((t, LANES), F32),
            jax.ShapeDtypeStruct((t, LANES), I32),
            jax.ShapeDtypeStruct((1, LANES), F32),
        ],
        scratch_shapes=[pltpu.VMEM((1, LANES), F32)],
        compiler_params=_cparams("arbitrary"),
        name="outproj",
    )(x2, a2, r2, wo_bf, g2, wr, br, tri)


def _slab(ref, r):
    return ref.at[pl.ds(pl.multiple_of(ROW_GROUPS * r, ROW_GROUPS), ROW_GROUPS)]


def _slab_copy_wait(src, dst, sem, n):
    pltpu.make_async_copy(src.at[pl.ds(0, ROW_GROUPS * n)], dst.at[pl.ds(0, ROW_GROUPS * n)], sem).wait()


def _scatter_kernel(dest_ref, hp_ref, xz_ref, xb_ref, sem):
    del xz_ref
    tm = hp_ref.shape[0] // ROW_GROUPS
    for t in range(tm):
        for k in range(TOP_K):
            pltpu.make_async_copy(_slab(hp_ref, t), _slab(xb_ref, dest_ref[0, t * TOP_K + k]),
                                  sem).start(priority=k % 2)
    for _ in range(TOP_K):
        _slab_copy_wait(hp_ref, xb_ref, sem, tm)


def _scatter(dest2, hp, xzero, *, tm):
    t = hp.shape[0] // ROW_GROUPS
    return pl.pallas_call(
        _scatter_kernel,
        grid=(t // tm,),
        in_specs=[
            pl.BlockSpec((None, 1, tm * TOP_K), lambda i: (i, 0, 0), memory_space=pltpu.SMEM),
            pl.BlockSpec((ROW_GROUPS * tm, LANES), lambda i: (i, 0)),
            pl.BlockSpec(memory_space=pl.ANY),
        ],
        out_specs=pl.BlockSpec(memory_space=pl.ANY),
        out_shape=jax.ShapeDtypeStruct(xzero.shape, xzero.dtype),
        scratch_shapes=[pltpu.SemaphoreType.DMA],
        input_output_aliases={2: 0},
        compiler_params=_cparams("arbitrary"),
        name="scatter",
    )(dest2, hp, xzero)


def _experts_kernel(be_ref, nu_ref, xb_ref, wgu_ref, bgu_ref, wd_ref, bd_ref, yb_ref, wgu_bf, wd_bf):
    i = pl.program_id(0)
    fresh = jnp.logical_or(i == 0, be_ref[i] != be_ref[jnp.maximum(i - 1, 0)])

    @pl.when(fresh)
    def _():
        wgu_bf[...] = wgu_ref[...].astype(BF16)
        wd_bf[...] = wd_ref[...].astype(BF16)

    @pl.when(i < nu_ref[0])
    def _():
        x_lo, x_hi = _unpack_bf16_pairs(_load_interleaved(xb_ref))
        x = jnp.concatenate([x_lo.astype(BF16), x_hi.astype(BF16)], axis=1)
        gu = jnp.dot(x, wgu_bf[...], preferred_element_type=F32) + bgu_ref[...]
        f = gu.shape[1] // 2
        g_lin = jnp.minimum(gu[:, :f], SWIGLU_LIMIT)
        u_lin = jnp.clip(gu[:, f:], -SWIGLU_LIMIT, SWIGLU_LIMIT)
        act = g_lin * _sigmoid(SWIGLU_ALPHA * g_lin) * (u_lin + 1.0)
        y = jnp.dot(act.astype(BF16), wd_bf[...], preferred_element_type=F32) + bd_ref[...]
        _store_interleaved(yb_ref, _pack_bf16_pairs(y))

    @pl.when(i >= nu_ref[0])
    def _():
        yb_ref[...] = jnp.zeros_like(yb_ref)


def _experts(block_e, n_used, xb, w_gate_up, b_gate_up3, w_down, b_down3, *, rb):
    n_rows = xb.shape[0] // ROW_GROUPS
    d = w_gate_up.shape[1]
    f2 = w_gate_up.shape[2]
    f = w_down.shape[1]
    nblk = n_rows // rb
    rowblk = lambda i, be, nu: (jnp.minimum(i, nu[0] - 1), 0)
    wsel = lambda i, be, nu: (be[i], 0, 0)
    grid_spec = pltpu.PrefetchScalarGridSpec(
        num_scalar_prefetch=2,
        grid=(nblk,),
        in_specs=[
            pl.BlockSpec((ROW_GROUPS * rb, LANES), rowblk),
            pl.BlockSpec((None, d, f2), wsel),
            pl.BlockSpec((None, 1, f2), wsel),
            pl.BlockSpec((None, f, d), wsel),
            pl.BlockSpec((None, 1, d), wsel),
        ],
        out_specs=pl.BlockSpec((ROW_GROUPS * rb, LANES), lambda i, be, nu: (i, 0)),
        scratch_shapes=[pltpu.VMEM((d, f2), BF16), pltpu.VMEM((f, d), BF16)],
    )
    return pl.pallas_call(
        _experts_kernel,
        grid_spec=grid_spec,
        out_shape=jax.ShapeDtypeStruct(xb.shape, U32),
        compiler_params=_cparams("arbitrary"),
        name="experts",
    )(block_e, n_used, xb, w_gate_up, b_gate_up3, w_down, b_down3)


def _combine_kernel(dest0_ref, destn_ref, x1_ref, tg_ref, yb_ref, o_ref, buf, sem):
    i = pl.program_id(0)
    n = pl.num_programs(0)
    tm = x1_ref.shape[0]
    slot = i % 2

    def gather_one(dest_ref, s, t):
        for k in range(TOP_K):
            pltpu.make_async_copy(_slab(yb_ref, dest_ref[0, t * TOP_K + k]), _slab(buf.at[s, k], t),
                                  sem.at[s]).start(priority=k % 2)

    @pl.when(i == 0)
    def _():
        def body(t, c):
            gather_one(dest0_ref, 0, t)
            return c
        lax.fori_loop(0, tm, body, 0)

    @pl.when(i + 1 < n)
    def _():
        for t in range(tm):
            gather_one(destn_ref, 1 - slot, t)

    for k in range(TOP_K):
        _slab_copy_wait(yb_ref, buf.at[slot, k], sem.at[slot], tm)
    tg = tg_ref[...]
    half = x1_ref.shape[1] // 2
    for j in range(ROW_GROUPS):
        lo_cols = slice(j * LANES, (j + 1) * LANES)
        hi_cols = slice(half + j * LANES, half + (j + 1) * LANES)
        out_lo = x1_ref[:, lo_cols]
        out_hi = x1_ref[:, hi_cols]
        for k in range(TOP_K):
            y_lo, y_hi = _unpack_bf16_pairs(buf[slot, k, pl.ds(j, tm, stride=ROW_GROUPS), :])
            out_lo = out_lo + tg[:, k:k + 1] * y_lo
            out_hi = out_hi + tg[:, k:k + 1] * y_hi
        o_ref[:, lo_cols] = out_lo
        o_ref[:, hi_cols] = out_hi


def _combine(dest2, x1, tg, yb, *, tm):
    t, d = x1.shape
    n = t // tm
    return pl.pallas_call(
        _combine_kernel,
        grid=(n,),
        in_specs=[
            pl.BlockSpec((None, 1, tm * TOP_K), lambda i: (0, 0, 0), memory_space=pltpu.SMEM),
            pl.BlockSpec((None, 1, tm * TOP_K), lambda i: (jnp.minimum(i + 1, n - 1), 0, 0),
                         memory_space=pltpu.SMEM),
            pl.BlockSpec((tm, d), lambda i: (i, 0)),
            pl.BlockSpec((tm, LANES), lambda i: (i, 0)),
            pl.BlockSpec(memory_space=pl.ANY),
        ],
        out_specs=pl.BlockSpec((tm, d), lambda i: (i, 0)),
        out_shape=jax.ShapeDtypeStruct((t, d), F32),
        scratch_shapes=[pltpu.VMEM((2, TOP_K, ROW_GROUPS * tm, LANES), yb.dtype), pltpu.SemaphoreType.DMA((2,))],
        compiler_params=_cparams("arbitrary"),
        name="combine",
    )(dest2, dest2, x1, tg, yb)


def _block_ones(n, blk):
    i = jnp.arange(n)
    return (i[:, None] // blk == i[None, :] // blk).astype(BF16)


def _pair_suffix(tq):
    j = jnp.arange(2 * tq)
    return ((j[:, None] > j[None, :]) & (j[:, None] // tq == j[None, :] // tq)).astype(BF16)


def _chunk_tril(n, blk):
    i = jnp.arange(n)
    return ((i[:, None] // blk == i[None, :] // blk) & (i[None, :] <= i[:, None])).astype(BF16)


def _strict_tril(n):
    i = jnp.arange(n)
    return (i[None, :] < i[:, None]).astype(BF16)


def _layer(x, norm1_g, w_in, q_g, k_g, sbo_g, lb, hgo_g, w_out, norm2_g, w_router, b_router,
           w_gate_up, b_gate_up, w_down, b_down):
    bsz, seq, d = x.shape
    rb = RB_EXPERTS
    t = bsz * seq
    x2 = x.reshape(t, d)
    nsb = SB_WIDTH // SB_HEAD_DIM

    qkv, hq, hg, hk, hi, sg = _inproj(
        x2, norm1_g.reshape(1, d), w_in.astype(BF16),
        jnp.tile(q_g, nsb).reshape(1, SB_WIDTH), jnp.tile(k_g, nsb).reshape(1, SB_WIDTH),
        lb.reshape(1, HG_WIDTH), _block_ones(256, SB_HEAD_DIM), tm=TM_INPROJ)

    a = _sbattn(qkv.reshape(bsz, seq, 3 * SB_WIDTH), sbo_g.reshape(1, SB_WIDTH),
                _pair_suffix(TQ_ATTN), _block_ones(LANES, SB_HEAD_DIM), tq=TQ_ATTN)
    sh = (bsz, seq, HG_WIDTH)
    r = _hgrn(hq.reshape(sh), hg.reshape(sh), hk.reshape(sh), hi.reshape(sh), sg.reshape(sh),
              hgo_g.reshape(1, HG_WIDTH), _chunk_tril(LANES, HG_CHUNK))

    wr = jnp.zeros((d, LANES), F32).at[:, :N_EXPERTS].set(w_router)
    br = jnp.full((1, LANES), -1e30, F32).at[0, :N_EXPERTS].set(b_router)
    x1, hp, ti, tg, rk, cnt = _outproj(
        x2, a.reshape(t, SB_WIDTH), r.reshape(t, HG_WIDTH), w_out.astype(BF16),
        norm2_g.reshape(1, d), wr, br, _strict_tril(TM_OUTPROJ), tm=TM_OUTPROJ)

    counts = cnt[0, :N_EXPERTS].astype(I32)
    padded = (counts + rb - 1) // rb * rb
    pad_ends = jnp.cumsum(padded)
    pad_starts = pad_ends - padded
    n_assign = t * TOP_K
    nblk = n_assign // rb + N_EXPERTS
    eids = jnp.arange(N_EXPERTS, dtype=I32)
    dest = rk[:, :TOP_K] + jnp.sum(jnp.where(ti[:, :TOP_K, None] == eids, pad_starts, 0), axis=-1)
    n_used = (pad_ends[-1] // rb).astype(I32).reshape(1)
    blk_start = jnp.minimum(jnp.arange(nblk, dtype=I32), n_used[0] - 1) * rb
    block_e = jnp.sum((blk_start[:, None] >= pad_ends[None, :]).astype(I32), axis=1)

    xb = _scatter(dest.reshape(t // TM_SCATTER, 1, TM_SCATTER * TOP_K), hp,
                  jnp.zeros((ROW_GROUPS * nblk * rb, LANES), U32), tm=TM_SCATTER)
    yb = _experts(block_e, n_used, xb, w_gate_up, b_gate_up.reshape(N_EXPERTS, 1, -1),
                  w_down, b_down.reshape(N_EXPERTS, 1, -1), rb=rb)
    out = _combine(dest.reshape(t // TM_COMBINE, 1, TM_COMBINE * TOP_K), x1, tg, yb, tm=TM_COMBINE)
    return out.reshape(bsz, seq, d)


def kernel(x, norm1_g, w_in, sb_q_norm_g, sb_k_norm_g, sb_out_norm_g, hg_lb_logits, hg_out_norm_g,
           w_out, norm2_g, w_router, b_router, w_gate_up, b_gate_up, w_down, b_down):
    depth = w_in.shape[0]
    lb_all = jnp.cumsum(jax.nn.softmax(hg_lb_logits.astype(F32), axis=0), axis=0)
    for l in range(depth):
        x = _layer(x, norm1_g[l], w_in[l], sb_q_norm_g[l], sb_k_norm_g[l], sb_out_norm_g[l], lb_all[l],
                   hg_out_norm_g[l], w_out[l], norm2_g[l], w_router[l], b_router[l],
                   w_gate_up[l], b_gate_up[l], w_down[l], b_down[l])
    return x
```

```python
import functools

import jax
import jax.numpy as jnp
from jax import lax
from jax.experimental import pallas as pl
from jax.experimental.pallas import tpu as pltpu

F32 = jnp.float32
BF16 = jnp.bfloat16
I32 = jnp.int32
U32 = jnp.uint32

EPS = 1e-6
LANES = 128
SB_HEAD_DIM = 64
SB_WIDTH = 512
HG_HEAD_DIM = 128
HG_WIDTH = 512
HG_CHUNK = 32
N_EXPERTS = 32
TOP_K = 4
SWIGLU_LIMIT = 7.0
SWIGLU_ALPHA = 1.702
VMEM_LIMIT = 56 * 1024 * 1024

TM_INPROJ = 512
TQ_ATTN = 256
TM_OUTPROJ = 512
TM_SCATTER = 512
TM_COMBINE = 256
RB_EXPERTS = 512

SEG = SB_WIDTH
MXU_TILE = 256

NT_DIMS = (((1,), (1,)), ((), ()))
NPAIR = SB_WIDTH // LANES
LOG2E = 1.4426950408889634


def _cparams(*sem):
    return pltpu.CompilerParams(dimension_semantics=sem, vmem_limit_bytes=VMEM_LIMIT)


def _bf16_split(a, n):
    parts = []
    for _ in range(n - 1):
        p = a.astype(BF16)
        parts.append(p)
        a = a - p.astype(F32)
    parts.append(a.astype(BF16))
    return parts


def _sigmoid(a):
    return 1.0 / (1.0 + jnp.exp(-a))


def _pack_bf16_pairs(a):
    w = a.shape[1] // 2
    lo_bits = pltpu.bitcast(a[:, :w].astype(BF16).astype(F32), U32)
    hi_bits = pltpu.bitcast(a[:, w:].astype(BF16).astype(F32), U32)
    return (hi_bits & jnp.uint32(0xFFFF0000)) | (lo_bits >> 16)


def _unpack_bf16_pairs(p):
    return pltpu.bitcast(p << 16, F32), pltpu.bitcast(p & jnp.uint32(0xFFFF0000), F32)


ROW_GROUPS = 4


def _store_interleaved(ref, a):
    rows = a.shape[0]
    for j in range(ROW_GROUPS):
        ref[pl.ds(j, rows, stride=ROW_GROUPS), :] = a[:, j * LANES:(j + 1) * LANES]


def _load_interleaved(ref):
    rows = ref.shape[0] // ROW_GROUPS
    return jnp.concatenate([ref[pl.ds(j, rows, stride=ROW_GROUPS), :] for j in range(ROW_GROUPS)], axis=1)


def _inproj_kernel(x_ref, g1_ref, w_ref, gq_ref, gk_ref, lb_ref, bd_ref,
                   qkv_ref, hq_ref, hg_ref, hk_ref, hi_ref, sg_ref):
    x = x_ref[...]
    ms = jnp.mean(x * x, axis=-1, keepdims=True)
    h = (x * lax.rsqrt(ms + EPS) * g1_ref[...]).astype(BF16)
    bd = bd_ref[...]

    def seg(j):
        return jnp.dot(h, w_ref[:, j * SEG:(j + 1) * SEG], preferred_element_type=F32)

    def head_norm(a, g, scale):
        outs = []
        for c in range(SEG // MXU_TILE):
            ac = a[:, c * MXU_TILE:(c + 1) * MXU_TILE]
            ss = jnp.dot((ac * ac).astype(BF16), bd, preferred_element_type=F32)
            outs.append(ac * lax.rsqrt(ss * (1.0 / SB_HEAD_DIM) + EPS))
        y = jnp.concatenate(outs, axis=1) * g
        return y * scale if scale != 1.0 else y

    qkv_ref[:, 0:SEG] = head_norm(seg(0), gq_ref[...], LOG2E * SB_HEAD_DIM ** -0.5).astype(BF16)
    qkv_ref[:, SEG:2 * SEG] = head_norm(seg(1), gk_ref[...], 1.0).astype(BF16)
    qkv_ref[:, 2 * SEG:3 * SEG] = seg(2).astype(BF16)
    hq_ref[...] = seg(3).astype(BF16)
    f = seg(4)
    lb = lb_ref[...]
    hg_ref[...] = jnp.log(lb + (1.0 - lb) * _sigmoid(f))
    hk_ref[...] = (1.0 - lb) * _sigmoid(-f)
    hi_ref[...] = seg(5).astype(BF16)
    gate = seg(6)
    sg_ref[...] = (gate * _sigmoid(gate)).astype(BF16)


def _inproj(x2, g1, w_bf, gq, gk, lb, bd, *, tm):
    t, d = x2.shape
    n = w_bf.shape[1]
    row = lambda i: (i, 0)
    fix = lambda i: (0, 0)
    return pl.pallas_call(
        _inproj_kernel,
        grid=(t // tm,),
        in_specs=[
            pl.BlockSpec((tm, d), row),
            pl.BlockSpec((1, d), fix),
            pl.BlockSpec((d, n), fix),
            pl.BlockSpec((1, SEG), fix),
            pl.BlockSpec((1, SEG), fix),
            pl.BlockSpec((1, SEG), fix),
            pl.BlockSpec((MXU_TILE, MXU_TILE), fix),
        ],
        out_specs=[
            pl.BlockSpec((tm, 3 * SEG), row),
            pl.BlockSpec((tm, SEG), row),
            pl.BlockSpec((tm, SEG), row),
            pl.BlockSpec((tm, SEG), row),
            pl.BlockSpec((tm, SEG), row),
            pl.BlockSpec((tm, SEG), row),
        ],
        out_shape=[
            jax.ShapeDtypeStruct((t, 3 * SEG), BF16),
            jax.ShapeDtypeStruct((t, SEG), BF16),
            jax.ShapeDtypeStruct((t, SEG), F32),
            jax.ShapeDtypeStruct((t, SEG), F32),
            jax.ShapeDtypeStruct((t, SEG), BF16),
            jax.ShapeDtypeStruct((t, SEG), BF16),
        ],
        compiler_params=_cparams("parallel"),
        name="inproj",
    )(x2, g1, w_bf, gq, gk, lb, bd)


def _sbattn_kernel(q_ref, k_ref, v_ref, go_ref, u2_ref, bd_ref, o_ref, zero_ref, vm_ref, *scr, tq):
    acc_ref = scr[0:NPAIR]
    carry_ref = scr[NPAIR:2 * NPAIR]
    z_ref = scr[2 * NPAIR:3 * NPAIR]
    w_ref = scr[3 * NPAIR:4 * NPAIR]
    qi = pl.program_id(1)
    hps = range(NPAIR)
    lane = lax.broadcasted_iota(I32, (tq, LANES), 1)
    lo = lane < SB_HEAD_DIM

    @pl.when(qi == 0)
    def _():
        v = v_ref[...]
        lo_all = (lax.broadcasted_iota(I32, v.shape, 1) & (LANES - 1)) < SB_HEAD_DIM
        zv = jnp.zeros_like(v)
        vm_ref[0] = jnp.where(lo_all, v, zv)
        vm_ref[1] = jnp.where(lo_all, zv, v)

    u2 = u2_ref[...]
    for hp in hps:
        acc_ref[hp][...] = jnp.zeros_like(acc_ref[hp])
        carry_ref[hp][...] = jnp.zeros_like(carry_ref[hp])
    row = lax.broadcasted_iota(I32, (2 * tq, tq), 0)
    col = lax.broadcasted_iota(I32, (2 * tq, tq), 1)
    strict = col < jnp.where(row >= tq, row - tq, row)
    q2s = []
    for hp in hps:
        q = q_ref[:, hp * LANES:(hp + 1) * LANES]
        zq = jnp.zeros_like(q)
        q2s.append(jnp.concatenate([jnp.where(lo, q, zq), jnp.where(lo, zq, q)], axis=0))

    def scores(kb):
        off = pl.multiple_of(kb * tq, tq)
        return [lax.dot_general(q2s[hp], k_ref[pl.ds(off, tq), hp * LANES:(hp + 1) * LANES], NT_DIMS,
                                preferred_element_type=F32) for hp in hps]

    def weighted_values(kb):
        off = pl.multiple_of(kb * tq, tq)
        for hp in hps:
            vcat = jnp.concatenate([vm_ref[0, pl.ds(off, tq), hp * LANES:(hp + 1) * LANES],
                                    vm_ref[1, pl.ds(off, tq), hp * LANES:(hp + 1) * LANES]], axis=0)
            acc_ref[hp][...] += jnp.dot(w_ref[hp][...], vcat, preferred_element_type=F32)

    def weights(zs, masked):
        lgs = []
        for z in zs:
            nz = -z
            lg = jnp.minimum(nz, 0.0) - jnp.log2(1.0 + jnp.exp2(jnp.minimum(z, nz)))
            if masked:
                lg = jnp.where(strict, lg, 0.0)
            lgs.append(lg)
        withins = []
        for pr in range(NPAIR // 2):
            cat = jnp.concatenate([lgs[2 * pr], lgs[2 * pr + 1]], axis=1).astype(BF16)
            res = jnp.dot(cat, u2, preferred_element_type=F32)
            withins += [res[:, :tq], res[:, tq:]]
        reps = tq // LANES
        carries = [jnp.concatenate([carry_ref[hp][...]] * reps, axis=1) for hp in hps]
        ws = [jnp.exp2((zs[hp] + lgs[hp]) + (carries[hp] + withins[hp])) for hp in hps]
        if masked:
            ws = [jnp.where(strict, w, 0.0) for w in ws]
        for hp in hps:
            carry_ref[hp][...] += jnp.sum(lgs[hp], axis=-1, keepdims=True)
        return [jnp.concatenate([w[:tq], w[tq:]], axis=1).astype(BF16) for w in ws]

    ws = weights(scores(qi), True)
    zn = scores(jnp.maximum(qi - 1, 0))
    for hp in hps:
        w_ref[hp][...] = ws[hp]
        z_ref[hp][...] = zn[hp]

    def body(it, c):
        kb = qi - 1 - it
        zs = [z_ref[hp][...] for hp in hps]
        zn = scores(jnp.maximum(kb - 1, 0))
        weighted_values(kb + 1)
        ws = weights(zs, False)
        for hp in hps:
            w_ref[hp][...] = ws[hp]
            z_ref[hp][...] = zn[hp]
        return c

    lax.fori_loop(0, qi, body, 0)
    weighted_values(0)
    for hp in hps:
        a = acc_ref[hp][...]
        ss = jnp.dot((a * a).astype(BF16), bd_ref[...], preferred_element_type=F32)
        o_ref[:, hp * LANES:(hp + 1) * LANES] = (
            a * lax.rsqrt(ss * (1.0 / SB_HEAD_DIM) + EPS) * go_ref[:, hp * LANES:(hp + 1) * LANES]).astype(BF16)
    zero_ref[...] = jnp.zeros_like(zero_ref)


def _sbattn(qkv, go, u2, bd, *, tq, zero_rows):
    b, s, _ = qkv.shape
    nq = s // tq
    zrows = zero_rows // (b * nq)
    assert zrows * b * nq == zero_rows and zrows % 8 == 0
    return pl.pallas_call(
        functools.partial(_sbattn_kernel, tq=tq),
        grid=(b, s // tq),
        in_specs=[
            pl.BlockSpec((None, tq, SB_WIDTH), lambda bi, qi: (bi, qi, 0)),
            pl.BlockSpec((None, s, SB_WIDTH), lambda bi, qi: (bi, 0, 1)),
            pl.BlockSpec((None, s, SB_WIDTH), lambda bi, qi: (bi, 0, 2)),
            pl.BlockSpec((1, SB_WIDTH), lambda bi, qi: (0, 0)),
            pl.BlockSpec((2 * tq, 2 * tq), lambda bi, qi: (0, 0)),
            pl.BlockSpec((LANES, LANES), lambda bi, qi: (0, 0)),
        ],
        out_specs=[pl.BlockSpec((None, tq, SB_WIDTH), lambda bi, qi: (bi, qi, 0)),
                   pl.BlockSpec((zrows, LANES), lambda bi, qi: (bi * nq + qi, 0))],
        out_shape=[jax.ShapeDtypeStruct((b, s, SB_WIDTH), BF16),
                   jax.ShapeDtypeStruct((zero_rows, LANES), U32)],
        scratch_shapes=([pltpu.VMEM((2, s, SB_WIDTH), BF16)]
                        + [pltpu.VMEM((tq, LANES), F32)] * NPAIR
                        + [pltpu.VMEM((2 * tq, LANES), F32)] * NPAIR
                        + [pltpu.VMEM((2 * tq, tq), F32)] * NPAIR
                        + [pltpu.VMEM((tq, 2 * tq), BF16)] * NPAIR),
        compiler_params=_cparams("parallel", "arbitrary"),
        name="sbattn",
    )(qkv, qkv, qkv, go, u2, bd)


def _hgrn_kernel(q_ref, g_ref, k_ref, v_ref, sg_ref, gn_ref, tl_ref, o_ref, *state_ref):
    s = q_ref.shape[0]
    grp = LANES
    nch = grp // HG_CHUNK
    hs = range(HG_WIDTH // HG_HEAD_DIM)
    for h in hs:
        state_ref[h][...] = jnp.zeros_like(state_ref[h])
    tl = tl_ref[...]
    rr = lax.broadcasted_iota(I32, (grp, grp), 0)
    cc = lax.broadcasted_iota(I32, (grp, grp), 1)
    shift = HG_CHUNK.bit_length() - 1
    rchunk = rr >> shift
    causal = (rchunk == (cc >> shift)) & (cc <= rr)
    inchunk = [rchunk == ci for ci in range(nch)]

    def group(r, c):
        off = pl.multiple_of(r * grp, grp)
        col = lambda h: slice(h * HG_HEAD_DIM, (h + 1) * HG_HEAD_DIM)
        rows = pl.ds(off, grp)
        vs = [v_ref[rows, col(h)] for h in hs]
        bs = [sum(jnp.dot(tl, p, preferred_element_type=F32) for p in _bf16_split(g_ref[rows, col(h)], 2))
              for h in hs]
        lasts = [[b[(ci + 1) * HG_CHUNK - 1:(ci + 1) * HG_CHUNK, :] for ci in range(nch)] for b in bs]
        bls = [jnp.concatenate([jnp.broadcast_to(l, (HG_CHUNK, HG_HEAD_DIM)) for l in last], axis=0)
               for last in lasts]
        qes = [(q_ref[rows, col(h)].astype(F32) * jnp.exp(bs[h])).astype(BF16) for h in hs]
        kes = [(k_ref[rows, col(h)] * jnp.exp(-bs[h])).astype(BF16) for h in hs]
        kds = [(k_ref[rows, col(h)] * jnp.exp(bls[h] - bs[h])).astype(BF16) for h in hs]
        attns = [jnp.where(causal, lax.dot_general(qes[h], kes[h], NT_DIMS, preferred_element_type=F32), 0.0)
                 .astype(BF16) for h in hs]
        os = [jnp.dot(attns[h], vs[h], preferred_element_type=F32) for h in hs]
        vts = [vs[h].astype(F32).T.astype(BF16) for h in hs]
        uts = []
        for h in hs:
            zk = jnp.zeros_like(kds[h])
            kdx = jnp.concatenate([jnp.where(inchunk[ci], kds[h], zk) for ci in range(nch)], axis=1)
            uts.append(jnp.dot(vts[h], kdx, preferred_element_type=F32))
        for h in hs:
            st = state_ref[h][...]
            parts = []
            for ci in range(nch):
                parts.append(st.astype(BF16))
                st = st * jnp.exp(lasts[h][ci]) + uts[h][:, ci * HG_HEAD_DIM:(ci + 1) * HG_HEAD_DIM]
            state_ref[h][...] = st
            zq = jnp.zeros_like(qes[h])
            qx = jnp.concatenate([jnp.where(inchunk[ci], qes[h], zq) for ci in range(nch)], axis=1)
            o = os[h] + lax.dot_general(qx, jnp.concatenate(parts, axis=1), NT_DIMS, preferred_element_type=F32)
            ms = jnp.mean(o * o, axis=-1, keepdims=True)
            y = o * lax.rsqrt(ms + EPS) * gn_ref[:, col(h)]
            o_ref[rows, col(h)] = (y * sg_ref[rows, col(h)].astype(F32)).astype(BF16)
        return c

    lax.fori_loop(0, s // grp, group, 0, unroll=4)


def _hgrn(hq, hg, hk, hi, sg, gn, tl):
    b, s, w = hq.shape
    nh = HG_WIDTH // HG_HEAD_DIM
    blk = pl.BlockSpec((None, s, w), lambda bi: (bi, 0, 0))
    return pl.pallas_call(
        _hgrn_kernel,
        grid=(b,),
        in_specs=[blk, blk, blk, blk, blk,
                  pl.BlockSpec((1, w), lambda bi: (0, 0)),
                  pl.BlockSpec((LANES, LANES), lambda bi: (0, 0))],
        out_specs=blk,
        out_shape=jax.ShapeDtypeStruct((b, s, w), BF16),
        scratch_shapes=[pltpu.VMEM((HG_HEAD_DIM, HG_HEAD_DIM), F32)] * nh,
        compiler_params=_cparams("parallel"),
        name="hgrn",
    )(hq, hg, hk, hi, sg, gn, tl)


def _outproj_kernel(x_ref, a_ref, r_ref, wo_ref, g2_ref, wr_ref, br_ref, tri_ref,
                    x1_ref, hp_ref, ti_ref, tg_ref, rk_ref, cnt_ref, run_ref):
    i = pl.program_id(0)

    @pl.when(i == 0)
    def _():
        run_ref[...] = jnp.zeros_like(run_ref)

    half = a_ref.shape[1]
    x1 = (x_ref[...]
          + jnp.dot(a_ref[...], wo_ref[:half, :], preferred_element_type=F32)
          + jnp.dot(r_ref[...], wo_ref[half:, :], preferred_element_type=F32))
    x1_ref[...] = x1
    ms = jnp.mean(x1 * x1, axis=-1, keepdims=True)
    h2 = x1 * lax.rsqrt(ms + EPS) * g2_ref[...]
    _store_interleaved(hp_ref, _pack_bf16_pairs(h2))

    h_hi, h_lo = _bf16_split(h2, 2)
    w_hi, w_lo = _bf16_split(wr_ref[...], 2)
    hi_both = jnp.dot(h_hi, jnp.concatenate([w_hi, w_lo], axis=1), preferred_element_type=F32)
    logits = ((hi_both[:, :LANES] + jnp.dot(h_lo, w_hi, preferred_element_type=F32))
              + hi_both[:, LANES:]) + br_ref[...]
    tm = logits.shape[0]
    lane = lax.broadcasted_iota(I32, (tm, LANES), 1).astype(F32)
    neg = jnp.float32(-jnp.inf)
    ti = jnp.zeros((tm, LANES), F32)
    tv = jnp.zeros((tm, LANES), F32)
    onehot = jnp.zeros((tm, LANES), F32)
    sels = []
    avail = lane < N_EXPERTS
    for k in range(TOP_K):
        m = jnp.max(jnp.where(avail, logits, neg), axis=-1, keepdims=True)
        idx = jnp.min(jnp.where(avail & (logits == m), lane, float(LANES)), axis=-1, keepdims=True)
        sel = lane == idx
        sels.append(sel)
        ti = jnp.where(lane == k, idx, ti)
        tv = jnp.where(lane == k, m, tv)
        onehot = jnp.where(sel, 1.0, onehot)
        avail = avail & jnp.logical_not(sel)
    valid = lane < TOP_K
    e = jnp.where(valid, jnp.exp(tv - jnp.max(jnp.where(valid, tv, neg), axis=-1, keepdims=True)), 0.0)
    tg_ref[...] = e / jnp.sum(e, axis=-1, keepdims=True)
    ti_ref[...] = ti.astype(I32)

    before = jnp.dot(tri_ref[...], onehot.astype(BF16), preferred_element_type=F32) + run_ref[...]
    rk = jnp.zeros((tm, LANES), F32)
    for k in range(TOP_K):
        rk = jnp.where(lane == k, jnp.sum(jnp.where(sels[k], before, 0.0), axis=-1, keepdims=True), rk)
    rk_ref[...] = rk.astype(I32)
    run_ref[...] += jnp.sum(onehot, axis=0, keepdims=True)
    cnt_ref[...] = run_ref[...]


def _outproj(x2, a2, r2, wo_bf, g2, wr, br, tri, *, tm):
    t, d = x2.shape
    half = a2.shape[1]
    row = lambda i: (i, 0)
    fix = lambda i: (0, 0)
    return pl.pallas_call(
        _outproj_kernel,
        grid=(t // tm,),
        in_specs=[
            pl.BlockSpec((tm, d), row),
            pl.BlockSpec((tm, half), row),
            pl.BlockSpec((tm, half), row),
            pl.BlockSpec((2 * half, d), fix),
            pl.BlockSpec((1, d), fix),
            pl.BlockSpec((d, LANES), fix),
            pl.BlockSpec((1, LANES), fix),
            pl.BlockSpec((tm, tm), fix),
        ],
        out_specs=[
            pl.BlockSpec((tm, d), row),
            pl.BlockSpec((ROW_GROUPS * tm, LANES), row),
            pl.BlockSpec((tm, LANES), row),
            pl.BlockSpec((tm, LANES), row),
            pl.BlockSpec((tm, LANES), row),
            pl.BlockSpec((1, LANES), fix),
        ],
        out_shape=[
            jax.ShapeDtypeStruct((t, d), F32),
            jax.ShapeDtypeStruct((ROW_GROUPS * t, LANES), U32),
            jax.ShapeDtypeStruct((t, LANES), I32),
            jax.ShapeDtypeStruct((t, LANES), F32),
            jax.ShapeDtypeStruct((t, LANES), I32),
            jax.ShapeDtypeStruct((1, LANES), F32),
        ],
        scratch_shapes=[pltpu.VMEM((1, LANES), F32)],
        compiler_params=_cparams("arbitrary"),
        name="outproj",
    )(x2, a2, r2, wo_bf, g2, wr, br, tri)


def _slab(ref, r):
    return ref.at[pl.ds(pl.multiple_of(ROW_GROUPS * r, ROW_GROUPS), ROW_GROUPS)]


def _slab_copy_wait(src, dst, sem, n):
    pltpu.make_async_copy(src.at[pl.ds(0, ROW_GROUPS * n)], dst.at[pl.ds(0, ROW_GROUPS * n)], sem).wait()


def _scatter_kernel(dest_ref, hp_ref, xz_ref, xb_ref, sem):
    del xz_ref
    tm = hp_ref.shape[0] // ROW_GROUPS
    for t in range(tm):
        for k in range(TOP_K):
            pltpu.make_async_copy(_slab(hp_ref, t), _slab(xb_ref, dest_ref[0, t * TOP_K + k]),
                                  sem).start(priority=k % 2)
    for _ in range(TOP_K):
        _slab_copy_wait(hp_ref, xb_ref, sem, tm)


def _scatter(dest2, hp, xzero, *, tm):
    t = hp.shape[0] // ROW_GROUPS
    return pl.pallas_call(
        _scatter_kernel,
        grid=(t // tm,),
        in_specs=[
            pl.BlockSpec((None, 1, tm * TOP_K), lambda i: (i, 0, 0), memory_space=pltpu.SMEM),
            pl.BlockSpec((ROW_GROUPS * tm, LANES), lambda i: (i, 0)),
            pl.BlockSpec(memory_space=pl.ANY),
        ],
        out_specs=pl.BlockSpec(memory_space=pl.ANY),
        out_shape=jax.ShapeDtypeStruct(xzero.shape, xzero.dtype),
        scratch_shapes=[pltpu.SemaphoreType.DMA],
        input_output_aliases={2: 0},
        compiler_params=_cparams("arbitrary"),
        name="scatter",
    )(dest2, hp, xzero)


def _experts_kernel(be_ref, nu_ref, xb_ref, wgu_ref, bgu_ref, wd_ref, bd_ref, yb_ref, wgu_bf, wd_bf):
    i = pl.program_id(0)
    fresh = jnp.logical_or(i == 0, be_ref[i] != be_ref[jnp.maximum(i - 1, 0)])

    @pl.when(fresh)
    def _():
        wgu_bf[...] = wgu_ref[...].astype(BF16)
        wd_bf[...] = wd_ref[...].astype(BF16)

    @pl.when(i < nu_ref[0])
    def _():
        x_lo, x_hi = _unpack_bf16_pairs(_load_interleaved(xb_ref))
        x = jnp.concatenate([x_lo.astype(BF16), x_hi.astype(BF16)], axis=1)
        gu = jnp.dot(x, wgu_bf[...], preferred_element_type=F32) + bgu_ref[...]
        f = gu.shape[1] // 2
        g_lin = jnp.minimum(gu[:, :f], SWIGLU_LIMIT)
        u_lin = jnp.clip(gu[:, f:], -SWIGLU_LIMIT, SWIGLU_LIMIT)
        act = g_lin * _sigmoid(SWIGLU_ALPHA * g_lin) * (u_lin + 1.0)
        y = jnp.dot(act.astype(BF16), wd_bf[...], preferred_element_type=F32) + bd_ref[...]
        _store_interleaved(yb_ref, _pack_bf16_pairs(y))

    @pl.when(i >= nu_ref[0])
    def _():
        yb_ref[...] = jnp.zeros_like(yb_ref)


def _experts(block_e, n_used, xb, w_gate_up, b_gate_up3, w_down, b_down3, *, rb):
    n_rows = xb.shape[0] // ROW_GROUPS
    d = w_gate_up.shape[1]
    f2 = w_gate_up.shape[2]
    f = w_down.shape[1]
    nblk = n_rows // rb
    rowblk = lambda i, be, nu: (jnp.minimum(i, nu[0] - 1), 0)
    wsel = lambda i, be, nu: (be[i], 0, 0)
    grid_spec = pltpu.PrefetchScalarGridSpec(
        num_scalar_prefetch=2,
        grid=(nblk,),
        in_specs=[
            pl.BlockSpec((ROW_GROUPS * rb, LANES), rowblk),
            pl.BlockSpec((None, d, f2), wsel),
            pl.BlockSpec((None, 1, f2), wsel),
            pl.BlockSpec((None, f, d), wsel),
            pl.BlockSpec((None, 1, d), wsel),
        ],
        out_specs=pl.BlockSpec((ROW_GROUPS * rb, LANES), lambda i, be, nu: (i, 0)),
        scratch_shapes=[pltpu.VMEM((d, f2), BF16), pltpu.VMEM((f, d), BF16)],
    )
    return pl.pallas_call(
        _experts_kernel,
        grid_spec=grid_spec,
        out_shape=jax.ShapeDtypeStruct(xb.shape, U32),
        compiler_params=_cparams("arbitrary"),
        name="experts",
    )(block_e, n_used, xb, w_gate_up, b_gate_up3, w_down, b_down3)


def _combine_kernel(dest0_ref, destn_ref, x1_ref, tg_ref, yb_ref, o_ref, buf, sem):
    i = pl.program_id(0)
    n = pl.num_programs(0)
    tm = x1_ref.shape[0]
    slot = i % 2

    def gather_one(dest_ref, s, t):
        for k in range(TOP_K):
            pltpu.make_async_copy(_slab(yb_ref, dest_ref[0, t * TOP_K + k]), _slab(buf.at[s, k], t),
                                  sem.at[s]).start(priority=k % 2)

    @pl.when(i == 0)
    def _():
        def body(t, c):
            gather_one(dest0_ref, 0, t)
            return c
        lax.fori_loop(0, tm, body, 0)

    @pl.when(i + 1 < n)
    def _():
        for t in range(tm):
            gather_one(destn_ref, 1 - slot, t)

    for k in range(TOP_K):
        _slab_copy_wait(yb_ref, buf.at[slot, k], sem.at[slot], tm)
    tg = tg_ref[...]
    half = x1_ref.shape[1] // 2
    for j in range(ROW_GROUPS):
        lo_cols = slice(j * LANES, (j + 1) * LANES)
        hi_cols = slice(half + j * LANES, half + (j + 1) * LANES)
        out_lo = x1_ref[:, lo_cols]
        out_hi = x1_ref[:, hi_cols]
        for k in range(TOP_K):
            y_lo, y_hi = _unpack_bf16_pairs(buf[slot, k, pl.ds(j, tm, stride=ROW_GROUPS), :])
            out_lo = out_lo + tg[:, k:k + 1] * y_lo
            out_hi = out_hi + tg[:, k:k + 1] * y_hi
        o_ref[:, lo_cols] = out_lo
        o_ref[:, hi_cols] = out_hi


def _combine(dest2, x1, tg, yb, *, tm):
    t, d = x1.shape
    n = t // tm
    return pl.pallas_call(
        _combine_kernel,
        grid=(n,),
        in_specs=[
            pl.BlockSpec((None, 1, tm * TOP_K), lambda i: (0, 0, 0), memory_space=pltpu.SMEM),
            pl.BlockSpec((None, 1, tm * TOP_K), lambda i: (jnp.minimum(i + 1, n - 1), 0, 0),
                         memory_space=pltpu.SMEM),
            pl.BlockSpec((tm, d), lambda i: (i, 0)),
            pl.BlockSpec((tm, LANES), lambda i: (i, 0)),
            pl.BlockSpec(memory_space=pl.ANY),
        ],
        out_specs=pl.BlockSpec((tm, d), lambda i: (i, 0)),
        out_shape=jax.ShapeDtypeStruct((t, d), F32),
        scratch_shapes=[pltpu.VMEM((2, TOP_K, ROW_GROUPS * tm, LANES), yb.dtype), pltpu.SemaphoreType.DMA((2,))],
        compiler_params=_cparams("arbitrary"),
        name="combine",
    )(dest2, dest2, x1, tg, yb)


def _block_ones(n, blk):
    i = jnp.arange(n)
    return (i[:, None] // blk == i[None, :] // blk).astype(BF16)


def _pair_suffix(tq):
    j = jnp.arange(2 * tq)
    return ((j[:, None] > j[None, :]) & (j[:, None] // tq == j[None, :] // tq)).astype(BF16)


def _chunk_tril(n, blk):
    i = jnp.arange(n)
    return ((i[:, None] // blk == i[None, :] // blk) & (i[None, :] <= i[:, None])).astype(BF16)


def _strict_tril(n):
    i = jnp.arange(n)
    return (i[None, :] < i[:, None]).astype(BF16)


def _layer(x, norm1_g, w_in, q_g, k_g, sbo_g, lb, hgo_g, w_out, norm2_g, w_router, b_router,
           w_gate_up, b_gate_up, w_down, b_down):
    bsz, seq, d = x.shape
    rb = RB_EXPERTS
    t = bsz * seq
    x2 = x.reshape(t, d)
    nsb = SB_WIDTH // SB_HEAD_DIM

    qkv, hq, hg, hk, hi, sg = _inproj(
        x2, norm1_g.reshape(1, d), w_in.astype(BF16),
        jnp.tile(q_g, nsb).reshape(1, SB_WIDTH), jnp.tile(k_g, nsb).reshape(1, SB_WIDTH),
        lb.reshape(1, HG_WIDTH), _block_ones(MXU_TILE, SB_HEAD_DIM), tm=TM_INPROJ)

    n_assign = t * TOP_K
    nblk = n_assign // rb + N_EXPERTS
    a, xzero = _sbattn(qkv.reshape(bsz, seq, 3 * SB_WIDTH), sbo_g.reshape(1, SB_WIDTH),
                       _pair_suffix(TQ_ATTN), _block_ones(LANES, SB_HEAD_DIM), tq=TQ_ATTN,
                       zero_rows=ROW_GROUPS * nblk * rb)
    sh = (bsz, seq, HG_WIDTH)
    r = _hgrn(hq.reshape(sh), hg.reshape(sh), hk.reshape(sh), hi.reshape(sh), sg.reshape(sh),
              hgo_g.reshape(1, HG_WIDTH), _chunk_tril(LANES, HG_CHUNK))

    wr = jnp.zeros((d, LANES), F32).at[:, :N_EXPERTS].set(w_router)
    br = jnp.zeros((1, LANES), F32).at[0, :N_EXPERTS].set(b_router)
    x1, hp, ti, tg, rk, cnt = _outproj(
        x2, a.reshape(t, SB_WIDTH), r.reshape(t, HG_WIDTH), w_out.astype(BF16),
        norm2_g.reshape(1, d), wr, br, _strict_tril(TM_OUTPROJ), tm=TM_OUTPROJ)

    counts = cnt[0, :N_EXPERTS].astype(I32)
    padded = (counts + rb - 1) // rb * rb
    pad_ends = jnp.cumsum(padded)
    pad_starts = pad_ends - padded
    eids = jnp.arange(N_EXPERTS, dtype=I32)
    dest = rk[:, :TOP_K] + jnp.sum(jnp.where(ti[:, :TOP_K, None] == eids, pad_starts, 0), axis=-1)
    n_used = (pad_ends[-1] // rb).astype(I32).reshape(1)
    blk_start = jnp.minimum(jnp.arange(nblk, dtype=I32), n_used[0] - 1) * rb
    block_e = jnp.sum((blk_start[:, None] >= pad_ends[None, :]).astype(I32), axis=1)

    xb = _scatter(dest.reshape(t // TM_SCATTER, 1, TM_SCATTER * TOP_K), hp, xzero, tm=TM_SCATTER)
    yb = _experts(block_e, n_used, xb, w_gate_up, b_gate_up.reshape(N_EXPERTS, 1, -1),
                  w_down, b_down.reshape(N_EXPERTS, 1, -1), rb=rb)
    out = _combine(dest.reshape(t // TM_COMBINE, 1, TM_COMBINE * TOP_K), x1, tg, yb, tm=TM_COMBINE)
    return out.reshape(bsz, seq, d)


def kernel(x, norm1_g, w_in, sb_q_norm_g, sb_k_norm_g, sb_out_norm_g, hg_lb_logits, hg_out_norm_g,
           w_out, norm2_g, w_router, b_router, w_gate_up, b_gate_up, w_down, b_down):
    depth = w_in.shape[0]
    lb_all = jnp.cumsum(jax.nn.softmax(hg_lb_logits.astype(F32), axis=0), axis=0)
    for l in range(depth):
        x = _layer(x, norm1_g[l], w_in[l], sb_q_norm_g[l], sb_k_norm_g[l], sb_out_norm_g[l], lb_all[l],
                   hg_out_norm_g[l], w_out[l], norm2_g[l], w_router[l], b_router[l],
                   w_gate_up[l], b_gate_up[l], w_down[l], b_down[l])
    return x
```

```python
import functools

import jax
import jax.numpy as jnp
from jax import lax
from jax.experimental import pallas as pl
from jax.experimental.pallas import tpu as pltpu

F32 = jnp.float32
BF16 = jnp.bfloat16
I32 = jnp.int32
U32 = jnp.uint32

EPS = 1e-6
LANES = 128
SB_HEAD_DIM = 64
SB_WIDTH = 512
HG_HEAD_DIM = 128
HG_WIDTH = 512
HG_CHUNK = 32
N_EXPERTS = 32
TOP_K = 4
SWIGLU_LIMIT = 7.0
SWIGLU_ALPHA = 1.702
VMEM_LIMIT = 56 * 1024 * 1024

TM_INPROJ = 512
TQ_ATTN = 256
TM_OUTPROJ = 512
TM_SCATTER = 2048
TM_COMBINE = 256
RB_EXPERTS = 512

SEG = SB_WIDTH
MXU_TILE = 256

NT_DIMS = (((1,), (1,)), ((), ()))
NPAIR = SB_WIDTH // LANES
LOG2E = 1.4426950408889634


def _cparams(*sem):
    return pltpu.CompilerParams(dimension_semantics=sem, vmem_limit_bytes=VMEM_LIMIT)


def _bf16_split(a, n):
    parts = []
    for _ in range(n - 1):
        p = a.astype(BF16)
        parts.append(p)
        a = a - p.astype(F32)
    parts.append(a.astype(BF16))
    return parts


def _sigmoid(a):
    return 1.0 / (1.0 + jnp.exp(-a))


def _pack_bf16_pairs(a):
    w = a.shape[1] // 2
    lo_bits = pltpu.bitcast(a[:, :w].astype(BF16).astype(F32), U32)
    hi_bits = pltpu.bitcast(a[:, w:].astype(BF16).astype(F32), U32)
    return (hi_bits & jnp.uint32(0xFFFF0000)) | (lo_bits >> 16)


def _unpack_bf16_pairs(p):
    return pltpu.bitcast(p << 16, F32), pltpu.bitcast(p & jnp.uint32(0xFFFF0000), F32)


ROW_GROUPS = 4


def _store_interleaved(ref, a):
    rows = a.shape[0]
    for j in range(ROW_GROUPS):
        ref[pl.ds(j, rows, stride=ROW_GROUPS), :] = a[:, j * LANES:(j + 1) * LANES]


def _load_interleaved(ref):
    rows = ref.shape[0] // ROW_GROUPS
    return jnp.concatenate([ref[pl.ds(j, rows, stride=ROW_GROUPS), :] for j in range(ROW_GROUPS)], axis=1)


def _inproj_kernel(x_ref, g1_ref, w_ref, gq_ref, gk_ref, lb_ref, bd_ref,
                   qkv_ref, hq_ref, hg_ref, hk_ref, hi_ref, sg_ref):
    x = x_ref[...]
    ms = jnp.mean(x * x, axis=-1, keepdims=True)
    h = (x * lax.rsqrt(ms + EPS) * g1_ref[...]).astype(BF16)
    bd = bd_ref[...]

    def seg(j):
        return jnp.dot(h, w_ref[:, j * SEG:(j + 1) * SEG], preferred_element_type=F32)

    def head_norm(a, g, scale):
        outs = []
        for c in range(SEG // MXU_TILE):
            ac = a[:, c * MXU_TILE:(c + 1) * MXU_TILE]
            ss = jnp.dot((ac * ac).astype(BF16), bd, preferred_element_type=F32)
            outs.append(ac * lax.rsqrt(ss * (1.0 / SB_HEAD_DIM) + EPS))
        y = jnp.concatenate(outs, axis=1) * g
        return y * scale if scale != 1.0 else y

    qkv_ref[:, 0:SEG] = head_norm(seg(0), gq_ref[...], LOG2E * SB_HEAD_DIM ** -0.5).astype(BF16)
    qkv_ref[:, SEG:2 * SEG] = head_norm(seg(1), gk_ref[...], 1.0).astype(BF16)
    qkv_ref[:, 2 * SEG:3 * SEG] = seg(2).astype(BF16)
    hq_ref[...] = seg(3).astype(BF16)
    f = seg(4)
    lb = lb_ref[...]
    hg_ref[...] = jnp.log(lb + (1.0 - lb) * _sigmoid(f))
    hk_ref[...] = (1.0 - lb) * _sigmoid(-f)
    hi_ref[...] = seg(5).astype(BF16)
    gate = seg(6)
    sg_ref[...] = (gate * _sigmoid(gate)).astype(BF16)


def _inproj(x2, g1, w_bf, gq, gk, lb, bd, *, tm):
    t, d = x2.shape
    n = w_bf.shape[1]
    row = lambda i: (i, 0)
    fix = lambda i: (0, 0)
    return pl.pallas_call(
        _inproj_kernel,
        grid=(t // tm,),
        in_specs=[
            pl.BlockSpec((tm, d), row),
            pl.BlockSpec((1, d), fix),
            pl.BlockSpec((d, n), fix),
            pl.BlockSpec((1, SEG), fix),
            pl.BlockSpec((1, SEG), fix),
            pl.BlockSpec((1, SEG), fix),
            pl.BlockSpec((MXU_TILE, MXU_TILE), fix),
        ],
        out_specs=[
            pl.BlockSpec((tm, 3 * SEG), row),
            pl.BlockSpec((tm, SEG), row),
            pl.BlockSpec((tm, SEG), row),
            pl.BlockSpec((tm, SEG), row),
            pl.BlockSpec((tm, SEG), row),
            pl.BlockSpec((tm, SEG), row),
        ],
        out_shape=[
            jax.ShapeDtypeStruct((t, 3 * SEG), BF16),
            jax.ShapeDtypeStruct((t, SEG), BF16),
            jax.ShapeDtypeStruct((t, SEG), F32),
            jax.ShapeDtypeStruct((t, SEG), F32),
            jax.ShapeDtypeStruct((t, SEG), BF16),
            jax.ShapeDtypeStruct((t, SEG), BF16),
        ],
        compiler_params=_cparams("parallel"),
        name="inproj",
    )(x2, g1, w_bf, gq, gk, lb, bd)


def _sbattn_kernel(q_ref, k_ref, v_ref, go_ref, u2_ref, bd_ref, o_ref, zero_ref, vm_ref, *scr, tq):
    acc_ref = scr[0:NPAIR]
    carry_ref = scr[NPAIR:2 * NPAIR]
    z_ref = scr[2 * NPAIR:3 * NPAIR]
    w_ref = scr[3 * NPAIR:4 * NPAIR]
    qi = pl.program_id(1)
    hps = range(NPAIR)
    lane = lax.broadcasted_iota(I32, (tq, LANES), 1)
    lo = lane < SB_HEAD_DIM

    @pl.when(qi == 0)
    def _():
        v = v_ref[...]
        lo_all = (lax.broadcasted_iota(I32, v.shape, 1) & (LANES - 1)) < SB_HEAD_DIM
        zv = jnp.zeros_like(v)
        vm_ref[0] = jnp.where(lo_all, v, zv)
        vm_ref[1] = jnp.where(lo_all, zv, v)

    u2 = u2_ref[...]
    for hp in hps:
        acc_ref[hp][...] = jnp.zeros_like(acc_ref[hp])
        carry_ref[hp][...] = jnp.zeros_like(carry_ref[hp])
    row = lax.broadcasted_iota(I32, (2 * tq, tq), 0)
    col = lax.broadcasted_iota(I32, (2 * tq, tq), 1)
    strict = col < jnp.where(row >= tq, row - tq, row)
    q2s = []
    for hp in hps:
        q = q_ref[:, hp * LANES:(hp + 1) * LANES]
        zq = jnp.zeros_like(q)
        q2s.append(jnp.concatenate([jnp.where(lo, q, zq), jnp.where(lo, zq, q)], axis=0))

    def scores(kb):
        off = pl.multiple_of(kb * tq, tq)
        return [lax.dot_general(q2s[hp], k_ref[pl.ds(off, tq), hp * LANES:(hp + 1) * LANES], NT_DIMS,
                                preferred_element_type=F32) for hp in hps]

    def weighted_values(kb):
        off = pl.multiple_of(kb * tq, tq)
        for hp in hps:
            vcat = jnp.concatenate([vm_ref[0, pl.ds(off, tq), hp * LANES:(hp + 1) * LANES],
                                    vm_ref[1, pl.ds(off, tq), hp * LANES:(hp + 1) * LANES]], axis=0)
            acc_ref[hp][...] += jnp.dot(w_ref[hp][...], vcat, preferred_element_type=F32)

    def weights(zs, masked):
        lgs = []
        for z in zs:
            nz = -z
            lg = jnp.minimum(nz, 0.0) - jnp.log2(1.0 + jnp.exp2(jnp.minimum(z, nz)))
            if masked:
                lg = jnp.where(strict, lg, 0.0)
            lgs.append(lg)
        withins = []
        for pr in range(NPAIR // 2):
            cat = jnp.concatenate([lgs[2 * pr], lgs[2 * pr + 1]], axis=1).astype(BF16)
            res = jnp.dot(cat, u2, preferred_element_type=F32)
            withins += [res[:, :tq], res[:, tq:]]
        reps = tq // LANES
        carries = [jnp.concatenate([carry_ref[hp][...]] * reps, axis=1) for hp in hps]
        ws = [jnp.exp2((zs[hp] + lgs[hp]) + (carries[hp] + withins[hp])) for hp in hps]
        if masked:
            ws = [jnp.where(strict, w, 0.0) for w in ws]
        for hp in hps:
            carry_ref[hp][...] += jnp.sum(lgs[hp], axis=-1, keepdims=True)
        return [jnp.concatenate([w[:tq], w[tq:]], axis=1).astype(BF16) for w in ws]

    ws = weights(scores(qi), True)
    zn = scores(jnp.maximum(qi - 1, 0))
    for hp in hps:
        w_ref[hp][...] = ws[hp]
        z_ref[hp][...] = zn[hp]

    def body(it, c):
        kb = qi - 1 - it
        zs = [z_ref[hp][...] for hp in hps]
        zn = scores(jnp.maximum(kb - 1, 0))
        weighted_values(kb + 1)
        ws = weights(zs, False)
        for hp in hps:
            w_ref[hp][...] = ws[hp]
            z_ref[hp][...] = zn[hp]
        return c

    lax.fori_loop(0, qi, body, 0)
    weighted_values(0)
    for hp in hps:
        a = acc_ref[hp][...]
        ss = jnp.dot((a * a).astype(BF16), bd_ref[...], preferred_element_type=F32)
        o_ref[:, hp * LANES:(hp + 1) * LANES] = (
            a * lax.rsqrt(ss * (1.0 / SB_HEAD_DIM) + EPS) * go_ref[:, hp * LANES:(hp + 1) * LANES]).astype(BF16)
    zero_ref[...] = jnp.zeros_like(zero_ref)


def _sbattn(qkv, go, u2, bd, *, tq, zero_rows):
    b, s, _ = qkv.shape
    nq = s // tq
    zrows = zero_rows // (b * nq)
    assert zrows * b * nq == zero_rows and zrows % 8 == 0
    return pl.pallas_call(
        functools.partial(_sbattn_kernel, tq=tq),
        grid=(b, s // tq),
        in_specs=[
            pl.BlockSpec((None, tq, SB_WIDTH), lambda bi, qi: (bi, qi, 0)),
            pl.BlockSpec((None, s, SB_WIDTH), lambda bi, qi: (bi, 0, 1)),
            pl.BlockSpec((None, s, SB_WIDTH), lambda bi, qi: (bi, 0, 2)),
            pl.BlockSpec((1, SB_WIDTH), lambda bi, qi: (0, 0)),
            pl.BlockSpec((2 * tq, 2 * tq), lambda bi, qi: (0, 0)),
            pl.BlockSpec((LANES, LANES), lambda bi, qi: (0, 0)),
        ],
        out_specs=[pl.BlockSpec((None, tq, SB_WIDTH), lambda bi, qi: (bi, qi, 0)),
                   pl.BlockSpec((zrows, LANES), lambda bi, qi: (bi * nq + qi, 0))],
        out_shape=[jax.ShapeDtypeStruct((b, s, SB_WIDTH), BF16),
                   jax.ShapeDtypeStruct((zero_rows, LANES), U32)],
        scratch_shapes=([pltpu.VMEM((2, s, SB_WIDTH), BF16)]
                        + [pltpu.VMEM((tq, LANES), F32)] * NPAIR
                        + [pltpu.VMEM((2 * tq, LANES), F32)] * NPAIR
                        + [pltpu.VMEM((2 * tq, tq), F32)] * NPAIR
                        + [pltpu.VMEM((tq, 2 * tq), BF16)] * NPAIR),
        compiler_params=_cparams("parallel", "arbitrary"),
        name="sbattn",
    )(qkv, qkv, qkv, go, u2, bd)


def _hgrn_kernel(q_ref, g_ref, k_ref, v_ref, sg_ref, gn_ref, tl_ref, o_ref, *state_ref):
    s = q_ref.shape[0]
    grp = LANES
    nch = grp // HG_CHUNK
    hs = range(HG_WIDTH // HG_HEAD_DIM)
    for h in hs:
        state_ref[h][...] = jnp.zeros_like(state_ref[h])
    tl = tl_ref[...]
    rr = lax.broadcasted_iota(I32, (grp, grp), 0)
    cc = lax.broadcasted_iota(I32, (grp, grp), 1)
    shift = HG_CHUNK.bit_length() - 1
    rchunk = rr >> shift
    causal = (rchunk == (cc >> shift)) & (cc <= rr)
    inchunk = [rchunk == ci for ci in range(nch)]

    def group(r, c):
        off = pl.multiple_of(r * grp, grp)
        col = lambda h: slice(h * HG_HEAD_DIM, (h + 1) * HG_HEAD_DIM)
        rows = pl.ds(off, grp)
        vs = [v_ref[rows, col(h)] for h in hs]
        bs = [sum(jnp.dot(tl, p, preferred_element_type=F32) for p in _bf16_split(g_ref[rows, col(h)], 2))
              for h in hs]
        lasts = [[b[(ci + 1) * HG_CHUNK - 1:(ci + 1) * HG_CHUNK, :] for ci in range(nch)] for b in bs]
        bls = [jnp.concatenate([jnp.broadcast_to(l, (HG_CHUNK, HG_HEAD_DIM)) for l in last], axis=0)
               for last in lasts]
        qes = [(q_ref[rows, col(h)].astype(F32) * jnp.exp(bs[h])).astype(BF16) for h in hs]
        kes = [(k_ref[rows, col(h)] * jnp.exp(-bs[h])).astype(BF16) for h in hs]
        kds = [(k_ref[rows, col(h)] * jnp.exp(bls[h] - bs[h])).astype(BF16) for h in hs]
        attns = [jnp.where(causal, lax.dot_general(qes[h], kes[h], NT_DIMS, preferred_element_type=F32), 0.0)
                 .astype(BF16) for h in hs]
        os = [jnp.dot(attns[h], vs[h], preferred_element_type=F32) for h in hs]
        vts = [vs[h].astype(F32).T.astype(BF16) for h in hs]
        uts = []
        for h in hs:
            zk = jnp.zeros_like(kds[h])
            kdx = jnp.concatenate([jnp.where(inchunk[ci], kds[h], zk) for ci in range(nch)], axis=1)
            uts.append(jnp.dot(vts[h], kdx, preferred_element_type=F32))
        for h in hs:
            st = state_ref[h][...]
            parts = []
            for ci in range(nch):
                parts.append(st.astype(BF16))
                st = st * jnp.exp(lasts[h][ci]) + uts[h][:, ci * HG_HEAD_DIM:(ci + 1) * HG_HEAD_DIM]
            state_ref[h][...] = st
            zq = jnp.zeros_like(qes[h])
            qx = jnp.concatenate([jnp.where(inchunk[ci], qes[h], zq) for ci in range(nch)], axis=1)
            o = os[h] + lax.dot_general(qx, jnp.concatenate(parts, axis=1), NT_DIMS, preferred_element_type=F32)
            ms = jnp.mean(o * o, axis=-1, keepdims=True)
            y = o * lax.rsqrt(ms + EPS) * gn_ref[:, col(h)]
            o_ref[rows, col(h)] = (y * sg_ref[rows, col(h)].astype(F32)).astype(BF16)
        return c

    lax.fori_loop(0, s // grp, group, 0, unroll=4)


def _hgrn(hq, hg, hk, hi, sg, gn, tl):
    b, s, w = hq.shape
    nh = HG_WIDTH // HG_HEAD_DIM
    blk = pl.BlockSpec((None, s, w), lambda bi: (bi, 0, 0))
    return pl.pallas_call(
        _hgrn_kernel,
        grid=(b,),
        in_specs=[blk, blk, blk, blk, blk,
                  pl.BlockSpec((1, w), lambda bi: (0, 0)),
                  pl.BlockSpec((LANES, LANES), lambda bi: (0, 0))],
        out_specs=blk,
        out_shape=jax.ShapeDtypeStruct((b, s, w), BF16),
        scratch_shapes=[pltpu.VMEM((HG_HEAD_DIM, HG_HEAD_DIM), F32)] * nh,
        compiler_params=_cparams("parallel"),
        name="hgrn",
    )(hq, hg, hk, hi, sg, gn, tl)


def _outproj_kernel(x_ref, a_ref, r_ref, wo_ref, g2_ref, wr_ref, br_ref, tri_ref,
                    x1_ref, hp_ref, ti_ref, tg_ref, rk_ref, cnt_ref, run_ref):
    i = pl.program_id(0)

    @pl.when(i == 0)
    def _():
        run_ref[...] = jnp.zeros_like(run_ref)

    half = a_ref.shape[1]
    x1 = (x_ref[...]
          + jnp.dot(a_ref[...], wo_ref[:half, :], preferred_element_type=F32)
          + jnp.dot(r_ref[...], wo_ref[half:, :], preferred_element_type=F32))
    x1_ref[...] = x1
    ms = jnp.mean(x1 * x1, axis=-1, keepdims=True)
    h2 = x1 * lax.rsqrt(ms + EPS) * g2_ref[...]
    _store_interleaved(hp_ref, _pack_bf16_pairs(h2))

    h_hi, h_lo = _bf16_split(h2, 2)
    w_hi, w_lo = _bf16_split(wr_ref[...], 2)
    hi_both = jnp.dot(h_hi, jnp.concatenate([w_hi, w_lo], axis=1), preferred_element_type=F32)
    logits = ((hi_both[:, :LANES] + jnp.dot(h_lo, w_hi, preferred_element_type=F32))
              + hi_both[:, LANES:]) + br_ref[...]
    tm = logits.shape[0]
    lane = lax.broadcasted_iota(I32, (tm, LANES), 1).astype(F32)
    neg = jnp.float32(-jnp.inf)
    ti = jnp.zeros((tm, LANES), F32)
    tv = jnp.zeros((tm, LANES), F32)
    onehot = jnp.zeros((tm, LANES), F32)
    sels = []
    avail = lane < N_EXPERTS
    for k in range(TOP_K):
        m = jnp.max(jnp.where(avail, logits, neg), axis=-1, keepdims=True)
        idx = jnp.min(jnp.where(avail & (logits == m), lane, float(LANES)), axis=-1, keepdims=True)
        sel = lane == idx
        sels.append(sel)
        ti = jnp.where(lane == k, idx, ti)
        tv = jnp.where(lane == k, m, tv)
        onehot = jnp.where(sel, 1.0, onehot)
        avail = avail & jnp.logical_not(sel)
    valid = lane < TOP_K
    e = jnp.where(valid, jnp.exp(tv - jnp.max(jnp.where(valid, tv, neg), axis=-1, keepdims=True)), 0.0)
    tg_ref[...] = e / jnp.sum(e, axis=-1, keepdims=True)
    ti_ref[...] = ti.astype(I32)

    before = jnp.dot(tri_ref[...], onehot.astype(BF16), preferred_element_type=F32) + run_ref[...]
    rk = jnp.zeros((tm, LANES), F32)
    for k in range(TOP_K):
        rk = jnp.where(lane == k, jnp.sum(jnp.where(sels[k], before, 0.0), axis=-1, keepdims=True), rk)
    rk_ref[...] = rk.astype(I32)
    run_ref[...] += jnp.sum(onehot, axis=0, keepdims=True)
    cnt_ref[...] = run_ref[...]


def _outproj(x2, a2, r2, wo_bf, g2, wr, br, tri, *, tm):
    t, d = x2.shape
    half = a2.shape[1]
    row = lambda i: (i, 0)
    fix = lambda i: (0, 0)
    return pl.pallas_call(
        _outproj_kernel,
        grid=(t // tm,),
        in_specs=[
            pl.BlockSpec((tm, d), row),
            pl.BlockSpec((tm, half), row),
            pl.BlockSpec((tm, half), row),
            pl.BlockSpec((2 * half, d), fix),
            pl.BlockSpec((1, d), fix),
            pl.BlockSpec((d, LANES), fix),
            pl.BlockSpec((1, LANES), fix),
            pl.BlockSpec((tm, tm), fix),
        ],
        out_specs=[
            pl.BlockSpec((tm, d), row),
            pl.BlockSpec((ROW_GROUPS * tm, LANES), row),
            pl.BlockSpec((tm, LANES), row),
            pl.BlockSpec((tm, LANES), row),
            pl.BlockSpec((tm, LANES), row),
            pl.BlockSpec((1, LANES), fix),
        ],
        out_shape=[
            jax.ShapeDtypeStruct((t, d), F32),
            jax.ShapeDtypeStruct((ROW_GROUPS * t, LANES), U32),
            jax.ShapeDtypeStruct((t, LANES), I32),
            jax.ShapeDtypeStruct((t, LANES), F32),
            jax.ShapeDtypeStruct((t, LANES), I32),
            jax.ShapeDtypeStruct((1, LANES), F32),
        ],
        scratch_shapes=[pltpu.VMEM((1, LANES), F32)],
        compiler_params=_cparams("arbitrary"),
        name="outproj",
    )(x2, a2, r2, wo_bf, g2, wr, br, tri)


def _slab(ref, r):
    return ref.at[pl.ds(pl.multiple_of(ROW_GROUPS * r, ROW_GROUPS), ROW_GROUPS)]


def _slab_copy_wait(src, dst, sem, n):
    pltpu.make_async_copy(src.at[pl.ds(0, ROW_GROUPS * n)], dst.at[pl.ds(0, ROW_GROUPS * n)], sem).wait()


def _scatter_kernel(dest_ref, hp_ref, xz_ref, xb_ref, sem):
    del xz_ref
    tm = hp_ref.shape[0] // ROW_GROUPS
    for t in range(tm):
        for k in range(TOP_K):
            pltpu.make_async_copy(_slab(hp_ref, t), _slab(xb_ref, dest_ref[0, t * TOP_K + k]),
                                  sem).start(priority=k % 2)
    for _ in range(TOP_K):
        _slab_copy_wait(hp_ref, xb_ref, sem, tm)


def _scatter(dest2, hp, xzero, *, tm):
    t = hp.shape[0] // ROW_GROUPS
    return pl.pallas_call(
        _scatter_kernel,
        grid=(t // tm,),
        in_specs=[
            pl.BlockSpec((None, 1, tm * TOP_K), lambda i: (i, 0, 0), memory_space=pltpu.SMEM),
            pl.BlockSpec((ROW_GROUPS * tm, LANES), lambda i: (i, 0)),
            pl.BlockSpec(memory_space=pl.ANY),
        ],
        out_specs=pl.BlockSpec(memory_space=pl.ANY),
        out_shape=jax.ShapeDtypeStruct(xzero.shape, xzero.dtype),
        scratch_shapes=[pltpu.SemaphoreType.DMA],
        input_output_aliases={2: 0},
        compiler_params=_cparams("arbitrary"),
        name="scatter",
    )(dest2, hp, xzero)


def _experts_kernel(be_ref, nu_ref, xb_ref, wgu_ref, bgu_ref, wd_ref, bd_ref, yb_ref, wgu_bf, wd_bf):
    i = pl.program_id(0)
    fresh = jnp.logical_or(i == 0, be_ref[i] != be_ref[jnp.maximum(i - 1, 0)])

    @pl.when(fresh)
    def _():
        wgu_bf[...] = wgu_ref[...].astype(BF16)
        wd_bf[...] = wd_ref[...].astype(BF16)

    @pl.when(i < nu_ref[0])
    def _():
        x_lo, x_hi = _unpack_bf16_pairs(_load_interleaved(xb_ref))
        x = jnp.concatenate([x_lo.astype(BF16), x_hi.astype(BF16)], axis=1)
        gu = jnp.dot(x, wgu_bf[...], preferred_element_type=F32) + bgu_ref[...]
        f = gu.shape[1] // 2
        g_lin = jnp.minimum(gu[:, :f], SWIGLU_LIMIT)
        u_lin = jnp.clip(gu[:, f:], -SWIGLU_LIMIT, SWIGLU_LIMIT)
        act = g_lin * _sigmoid(SWIGLU_ALPHA * g_lin) * (u_lin + 1.0)
        y = jnp.dot(act.astype(BF16), wd_bf[...], preferred_element_type=F32) + bd_ref[...]
        _store_interleaved(yb_ref, _pack_bf16_pairs(y))

    @pl.when(i >= nu_ref[0])
    def _():
        yb_ref[...] = jnp.zeros_like(yb_ref)


def _experts(block_e, n_used, xb, w_gate_up, b_gate_up3, w_down, b_down3, *, rb):
    n_rows = xb.shape[0] // ROW_GROUPS
    d = w_gate_up.shape[1]
    f2 = w_gate_up.shape[2]
    f = w_down.shape[1]
    nblk = n_rows // rb
    rowblk = lambda i, be, nu: (jnp.minimum(i, nu[0] - 1), 0)
    wsel = lambda i, be, nu: (be[i], 0, 0)
    grid_spec = pltpu.PrefetchScalarGridSpec(
        num_scalar_prefetch=2,
        grid=(nblk,),
        in_specs=[
            pl.BlockSpec((ROW_GROUPS * rb, LANES), rowblk),
            pl.BlockSpec((None, d, f2), wsel),
            pl.BlockSpec((None, 1, f2), wsel),
            pl.BlockSpec((None, f, d), wsel),
            pl.BlockSpec((None, 1, d), wsel),
        ],
        out_specs=pl.BlockSpec((ROW_GROUPS * rb, LANES), lambda i, be, nu: (i, 0)),
        scratch_shapes=[pltpu.VMEM((d, f2), BF16), pltpu.VMEM((f, d), BF16)],
    )
    return pl.pallas_call(
        _experts_kernel,
        grid_spec=grid_spec,
        out_shape=jax.ShapeDtypeStruct(xb.shape, U32),
        compiler_params=_cparams("arbitrary"),
        name="experts",
    )(block_e, n_used, xb, w_gate_up, b_gate_up3, w_down, b_down3)


def _combine_kernel(dest0_ref, destn_ref, x1_ref, tg_ref, yb_ref, o_ref, buf, sem):
    i = pl.program_id(0)
    n = pl.num_programs(0)
    tm = x1_ref.shape[0]
    slot = i % 2

    def gather_one(dest_ref, s, t):
        for k in range(TOP_K):
            pltpu.make_async_copy(_slab(yb_ref, dest_ref[0, t * TOP_K + k]), _slab(buf.at[s, k], t),
                                  sem.at[s]).start(priority=k % 2)

    @pl.when(i == 0)
    def _():
        def body(t, c):
            gather_one(dest0_ref, 0, t)
            return c
        lax.fori_loop(0, tm, body, 0)

    @pl.when(i + 1 < n)
    def _():
        for t in range(tm):
            gather_one(destn_ref, 1 - slot, t)

    for k in range(TOP_K):
        _slab_copy_wait(yb_ref, buf.at[slot, k], sem.at[slot], tm)
    tg = tg_ref[...]
    half = x1_ref.shape[1] // 2
    for j in range(ROW_GROUPS):
        lo_cols = slice(j * LANES, (j + 1) * LANES)
        hi_cols = slice(half + j * LANES, half + (j + 1) * LANES)
        out_lo = x1_ref[:, lo_cols]
        out_hi = x1_ref[:, hi_cols]
        for k in range(TOP_K):
            y_lo, y_hi = _unpack_bf16_pairs(buf[slot, k, pl.ds(j, tm, stride=ROW_GROUPS), :])
            out_lo = out_lo + tg[:, k:k + 1] * y_lo
            out_hi = out_hi + tg[:, k:k + 1] * y_hi
        o_ref[:, lo_cols] = out_lo
        o_ref[:, hi_cols] = out_hi


def _combine(dest2, x1, tg, yb, *, tm):
    t, d = x1.shape
    n = t // tm
    return pl.pallas_call(
        _combine_kernel,
        grid=(n,),
        in_specs=[
            pl.BlockSpec((None, 1, tm * TOP_K), lambda i: (0, 0, 0), memory_space=pltpu.SMEM),
            pl.BlockSpec((None, 1, tm * TOP_K), lambda i: (jnp.minimum(i + 1, n - 1), 0, 0),
                         memory_space=pltpu.SMEM),
            pl.BlockSpec((tm, d), lambda i: (i, 0)),
            pl.BlockSpec((tm, LANES), lambda i: (i, 0)),
            pl.BlockSpec(memory_space=pl.ANY),
        ],
        out_specs=pl.BlockSpec((tm, d), lambda i: (i, 0)),
        out_shape=jax.ShapeDtypeStruct((t, d), F32),
        scratch_shapes=[pltpu.VMEM((2, TOP_K, ROW_GROUPS * tm, LANES), yb.dtype), pltpu.SemaphoreType.DMA((2,))],
        compiler_params=_cparams("arbitrary"),
        name="combine",
    )(dest2, dest2, x1, tg, yb)


def _block_ones(n, blk):
    i = jnp.arange(n)
    return (i[:, None] // blk == i[None, :] // blk).astype(BF16)


def _pair_suffix(tq):
    j = jnp.arange(2 * tq)
    return ((j[:, None] > j[None, :]) & (j[:, None] // tq == j[None, :] // tq)).astype(BF16)


def _chunk_tril(n, blk):
    i = jnp.arange(n)
    return ((i[:, None] // blk == i[None, :] // blk) & (i[None, :] <= i[:, None])).astype(BF16)


def _strict_tril(n):
    i = jnp.arange(n)
    return (i[None, :] < i[:, None]).astype(BF16)


def _layer(x, norm1_g, w_in, q_g, k_g, sbo_g, lb, hgo_g, w_out, norm2_g, w_router, b_router,
           w_gate_up, b_gate_up, w_down, b_down):
    bsz, seq, d = x.shape
    rb = RB_EXPERTS
    t = bsz * seq
    x2 = x.reshape(t, d)
    nsb = SB_WIDTH // SB_HEAD_DIM

    qkv, hq, hg, hk, hi, sg = _inproj(
        x2, norm1_g.reshape(1, d), w_in.astype(BF16),
        jnp.tile(q_g, nsb).reshape(1, SB_WIDTH), jnp.tile(k_g, nsb).reshape(1, SB_WIDTH),
        lb.reshape(1, HG_WIDTH), _block_ones(MXU_TILE, SB_HEAD_DIM), tm=TM_INPROJ)

    n_assign = t * TOP_K
    nblk = n_assign // rb + N_EXPERTS
    a, xzero = _sbattn(qkv.reshape(bsz, seq, 3 * SB_WIDTH), sbo_g.reshape(1, SB_WIDTH),
                       _pair_suffix(TQ_ATTN), _block_ones(LANES, SB_HEAD_DIM), tq=TQ_ATTN,
                       zero_rows=ROW_GROUPS * nblk * rb)
    sh = (bsz, seq, HG_WIDTH)
    r = _hgrn(hq.reshape(sh), hg.reshape(sh), hk.reshape(sh), hi.reshape(sh), sg.reshape(sh),
              hgo_g.reshape(1, HG_WIDTH), _chunk_tril(LANES, HG_CHUNK))

    wr = jnp.zeros((d, LANES), F32).at[:, :N_EXPERTS].set(w_router)
    br = jnp.zeros((1, LANES), F32).at[0, :N_EXPERTS].set(b_router)
    x1, hp, ti, tg, rk, cnt = _outproj(
        x2, a.reshape(t, SB_WIDTH), r.reshape(t, HG_WIDTH), w_out.astype(BF16),
        norm2_g.reshape(1, d), wr, br, _strict_tril(TM_OUTPROJ), tm=TM_OUTPROJ)

    counts = cnt[0, :N_EXPERTS].astype(I32)
    padded = (counts + rb - 1) // rb * rb
    pad_ends = jnp.cumsum(padded)
    pad_starts = pad_ends - padded
    eids = jnp.arange(N_EXPERTS, dtype=I32)
    dest = rk[:, :TOP_K] + jnp.sum(jnp.where(ti[:, :TOP_K, None] == eids, pad_starts, 0), axis=-1)
    n_used = (pad_ends[-1] // rb).astype(I32).reshape(1)
    blk_start = jnp.minimum(jnp.arange(nblk, dtype=I32), n_used[0] - 1) * rb
    block_e = jnp.sum((blk_start[:, None] >= pad_ends[None, :]).astype(I32), axis=1)

    xb = _scatter(dest.reshape(t // TM_SCATTER, 1, TM_SCATTER * TOP_K), hp, xzero, tm=TM_SCATTER)
    yb = _experts(block_e, n_used, xb, w_gate_up, b_gate_up.reshape(N_EXPERTS, 1, -1),
                  w_down, b_down.reshape(N_EXPERTS, 1, -1), rb=rb)
    out = _combine(dest.reshape(t // TM_COMBINE, 1, TM_COMBINE * TOP_K), x1, tg, yb, tm=TM_COMBINE)
    return out.reshape(bsz, seq, d)


def kernel(x, norm1_g, w_in, sb_q_norm_g, sb_k_norm_g, sb_out_norm_g, hg_lb_logits, hg_out_norm_g,
           w_out, norm2_g, w_router, b_router, w_gate_up, b_gate_up, w_down, b_down):
    depth = w_in.shape[0]
    lb_all = jnp.cumsum(jax.nn.softmax(hg_lb_logits.astype(F32), axis=0), axis=0)
    for l in range(depth):
        x = _layer(x, norm1_g[l], w_in[l], sb_q_norm_g[l], sb_k_norm_g[l], sb_out_norm_g[l], lb_all[l],
                   hg_out_norm_g[l], w_out[l], norm2_g[l], w_router[l], b_router[l],
                   w_gate_up[l], b_gate_up[l], w_down[l], b_down[l])
    return x
```

```python
import functools

import jax
import jax.numpy as jnp
from jax import lax
from jax.experimental import pallas as pl
from jax.experimental.pallas import tpu as pltpu

F32 = jnp.float32
BF16 = jnp.bfloat16
I32 = jnp.int32
U32 = jnp.uint32

EPS = 1e-6
LANES = 128
SB_HEAD_DIM = 64
SB_WIDTH = 512
HG_HEAD_DIM = 128
HG_WIDTH = 512
HG_CHUNK = 32
HG_GROUPS = 4
N_EXPERTS = 32
TOP_K = 4
SWIGLU_LIMIT = 7.0
SWIGLU_ALPHA = 1.702
VMEM_LIMIT = 56 * 1024 * 1024

TM_INPROJ = 512
TQ_ATTN = 256
TM_OUTPROJ = 512
TM_SCATTER = 2048
TM_COMBINE = 256
RB_EXPERTS = 512

SEG = SB_WIDTH
MXU_TILE = 256

NT_DIMS = (((1,), (1,)), ((), ()))
NPAIR = SB_WIDTH // LANES
LOG2E = 1.4426950408889634


def _cparams(*sem):
    return pltpu.CompilerParams(dimension_semantics=sem, vmem_limit_bytes=VMEM_LIMIT)


def _bf16_split(a, n):
    parts = []
    for _ in range(n - 1):
        p = a.astype(BF16)
        parts.append(p)
        a = a - p.astype(F32)
    parts.append(a.astype(BF16))
    return parts


def _sigmoid(a):
    return 1.0 / (1.0 + jnp.exp(-a))


def _pack_bf16_pairs(a):
    w = a.shape[1] // 2
    lo_bits = pltpu.bitcast(a[:, :w].astype(BF16).astype(F32), U32)
    hi_bits = pltpu.bitcast(a[:, w:].astype(BF16).astype(F32), U32)
    return (hi_bits & jnp.uint32(0xFFFF0000)) | (lo_bits >> 16)


def _unpack_bf16_pairs(p):
    return pltpu.bitcast(p << 16, F32), pltpu.bitcast(p & jnp.uint32(0xFFFF0000), F32)


ROW_GROUPS = 4


def _store_interleaved(ref, a):
    rows = a.shape[0]
    for j in range(ROW_GROUPS):
        ref[pl.ds(j, rows, stride=ROW_GROUPS), :] = a[:, j * LANES:(j + 1) * LANES]


def _load_interleaved(ref):
    rows = ref.shape[0] // ROW_GROUPS
    return jnp.concatenate([ref[pl.ds(j, rows, stride=ROW_GROUPS), :] for j in range(ROW_GROUPS)], axis=1)


def _inproj_kernel(x_ref, g1_ref, w_ref, gq_ref, gk_ref, lb_ref, bd_ref,
                   qkv_ref, hq_ref, hg_ref, hk_ref, hi_ref, sg_ref):
    x = x_ref[...]
    ms = jnp.mean(x * x, axis=-1, keepdims=True)
    h = (x * lax.rsqrt(ms + EPS) * g1_ref[...]).astype(BF16)
    bd = bd_ref[...]

    def seg(j):
        return jnp.dot(h, w_ref[:, j * SEG:(j + 1) * SEG], preferred_element_type=F32)

    def head_norm(a, g, scale):
        outs = []
        for c in range(SEG // MXU_TILE):
            ac = a[:, c * MXU_TILE:(c + 1) * MXU_TILE]
            ss = jnp.dot((ac * ac).astype(BF16), bd, preferred_element_type=F32)
            outs.append(ac * lax.rsqrt(ss * (1.0 / SB_HEAD_DIM) + EPS))
        y = jnp.concatenate(outs, axis=1) * g
        return y * scale if scale != 1.0 else y

    qkv_ref[:, 0:SEG] = head_norm(seg(0), gq_ref[...], LOG2E * SB_HEAD_DIM ** -0.5).astype(BF16)
    qkv_ref[:, SEG:2 * SEG] = head_norm(seg(1), gk_ref[...], 1.0).astype(BF16)
    qkv_ref[:, 2 * SEG:3 * SEG] = seg(2).astype(BF16)
    hq_ref[...] = seg(3).astype(BF16)
    f = seg(4)
    lb = lb_ref[...]
    hg_ref[...] = jnp.log(lb + (1.0 - lb) * _sigmoid(f))
    hk_ref[...] = (1.0 - lb) * _sigmoid(-f)
    hi_ref[...] = seg(5).astype(BF16)
    gate = seg(6)
    sg_ref[...] = (gate * _sigmoid(gate)).astype(BF16)


def _inproj(x2, g1, w_bf, gq, gk, lb, bd, *, tm):
    t, d = x2.shape
    n = w_bf.shape[1]
    row = lambda i: (i, 0)
    fix = lambda i: (0, 0)
    return pl.pallas_call(
        _inproj_kernel,
        grid=(t // tm,),
        in_specs=[
            pl.BlockSpec((tm, d), row),
            pl.BlockSpec((1, d), fix),
            pl.BlockSpec((d, n), fix),
            pl.BlockSpec((1, SEG), fix),
            pl.BlockSpec((1, SEG), fix),
            pl.BlockSpec((1, SEG), fix),
            pl.BlockSpec((MXU_TILE, MXU_TILE), fix),
        ],
        out_specs=[
            pl.BlockSpec((tm, 3 * SEG), row),
            pl.BlockSpec((tm, SEG), row),
            pl.BlockSpec((tm, SEG), row),
            pl.BlockSpec((tm, SEG), row),
            pl.BlockSpec((tm, SEG), row),
            pl.BlockSpec((tm, SEG), row),
        ],
        out_shape=[
            jax.ShapeDtypeStruct((t, 3 * SEG), BF16),
            jax.ShapeDtypeStruct((t, SEG), BF16),
            jax.ShapeDtypeStruct((t, SEG), F32),
            jax.ShapeDtypeStruct((t, SEG), F32),
            jax.ShapeDtypeStruct((t, SEG), BF16),
            jax.ShapeDtypeStruct((t, SEG), BF16),
        ],
        compiler_params=_cparams("parallel"),
        name="inproj",
    )(x2, g1, w_bf, gq, gk, lb, bd)


def _sbattn_kernel(q_ref, k_ref, v_ref, go_ref, u2_ref, bd_ref, o_ref, zero_ref, vm_ref, *scr, tq):
    acc_ref = scr[0:NPAIR]
    carry_ref = scr[NPAIR:2 * NPAIR]
    z_ref = scr[2 * NPAIR:3 * NPAIR]
    w_ref = scr[3 * NPAIR:4 * NPAIR]
    qi = pl.program_id(1)
    hps = range(NPAIR)
    lane = lax.broadcasted_iota(I32, (tq, LANES), 1)
    lo = lane < SB_HEAD_DIM

    @pl.when(qi == 0)
    def _():
        v = v_ref[...]
        lo_all = (lax.broadcasted_iota(I32, v.shape, 1) & (LANES - 1)) < SB_HEAD_DIM
        zv = jnp.zeros_like(v)
        vm_ref[0] = jnp.where(lo_all, v, zv)
        vm_ref[1] = jnp.where(lo_all, zv, v)

    u2 = u2_ref[...]
    for hp in hps:
        acc_ref[hp][...] = jnp.zeros_like(acc_ref[hp])
        carry_ref[hp][...] = jnp.zeros_like(carry_ref[hp])
    row = lax.broadcasted_iota(I32, (2 * tq, tq), 0)
    col = lax.broadcasted_iota(I32, (2 * tq, tq), 1)
    strict = col < jnp.where(row >= tq, row - tq, row)
    q2s = []
    for hp in hps:
        q = q_ref[:, hp * LANES:(hp + 1) * LANES]
        zq = jnp.zeros_like(q)
        q2s.append(jnp.concatenate([jnp.where(lo, q, zq), jnp.where(lo, zq, q)], axis=0))

    def scores(kb):
        off = pl.multiple_of(kb * tq, tq)
        return [lax.dot_general(q2s[hp], k_ref[pl.ds(off, tq), hp * LANES:(hp + 1) * LANES], NT_DIMS,
                                preferred_element_type=F32) for hp in hps]

    def weighted_values(kb):
        off = pl.multiple_of(kb * tq, tq)
        for hp in hps:
            vcat = jnp.concatenate([vm_ref[0, pl.ds(off, tq), hp * LANES:(hp + 1) * LANES],
                                    vm_ref[1, pl.ds(off, tq), hp * LANES:(hp + 1) * LANES]], axis=0)
            acc_ref[hp][...] += jnp.dot(w_ref[hp][...], vcat, preferred_element_type=F32)

    def weights(zs, masked):
        lgs = []
        for z in zs:
            nz = -z
            lg = jnp.minimum(nz, 0.0) - jnp.log2(1.0 + jnp.exp2(jnp.minimum(z, nz)))
            if masked:
                lg = jnp.where(strict, lg, 0.0)
            lgs.append(lg)
        withins = []
        for pr in range(NPAIR // 2):
            cat = jnp.concatenate([lgs[2 * pr], lgs[2 * pr + 1]], axis=1).astype(BF16)
            res = jnp.dot(cat, u2, preferred_element_type=F32)
            withins += [res[:, :tq], res[:, tq:]]
        reps = tq // LANES
        carries = [jnp.concatenate([carry_ref[hp][...]] * reps, axis=1) for hp in hps]
        ws = [jnp.exp2((zs[hp] + lgs[hp]) + (carries[hp] + withins[hp])) for hp in hps]
        if masked:
            ws = [jnp.where(strict, w, 0.0) for w in ws]
        for hp in hps:
            carry_ref[hp][...] += jnp.sum(lgs[hp], axis=-1, keepdims=True)
        return [jnp.concatenate([w[:tq], w[tq:]], axis=1).astype(BF16) for w in ws]

    ws = weights(scores(qi), True)
    zn = scores(jnp.maximum(qi - 1, 0))
    for hp in hps:
        w_ref[hp][...] = ws[hp]
        z_ref[hp][...] = zn[hp]

    def body(it, c):
        kb = qi - 1 - it
        zs = [z_ref[hp][...] for hp in hps]
        zn = scores(jnp.maximum(kb - 1, 0))
        weighted_values(kb + 1)
        ws = weights(zs, False)
        for hp in hps:
            w_ref[hp][...] = ws[hp]
            z_ref[hp][...] = zn[hp]
        return c

    lax.fori_loop(0, qi, body, 0)
    weighted_values(0)
    for hp in hps:
        a = acc_ref[hp][...]
        ss = jnp.dot((a * a).astype(BF16), bd_ref[...], preferred_element_type=F32)
        o_ref[:, hp * LANES:(hp + 1) * LANES] = (
            a * lax.rsqrt(ss * (1.0 / SB_HEAD_DIM) + EPS) * go_ref[:, hp * LANES:(hp + 1) * LANES]).astype(BF16)
    zero_ref[...] = jnp.zeros_like(zero_ref)


def _sbattn(qkv, go, u2, bd, *, tq, zero_rows):
    b, s, _ = qkv.shape
    nq = s // tq
    zrows = zero_rows // (b * nq)
    assert zrows * b * nq == zero_rows and zrows % 8 == 0
    return pl.pallas_call(
        functools.partial(_sbattn_kernel, tq=tq),
        grid=(b, s // tq),
        in_specs=[
            pl.BlockSpec((None, tq, SB_WIDTH), lambda bi, qi: (bi, qi, 0)),
            pl.BlockSpec((None, s, SB_WIDTH), lambda bi, qi: (bi, 0, 1)),
            pl.BlockSpec((None, s, SB_WIDTH), lambda bi, qi: (bi, 0, 2)),
            pl.BlockSpec((1, SB_WIDTH), lambda bi, qi: (0, 0)),
            pl.BlockSpec((2 * tq, 2 * tq), lambda bi, qi: (0, 0)),
            pl.BlockSpec((LANES, LANES), lambda bi, qi: (0, 0)),
        ],
        out_specs=[pl.BlockSpec((None, tq, SB_WIDTH), lambda bi, qi: (bi, qi, 0)),
                   pl.BlockSpec((zrows, LANES), lambda bi, qi: (bi * nq + qi, 0))],
        out_shape=[jax.ShapeDtypeStruct((b, s, SB_WIDTH), BF16),
                   jax.ShapeDtypeStruct((zero_rows, LANES), U32)],
        scratch_shapes=([pltpu.VMEM((2, s, SB_WIDTH), BF16)]
                        + [pltpu.VMEM((tq, LANES), F32)] * NPAIR
                        + [pltpu.VMEM((2 * tq, LANES), F32)] * NPAIR
                        + [pltpu.VMEM((2 * tq, tq), F32)] * NPAIR
                        + [pltpu.VMEM((tq, 2 * tq), BF16)] * NPAIR),
        compiler_params=_cparams("parallel", "arbitrary"),
        name="sbattn",
    )(qkv, qkv, qkv, go, u2, bd)


def _hgrn_kernel(q_ref, g_ref, k_ref, v_ref, sg_ref, gn_ref, tl_ref, o_ref, *state_ref):
    s = q_ref.shape[0]
    grp = LANES
    nch = grp // HG_CHUNK
    hs = range(HG_WIDTH // HG_HEAD_DIM)
    for h in hs:
        state_ref[h][...] = jnp.zeros_like(state_ref[h])
    tl = tl_ref[...]
    rr = lax.broadcasted_iota(I32, (grp, grp), 0)
    cc = lax.broadcasted_iota(I32, (grp, grp), 1)
    shift = HG_CHUNK.bit_length() - 1
    rchunk = rr >> shift
    causal = (rchunk == (cc >> shift)) & (cc <= rr)
    inchunk = [rchunk == ci for ci in range(nch)]

    def group(r, c):
        col = lambda h: slice(h * HG_HEAD_DIM, (h + 1) * HG_HEAD_DIM)
        gs = range(HG_GROUPS)
        rows = [pl.ds(pl.multiple_of((r * HG_GROUPS + g) * grp, grp), grp) for g in gs]
        ch = [(g, h) for g in gs for h in hs]
        vs = {c_: v_ref[rows[c_[0]], col(c_[1])] for c_ in ch}
        bs = {c_: sum(jnp.dot(tl, p, preferred_element_type=F32)
                      for p in _bf16_split(g_ref[rows[c_[0]], col(c_[1])], 2))
              for c_ in ch}
        lasts = {c_: [bs[c_][(ci + 1) * HG_CHUNK - 1:(ci + 1) * HG_CHUNK, :] for ci in range(nch)] for c_ in ch}
        bls = {c_: jnp.concatenate([jnp.broadcast_to(l, (HG_CHUNK, HG_HEAD_DIM)) for l in lasts[c_]], axis=0)
               for c_ in ch}
        qes = {c_: (q_ref[rows[c_[0]], col(c_[1])].astype(F32) * jnp.exp(bs[c_])).astype(BF16) for c_ in ch}
        kes = {c_: (k_ref[rows[c_[0]], col(c_[1])] * jnp.exp(-bs[c_])).astype(BF16) for c_ in ch}
        kds = {c_: (k_ref[rows[c_[0]], col(c_[1])] * jnp.exp(bls[c_] - bs[c_])).astype(BF16) for c_ in ch}
        attns = {c_: jnp.where(causal, lax.dot_general(qes[c_], kes[c_], NT_DIMS, preferred_element_type=F32), 0.0)
                 .astype(BF16) for c_ in ch}
        os = {c_: jnp.dot(attns[c_], vs[c_], preferred_element_type=F32) for c_ in ch}
        vts = {c_: vs[c_].astype(F32).T.astype(BF16) for c_ in ch}
        uts = {}
        for c_ in ch:
            zk = jnp.zeros_like(kds[c_])
            kdx = jnp.concatenate([jnp.where(inchunk[ci], kds[c_], zk) for ci in range(nch)], axis=1)
            uts[c_] = jnp.dot(vts[c_], kdx, preferred_element_type=F32)
        for h in hs:
            st = state_ref[h][...]
            for g in gs:
                c_ = (g, h)
                parts = []
                for ci in range(nch):
                    parts.append(st.astype(BF16))
                    st = st * jnp.exp(lasts[c_][ci]) + uts[c_][:, ci * HG_HEAD_DIM:(ci + 1) * HG_HEAD_DIM]
                zq = jnp.zeros_like(qes[c_])
                qx = jnp.concatenate([jnp.where(inchunk[ci], qes[c_], zq) for ci in range(nch)], axis=1)
                o = os[c_] + lax.dot_general(qx, jnp.concatenate(parts, axis=1), NT_DIMS,
                                             preferred_element_type=F32)
                ms = jnp.mean(o * o, axis=-1, keepdims=True)
                y = o * lax.rsqrt(ms + EPS) * gn_ref[:, col(h)]
                o_ref[rows[g], col(h)] = (y * sg_ref[rows[g], col(h)].astype(F32)).astype(BF16)
            state_ref[h][...] = st
        return c

    lax.fori_loop(0, s // (grp * HG_GROUPS), group, 0)


def _hgrn(hq, hg, hk, hi, sg, gn, tl):
    b, s, w = hq.shape
    nh = HG_WIDTH // HG_HEAD_DIM
    blk = pl.BlockSpec((None, s, w), lambda bi: (bi, 0, 0))
    return pl.pallas_call(
        _hgrn_kernel,
        grid=(b,),
        in_specs=[blk, blk, blk, blk, blk,
                  pl.BlockSpec((1, w), lambda bi: (0, 0)),
                  pl.BlockSpec((LANES, LANES), lambda bi: (0, 0))],
        out_specs=blk,
        out_shape=jax.ShapeDtypeStruct((b, s, w), BF16),
        scratch_shapes=[pltpu.VMEM((HG_HEAD_DIM, HG_HEAD_DIM), F32)] * nh,
        compiler_params=_cparams("parallel"),
        name="hgrn",
    )(hq, hg, hk, hi, sg, gn, tl)


def _outproj_kernel(x_ref, a_ref, r_ref, wo_ref, g2_ref, wr_ref, br_ref, tri_ref,
                    x1_ref, hp_ref, ti_ref, tg_ref, rk_ref, cnt_ref, run_ref):
    i = pl.program_id(0)

    @pl.when(i == 0)
    def _():
        run_ref[...] = jnp.zeros_like(run_ref)

    half = a_ref.shape[1]
    x1 = (x_ref[...]
          + jnp.dot(a_ref[...], wo_ref[:half, :], preferred_element_type=F32)
          + jnp.dot(r_ref[...], wo_ref[half:, :], preferred_element_type=F32))
    x1_ref[...] = x1
    ms = jnp.mean(x1 * x1, axis=-1, keepdims=True)
    h2 = x1 * lax.rsqrt(ms + EPS) * g2_ref[...]
    _store_interleaved(hp_ref, _pack_bf16_pairs(h2))

    h_hi, h_lo = _bf16_split(h2, 2)
    w_hi, w_lo = _bf16_split(wr_ref[...], 2)
    hi_both = jnp.dot(h_hi, jnp.concatenate([w_hi, w_lo], axis=1), preferred_element_type=F32)
    logits = ((hi_both[:, :LANES] + jnp.dot(h_lo, w_hi, preferred_element_type=F32))
              + hi_both[:, LANES:]) + br_ref[...]
    tm = logits.shape[0]
    lane = lax.broadcasted_iota(I32, (tm, LANES), 1).astype(F32)
    neg = jnp.float32(-jnp.inf)
    ti = jnp.zeros((tm, LANES), F32)
    tv = jnp.zeros((tm, LANES), F32)
    onehot = jnp.zeros((tm, LANES), F32)
    sels = []
    avail = lane < N_EXPERTS
    for k in range(TOP_K):
        m = jnp.max(jnp.where(avail, logits, neg), axis=-1, keepdims=True)
        idx = jnp.min(jnp.where(avail & (logits == m), lane, float(LANES)), axis=-1, keepdims=True)
        sel = lane == idx
        sels.append(sel)
        ti = jnp.where(lane == k, idx, ti)
        tv = jnp.where(lane == k, m, tv)
        onehot = jnp.where(sel, 1.0, onehot)
        avail = avail & jnp.logical_not(sel)
    valid = lane < TOP_K
    e = jnp.where(valid, jnp.exp(tv - jnp.max(jnp.where(valid, tv, neg), axis=-1, keepdims=True)), 0.0)
    tg_ref[...] = e / jnp.sum(e, axis=-1, keepdims=True)
    ti_ref[...] = ti.astype(I32)

    before = jnp.dot(tri_ref[...], onehot.astype(BF16), preferred_element_type=F32) + run_ref[...]
    rk = jnp.zeros((tm, LANES), F32)
    for k in range(TOP_K):
        rk = jnp.where(lane == k, jnp.sum(jnp.where(sels[k], before, 0.0), axis=-1, keepdims=True), rk)
    rk_ref[...] = rk.astype(I32)
    run_ref[...] += jnp.sum(onehot, axis=0, keepdims=True)
    cnt_ref[...] = run_ref[...]


def _outproj(x2, a2, r2, wo_bf, g2, wr, br, tri, *, tm):
    t, d = x2.shape
    half = a2.shape[1]
    row = lambda i: (i, 0)
    fix = lambda i: (0, 0)
    return pl.pallas_call(
        _outproj_kernel,
        grid=(t // tm,),
        in_specs=[
            pl.BlockSpec((tm, d), row),
            pl.BlockSpec((tm, half), row),
            pl.BlockSpec((tm, half), row),
            pl.BlockSpec((2 * half, d), fix),
            pl.BlockSpec((1, d), fix),
            pl.BlockSpec((d, LANES), fix),
            pl.BlockSpec((1, LANES), fix),
            pl.BlockSpec((tm, tm), fix),
        ],
        out_specs=[
            pl.BlockSpec((tm, d), row),
            pl.BlockSpec((ROW_GROUPS * tm, LANES), row),
            pl.BlockSpec((tm, LANES), row),
            pl.BlockSpec((tm, LANES), row),
            pl.BlockSpec((tm, LANES), row),
            pl.BlockSpec((1, LANES), fix),
        ],
        out_shape=[
            jax.ShapeDtypeStruct((t, d), F32),
            jax.ShapeDtypeStruct((ROW_GROUPS * t, LANES), U32),
            jax.ShapeDtypeStruct((t, LANES), I32),
            jax.ShapeDtypeStruct((t, LANES), F32),
            jax.ShapeDtypeStruct((t, LANES), I32),
            jax.ShapeDtypeStruct((1, LANES), F32),
        ],
        scratch_shapes=[pltpu.VMEM((1, LANES), F32)],
        compiler_params=_cparams("arbitrary"),
        name="outproj",
    )(x2, a2, r2, wo_bf, g2, wr, br, tri)


def _slab(ref, r):
    return ref.at[pl.ds(pl.multiple_of(ROW_GROUPS * r, ROW_GROUPS), ROW_GROUPS)]


def _slab_copy_wait(src, dst, sem, n):
    pltpu.make_async_copy(src.at[pl.ds(0, ROW_GROUPS * n)], dst.at[pl.ds(0, ROW_GROUPS * n)], sem).wait()


def _scatter_kernel(dest_ref, hp_ref, xz_ref, xb_ref, sem):
    del xz_ref
    tm = hp_ref.shape[0] // ROW_GROUPS
    for t in range(tm):
        for k in range(TOP_K):
            pltpu.make_async_copy(_slab(hp_ref, t), _slab(xb_ref, dest_ref[0, t * TOP_K + k]),
                                  sem).start(priority=k % 2)
    for _ in range(TOP_K):
        _slab_copy_wait(hp_ref, xb_ref, sem, tm)


def _scatter(dest2, hp, xzero, *, tm):
    t = hp.shape[0] // ROW_GROUPS
    return pl.pallas_call(
        _scatter_kernel,
        grid=(t // tm,),
        in_specs=[
            pl.BlockSpec((None, 1, tm * TOP_K), lambda i: (i, 0, 0), memory_space=pltpu.SMEM),
            pl.BlockSpec((ROW_GROUPS * tm, LANES), lambda i: (i, 0)),
            pl.BlockSpec(memory_space=pl.ANY),
        ],
        out_specs=pl.BlockSpec(memory_space=pl.ANY),
        out_shape=jax.ShapeDtypeStruct(xzero.shape, xzero.dtype),
        scratch_shapes=[pltpu.SemaphoreType.DMA],
        input_output_aliases={2: 0},
        compiler_params=_cparams("arbitrary"),
        name="scatter",
    )(dest2, hp, xzero)


def _experts_kernel(be_ref, nu_ref, xb_ref, wgu_ref, bgu_ref, wd_ref, bd_ref, yb_ref, wgu_bf, wd_bf):
    i = pl.program_id(0)
    fresh = jnp.logical_or(i == 0, be_ref[i] != be_ref[jnp.maximum(i - 1, 0)])

    @pl.when(fresh)
    def _():
        wgu_bf[...] = wgu_ref[...].astype(BF16)
        wd_bf[...] = wd_ref[...].astype(BF16)

    @pl.when(i < nu_ref[0])
    def _():
        x_lo, x_hi = _unpack_bf16_pairs(_load_interleaved(xb_ref))
        x = jnp.concatenate([x_lo.astype(BF16), x_hi.astype(BF16)], axis=1)
        gu = jnp.dot(x, wgu_bf[...], preferred_element_type=F32) + bgu_ref[...]
        f = gu.shape[1] // 2
        g_lin = jnp.minimum(gu[:, :f], SWIGLU_LIMIT)
        u_lin = jnp.clip(gu[:, f:], -SWIGLU_LIMIT, SWIGLU_LIMIT)
        act = g_lin * _sigmoid(SWIGLU_ALPHA * g_lin) * (u_lin + 1.0)
        y = jnp.dot(act.astype(BF16), wd_bf[...], preferred_element_type=F32) + bd_ref[...]
        _store_interleaved(yb_ref, _pack_bf16_pairs(y))

    @pl.when(i >= nu_ref[0])
    def _():
        yb_ref[...] = jnp.zeros_like(yb_ref)


def _experts(block_e, n_used, xb, w_gate_up, b_gate_up3, w_down, b_down3, *, rb):
    n_rows = xb.shape[0] // ROW_GROUPS
    d = w_gate_up.shape[1]
    f2 = w_gate_up.shape[2]
    f = w_down.shape[1]
    nblk = n_rows // rb
    rowblk = lambda i, be, nu: (jnp.minimum(i, nu[0] - 1), 0)
    wsel = lambda i, be, nu: (be[i], 0, 0)
    grid_spec = pltpu.PrefetchScalarGridSpec(
        num_scalar_prefetch=2,
        grid=(nblk,),
        in_specs=[
            pl.BlockSpec((ROW_GROUPS * rb, LANES), rowblk),
            pl.BlockSpec((None, d, f2), wsel),
            pl.BlockSpec((None, 1, f2), wsel),
            pl.BlockSpec((None, f, d), wsel),
            pl.BlockSpec((None, 1, d), wsel),
        ],
        out_specs=pl.BlockSpec((ROW_GROUPS * rb, LANES), lambda i, be, nu: (i, 0)),
        scratch_shapes=[pltpu.VMEM((d, f2), BF16), pltpu.VMEM((f, d), BF16)],
    )
    return pl.pallas_call(
        _experts_kernel,
        grid_spec=grid_spec,
        out_shape=jax.ShapeDtypeStruct(xb.shape, U32),
        compiler_params=_cparams("arbitrary"),
        name="experts",
    )(block_e, n_used, xb, w_gate_up, b_gate_up3, w_down, b_down3)


def _combine_kernel(dest0_ref, destn_ref, x1_ref, tg_ref, yb_ref, o_ref, buf, sem):
    i = pl.program_id(0)
    n = pl.num_programs(0)
    tm = x1_ref.shape[0]
    slot = i % 2

    def gather_one(dest_ref, s, t):
        for k in range(TOP_K):
            pltpu.make_async_copy(_slab(yb_ref, dest_ref[0, t * TOP_K + k]), _slab(buf.at[s, k], t),
                                  sem.at[s]).start(priority=k % 2)

    @pl.when(i == 0)
    def _():
        def body(t, c):
            gather_one(dest0_ref, 0, t)
            return c
        lax.fori_loop(0, tm, body, 0)

    @pl.when(i + 1 < n)
    def _():
        for t in range(tm):
            gather_one(destn_ref, 1 - slot, t)

    for k in range(TOP_K):
        _slab_copy_wait(yb_ref, buf.at[slot, k], sem.at[slot], tm)
    tg = tg_ref[...]
    half = x1_ref.shape[1] // 2
    for j in range(ROW_GROUPS):
        lo_cols = slice(j * LANES, (j + 1) * LANES)
        hi_cols = slice(half + j * LANES, half + (j + 1) * LANES)
        out_lo = x1_ref[:, lo_cols]
        out_hi = x1_ref[:, hi_cols]
        for k in range(TOP_K):
            y_lo, y_hi = _unpack_bf16_pairs(buf[slot, k, pl.ds(j, tm, stride=ROW_GROUPS), :])
            out_lo = out_lo + tg[:, k:k + 1] * y_lo
            out_hi = out_hi + tg[:, k:k + 1] * y_hi
        o_ref[:, lo_cols] = out_lo
        o_ref[:, hi_cols] = out_hi


def _combine(dest2, x1, tg, yb, *, tm):
    t, d = x1.shape
    n = t // tm
    return pl.pallas_call(
        _combine_kernel,
        grid=(n,),
        in_specs=[
            pl.BlockSpec((None, 1, tm * TOP_K), lambda i: (0, 0, 0), memory_space=pltpu.SMEM),
            pl.BlockSpec((None, 1, tm * TOP_K), lambda i: (jnp.minimum(i + 1, n - 1), 0, 0),
                         memory_space=pltpu.SMEM),
            pl.BlockSpec((tm, d), lambda i: (i, 0)),
            pl.BlockSpec((tm, LANES), lambda i: (i, 0)),
            pl.BlockSpec(memory_space=pl.ANY),
        ],
        out_specs=pl.BlockSpec((tm, d), lambda i: (i, 0)),
        out_shape=jax.ShapeDtypeStruct((t, d), F32),
        scratch_shapes=[pltpu.VMEM((2, TOP_K, ROW_GROUPS * tm, LANES), yb.dtype), pltpu.SemaphoreType.DMA((2,))],
        compiler_params=_cparams("arbitrary"),
        name="combine",
    )(dest2, dest2, x1, tg, yb)


def _block_ones(n, blk):
    i = jnp.arange(n)
    return (i[:, None] // blk == i[None, :] // blk).astype(BF16)


def _pair_suffix(tq):
    j = jnp.arange(2 * tq)
    return ((j[:, None] > j[None, :]) & (j[:, None] // tq == j[None, :] // tq)).astype(BF16)


def _chunk_tril(n, blk):
    i = jnp.arange(n)
    return ((i[:, None] // blk == i[None, :] // blk) & (i[None, :] <= i[:, None])).astype(BF16)


def _strict_tril(n):
    i = jnp.arange(n)
    return (i[None, :] < i[:, None]).astype(BF16)


def _layer(x, norm1_g, w_in, q_g, k_g, sbo_g, lb, hgo_g, w_out, norm2_g, w_router, b_router,
           w_gate_up, b_gate_up, w_down, b_down):
    bsz, seq, d = x.shape
    rb = RB_EXPERTS
    t = bsz * seq
    x2 = x.reshape(t, d)
    nsb = SB_WIDTH // SB_HEAD_DIM

    qkv, hq, hg, hk, hi, sg = _inproj(
        x2, norm1_g.reshape(1, d), w_in.astype(BF16),
        jnp.tile(q_g, nsb).reshape(1, SB_WIDTH), jnp.tile(k_g, nsb).reshape(1, SB_WIDTH),
        lb.reshape(1, HG_WIDTH), _block_ones(MXU_TILE, SB_HEAD_DIM), tm=TM_INPROJ)

    n_assign = t * TOP_K
    nblk = n_assign // rb + N_EXPERTS
    a, xzero = _sbattn(qkv.reshape(bsz, seq, 3 * SB_WIDTH), sbo_g.reshape(1, SB_WIDTH),
                       _pair_suffix(TQ_ATTN), _block_ones(LANES, SB_HEAD_DIM), tq=TQ_ATTN,
                       zero_rows=ROW_GROUPS * nblk * rb)
    sh = (bsz, seq, HG_WIDTH)
    r = _hgrn(hq.reshape(sh), hg.reshape(sh), hk.reshape(sh), hi.reshape(sh), sg.reshape(sh),
              hgo_g.reshape(1, HG_WIDTH), _chunk_tril(LANES, HG_CHUNK))

    wr = jnp.zeros((d, LANES), F32).at[:, :N_EXPERTS].set(w_router)
    br = jnp.zeros((1, LANES), F32).at[0, :N_EXPERTS].set(b_router)
    x1, hp, ti, tg, rk, cnt = _outproj(
        x2, a.reshape(t, SB_WIDTH), r.reshape(t, HG_WIDTH), w_out.astype(BF16),
        norm2_g.reshape(1, d), wr, br, _strict_tril(TM_OUTPROJ), tm=TM_OUTPROJ)

    counts = cnt[0, :N_EXPERTS].astype(I32)
    padded = (counts + rb - 1) // rb * rb
    pad_ends = jnp.cumsum(padded)
    pad_starts = pad_ends - padded
    eids = jnp.arange(N_EXPERTS, dtype=I32)
    dest = rk[:, :TOP_K] + jnp.sum(jnp.where(ti[:, :TOP_K, None] == eids, pad_starts, 0), axis=-1)
    n_used = (pad_ends[-1] // rb).astype(I32).reshape(1)
    blk_start = jnp.minimum(jnp.arange(nblk, dtype=I32), n_used[0] - 1) * rb
    block_e = jnp.sum((blk_start[:, None] >= pad_ends[None, :]).astype(I32), axis=1)

    xb = _scatter(dest.reshape(t // TM_SCATTER, 1, TM_SCATTER * TOP_K), hp, xzero, tm=TM_SCATTER)
    yb = _experts(block_e, n_used, xb, w_gate_up, b_gate_up.reshape(N_EXPERTS, 1, -1),
                  w_down, b_down.reshape(N_EXPERTS, 1, -1), rb=rb)
    out = _combine(dest.reshape(t // TM_COMBINE, 1, TM_COMBINE * TOP_K), x1, tg, yb, tm=TM_COMBINE)
    return out.reshape(bsz, seq, d)


def kernel(x, norm1_g, w_in, sb_q_norm_g, sb_k_norm_g, sb_out_norm_g, hg_lb_logits, hg_out_norm_g,
           w_out, norm2_g, w_router, b_router, w_gate_up, b_gate_up, w_down, b_down):
    depth = w_in.shape[0]
    lb_all = jnp.cumsum(jax.nn.softmax(hg_lb_logits.astype(F32), axis=0), axis=0)
    for l in range(depth):
        x = _layer(x, norm1_g[l], w_in[l], sb_q_norm_g[l], sb_k_norm_g[l], sb_out_norm_g[l], lb_all[l],
                   hg_out_norm_g[l], w_out[l], norm2_g[l], w_router[l], b_router[l],
                   w_gate_up[l], b_gate_up[l], w_down[l], b_down[l])
    return x
```

```python
import functools

import jax
import jax.numpy as jnp
from jax import lax
from jax.experimental import pallas as pl
from jax.experimental.pallas import tpu as pltpu

F32 = jnp.float32
BF16 = jnp.bfloat16
I32 = jnp.int32
U32 = jnp.uint32

EPS = 1e-6
LANES = 128
SB_HEAD_DIM = 64
SB_WIDTH = 512
HG_HEAD_DIM = 128
HG_WIDTH = 512
HG_CHUNK = 32
HG_GROUPS = 4
N_EXPERTS = 32
TOP_K = 4
SWIGLU_LIMIT = 7.0
SWIGLU_ALPHA = 1.702
VMEM_LIMIT = 56 * 1024 * 1024

TM_INPROJ = 512
TQ_ATTN = 256
TM_OUTPROJ = 512
TM_SCATTER = 2048
TM_COMBINE = 256
RB_EXPERTS = 512

SEG = SB_WIDTH
MXU_TILE = 256

NT_DIMS = (((1,), (1,)), ((), ()))
NPAIR = SB_WIDTH // LANES
LOG2E = 1.4426950408889634


def _cparams(*sem):
    return pltpu.CompilerParams(dimension_semantics=sem, vmem_limit_bytes=VMEM_LIMIT)


def _bf16_split(a, n):
    parts = []
    for _ in range(n - 1):
        p = a.astype(BF16)
        parts.append(p)
        a = a - p.astype(F32)
    parts.append(a.astype(BF16))
    return parts


def _sigmoid(a):
    return 1.0 / (1.0 + jnp.exp(-a))


def _pack_bf16_pairs(a):
    w = a.shape[1] // 2
    lo_bits = pltpu.bitcast(a[:, :w].astype(BF16).astype(F32), U32)
    hi_bits = pltpu.bitcast(a[:, w:].astype(BF16).astype(F32), U32)
    return (hi_bits & jnp.uint32(0xFFFF0000)) | (lo_bits >> 16)


def _unpack_bf16_pairs(p):
    return pltpu.bitcast(p << 16, F32), pltpu.bitcast(p & jnp.uint32(0xFFFF0000), F32)


ROW_GROUPS = 4


def _store_interleaved(ref, a):
    rows = a.shape[0]
    for j in range(ROW_GROUPS):
        ref[pl.ds(j, rows, stride=ROW_GROUPS), :] = a[:, j * LANES:(j + 1) * LANES]


def _load_interleaved(ref, rows):
    return jnp.concatenate([ref[pl.ds(j, rows, stride=ROW_GROUPS), :] for j in range(ROW_GROUPS)], axis=1)


def _inproj_kernel(x_ref, g1_ref, w_ref, gq_ref, gk_ref, lb_ref, bd_ref,
                   qkv_ref, hq_ref, hg_ref, hk_ref, hi_ref, sg_ref):
    x = x_ref[...]
    ms = jnp.mean(x * x, axis=-1, keepdims=True)
    h = (x * lax.rsqrt(ms + EPS) * g1_ref[...]).astype(BF16)
    bd = bd_ref[...]

    def seg(j):
        return jnp.dot(h, w_ref[:, j * SEG:(j + 1) * SEG], preferred_element_type=F32)

    def head_norm(a, g, scale):
        outs = []
        for c in range(SEG // MXU_TILE):
            ac = a[:, c * MXU_TILE:(c + 1) * MXU_TILE]
            ss = jnp.dot((ac * ac).astype(BF16), bd, preferred_element_type=F32)
            outs.append(ac * lax.rsqrt(ss * (1.0 / SB_HEAD_DIM) + EPS))
        y = jnp.concatenate(outs, axis=1) * g
        return y * scale if scale != 1.0 else y

    qkv_ref[:, 0:SEG] = head_norm(seg(0), gq_ref[...], LOG2E * SB_HEAD_DIM ** -0.5).astype(BF16)
    qkv_ref[:, SEG:2 * SEG] = head_norm(seg(1), gk_ref[...], 1.0).astype(BF16)
    qkv_ref[:, 2 * SEG:3 * SEG] = seg(2).astype(BF16)
    hq_ref[...] = seg(3).astype(BF16)
    f = seg(4)
    lb = lb_ref[...]
    hg_ref[...] = jnp.log(lb + (1.0 - lb) * _sigmoid(f))
    hk_ref[...] = (1.0 - lb) * _sigmoid(-f)
    hi_ref[...] = seg(5).astype(BF16)
    gate = seg(6)
    sg_ref[...] = (gate * _sigmoid(gate)).astype(BF16)


def _inproj(x2, g1, w_bf, gq, gk, lb, bd, *, tm):
    t, d = x2.shape
    n = w_bf.shape[1]
    row = lambda i: (i, 0)
    fix = lambda i: (0, 0)
    return pl.pallas_call(
        _inproj_kernel,
        grid=(t // tm,),
        in_specs=[
            pl.BlockSpec((tm, d), row),
            pl.BlockSpec((1, d), fix),
            pl.BlockSpec((d, n), fix),
            pl.BlockSpec((1, SEG), fix),
            pl.BlockSpec((1, SEG), fix),
            pl.BlockSpec((1, SEG), fix),
            pl.BlockSpec((MXU_TILE, MXU_TILE), fix),
        ],
        out_specs=[
            pl.BlockSpec((tm, 3 * SEG), row),
            pl.BlockSpec((tm, SEG), row),
            pl.BlockSpec((tm, SEG), row),
            pl.BlockSpec((tm, SEG), row),
            pl.BlockSpec((tm, SEG), row),
            pl.BlockSpec((tm, SEG), row),
        ],
        out_shape=[
            jax.ShapeDtypeStruct((t, 3 * SEG), BF16),
            jax.ShapeDtypeStruct((t, SEG), BF16),
            jax.ShapeDtypeStruct((t, SEG), F32),
            jax.ShapeDtypeStruct((t, SEG), F32),
            jax.ShapeDtypeStruct((t, SEG), BF16),
            jax.ShapeDtypeStruct((t, SEG), BF16),
        ],
        compiler_params=_cparams("parallel"),
        name="inproj",
    )(x2, g1, w_bf, gq, gk, lb, bd)


def _sbattn_kernel(q_ref, k_ref, v_ref, go_ref, u2_ref, bd_ref, o_ref, zero_ref, vm_ref, *scr, tq):
    acc_ref = scr[0:NPAIR]
    carry_ref = scr[NPAIR:2 * NPAIR]
    z_ref = scr[2 * NPAIR:3 * NPAIR]
    w_ref = scr[3 * NPAIR:4 * NPAIR]
    qi = pl.program_id(1)
    hps = range(NPAIR)
    lane = lax.broadcasted_iota(I32, (tq, LANES), 1)
    lo = lane < SB_HEAD_DIM

    @pl.when(qi == 0)
    def _():
        v = v_ref[...]
        lo_all = (lax.broadcasted_iota(I32, v.shape, 1) & (LANES - 1)) < SB_HEAD_DIM
        zv = jnp.zeros_like(v)
        vm_ref[0] = jnp.where(lo_all, v, zv)
        vm_ref[1] = jnp.where(lo_all, zv, v)

    u2 = u2_ref[...]
    for hp in hps:
        acc_ref[hp][...] = jnp.zeros_like(acc_ref[hp])
        carry_ref[hp][...] = jnp.zeros_like(carry_ref[hp])
    row = lax.broadcasted_iota(I32, (2 * tq, tq), 0)
    col = lax.broadcasted_iota(I32, (2 * tq, tq), 1)
    strict = col < jnp.where(row >= tq, row - tq, row)
    q2s = []
    for hp in hps:
        q = q_ref[:, hp * LANES:(hp + 1) * LANES]
        zq = jnp.zeros_like(q)
        q2s.append(jnp.concatenate([jnp.where(lo, q, zq), jnp.where(lo, zq, q)], axis=0))

    def scores(kb):
        off = pl.multiple_of(kb * tq, tq)
        return [lax.dot_general(q2s[hp], k_ref[pl.ds(off, tq), hp * LANES:(hp + 1) * LANES], NT_DIMS,
                                preferred_element_type=F32) for hp in hps]

    def weighted_values(kb):
        off = pl.multiple_of(kb * tq, tq)
        for hp in hps:
            vcat = jnp.concatenate([vm_ref[0, pl.ds(off, tq), hp * LANES:(hp + 1) * LANES],
                                    vm_ref[1, pl.ds(off, tq), hp * LANES:(hp + 1) * LANES]], axis=0)
            acc_ref[hp][...] += jnp.dot(w_ref[hp][...], vcat, preferred_element_type=F32)

    def weights(zs, masked):
        lgs = []
        for z in zs:
            nz = -z
            lg = jnp.minimum(nz, 0.0) - jnp.log2(1.0 + jnp.exp2(jnp.minimum(z, nz)))
            if masked:
                lg = jnp.where(strict, lg, 0.0)
            lgs.append(lg)
        withins = []
        for pr in range(NPAIR // 2):
            cat = jnp.concatenate([lgs[2 * pr], lgs[2 * pr + 1]], axis=1).astype(BF16)
            res = jnp.dot(cat, u2, preferred_element_type=F32)
            withins += [res[:, :tq], res[:, tq:]]
        reps = tq // LANES
        carries = [jnp.concatenate([carry_ref[hp][...]] * reps, axis=1) for hp in hps]
        ws = [jnp.exp2((zs[hp] + lgs[hp]) + (carries[hp] + withins[hp])) for hp in hps]
        if masked:
            ws = [jnp.where(strict, w, 0.0) for w in ws]
        for hp in hps:
            carry_ref[hp][...] += jnp.sum(lgs[hp], axis=-1, keepdims=True)
        return [jnp.concatenate([w[:tq], w[tq:]], axis=1).astype(BF16) for w in ws]

    ws = weights(scores(qi), True)
    zn = scores(jnp.maximum(qi - 1, 0))
    for hp in hps:
        w_ref[hp][...] = ws[hp]
        z_ref[hp][...] = zn[hp]

    def body(it, c):
        kb = qi - 1 - it
        zs = [z_ref[hp][...] for hp in hps]
        zn = scores(jnp.maximum(kb - 1, 0))
        weighted_values(kb + 1)
        ws = weights(zs, False)
        for hp in hps:
            w_ref[hp][...] = ws[hp]
            z_ref[hp][...] = zn[hp]
        return c

    lax.fori_loop(0, qi, body, 0)
    weighted_values(0)
    for hp in hps:
        a = acc_ref[hp][...]
        ss = jnp.dot((a * a).astype(BF16), bd_ref[...], preferred_element_type=F32)
        o_ref[:, hp * LANES:(hp + 1) * LANES] = (
            a * lax.rsqrt(ss * (1.0 / SB_HEAD_DIM) + EPS) * go_ref[:, hp * LANES:(hp + 1) * LANES]).astype(BF16)
    zero_ref[...] = jnp.zeros_like(zero_ref)


def _sbattn(qkv, go, u2, bd, *, tq, zero_rows):
    b, s, _ = qkv.shape
    nq = s // tq
    zrows = zero_rows // (b * nq)
    assert zrows * b * nq == zero_rows and zrows % 8 == 0
    return pl.pallas_call(
        functools.partial(_sbattn_kernel, tq=tq),
        grid=(b, s // tq),
        in_specs=[
            pl.BlockSpec((None, tq, SB_WIDTH), lambda bi, qi: (bi, qi, 0)),
            pl.BlockSpec((None, s, SB_WIDTH), lambda bi, qi: (bi, 0, 1)),
            pl.BlockSpec((None, s, SB_WIDTH), lambda bi, qi: (bi, 0, 2)),
            pl.BlockSpec((1, SB_WIDTH), lambda bi, qi: (0, 0)),
            pl.BlockSpec((2 * tq, 2 * tq), lambda bi, qi: (0, 0)),
            pl.BlockSpec((LANES, LANES), lambda bi, qi: (0, 0)),
        ],
        out_specs=[pl.BlockSpec((None, tq, SB_WIDTH), lambda bi, qi: (bi, qi, 0)),
                   pl.BlockSpec((zrows, LANES), lambda bi, qi: (bi * nq + qi, 0))],
        out_shape=[jax.ShapeDtypeStruct((b, s, SB_WIDTH), BF16),
                   jax.ShapeDtypeStruct((zero_rows, LANES), U32)],
        scratch_shapes=([pltpu.VMEM((2, s, SB_WIDTH), BF16)]
                        + [pltpu.VMEM((tq, LANES), F32)] * NPAIR
                        + [pltpu.VMEM((2 * tq, LANES), F32)] * NPAIR
                        + [pltpu.VMEM((2 * tq, tq), F32)] * NPAIR
                        + [pltpu.VMEM((tq, 2 * tq), BF16)] * NPAIR),
        compiler_params=_cparams("parallel", "arbitrary"),
        name="sbattn",
    )(qkv, qkv, qkv, go, u2, bd)


def _hgrn_kernel(q_ref, g_ref, k_ref, v_ref, sg_ref, gn_ref, tl_ref, o_ref, *state_ref):
    s = q_ref.shape[0]
    grp = LANES
    nch = grp // HG_CHUNK
    hs = range(HG_WIDTH // HG_HEAD_DIM)
    for h in hs:
        state_ref[h][...] = jnp.zeros_like(state_ref[h])
    tl = tl_ref[...]
    rr = lax.broadcasted_iota(I32, (grp, grp), 0)
    cc = lax.broadcasted_iota(I32, (grp, grp), 1)
    shift = HG_CHUNK.bit_length() - 1
    rchunk = rr >> shift
    causal = (rchunk == (cc >> shift)) & (cc <= rr)
    inchunk = [rchunk == ci for ci in range(nch)]

    def group(r, c):
        col = lambda h: slice(h * HG_HEAD_DIM, (h + 1) * HG_HEAD_DIM)
        gs = range(HG_GROUPS)
        rows = [pl.ds(pl.multiple_of((r * HG_GROUPS + g) * grp, grp), grp) for g in gs]
        ch = [(g, h) for g in gs for h in hs]
        vs = {c_: v_ref[rows[c_[0]], col(c_[1])] for c_ in ch}
        bs = {c_: sum(jnp.dot(tl, p, preferred_element_type=F32)
                      for p in _bf16_split(g_ref[rows[c_[0]], col(c_[1])], 2))
              for c_ in ch}
        lasts = {c_: [bs[c_][(ci + 1) * HG_CHUNK - 1:(ci + 1) * HG_CHUNK, :] for ci in range(nch)] for c_ in ch}
        bls = {c_: jnp.concatenate([jnp.broadcast_to(l, (HG_CHUNK, HG_HEAD_DIM)) for l in lasts[c_]], axis=0)
               for c_ in ch}
        qes = {c_: (q_ref[rows[c_[0]], col(c_[1])].astype(F32) * jnp.exp(bs[c_])).astype(BF16) for c_ in ch}
        kes = {c_: (k_ref[rows[c_[0]], col(c_[1])] * jnp.exp(-bs[c_])).astype(BF16) for c_ in ch}
        kds = {c_: (k_ref[rows[c_[0]], col(c_[1])] * jnp.exp(bls[c_] - bs[c_])).astype(BF16) for c_ in ch}
        attns = {c_: jnp.where(causal, lax.dot_general(qes[c_], kes[c_], NT_DIMS, preferred_element_type=F32), 0.0)
                 .astype(BF16) for c_ in ch}
        os = {c_: jnp.dot(attns[c_], vs[c_], preferred_element_type=F32) for c_ in ch}
        vts = {c_: vs[c_].astype(F32).T.astype(BF16) for c_ in ch}
        uts = {}
        for c_ in ch:
            zk = jnp.zeros_like(kds[c_])
            kdx = jnp.concatenate([jnp.where(inchunk[ci], kds[c_], zk) for ci in range(nch)], axis=1)
            uts[c_] = jnp.dot(vts[c_], kdx, preferred_element_type=F32)
        for h in hs:
            st = state_ref[h][...]
            for g in gs:
                c_ = (g, h)
                parts = []
                for ci in range(nch):
                    parts.append(st.astype(BF16))
                    st = st * jnp.exp(lasts[c_][ci]) + uts[c_][:, ci * HG_HEAD_DIM:(ci + 1) * HG_HEAD_DIM]
                zq = jnp.zeros_like(qes[c_])
                qx = jnp.concatenate([jnp.where(inchunk[ci], qes[c_], zq) for ci in range(nch)], axis=1)
                o = os[c_] + lax.dot_general(qx, jnp.concatenate(parts, axis=1), NT_DIMS,
                                             preferred_element_type=F32)
                ms = jnp.mean(o * o, axis=-1, keepdims=True)
                y = o * lax.rsqrt(ms + EPS) * gn_ref[:, col(h)]
                o_ref[rows[g], col(h)] = (y * sg_ref[rows[g], col(h)].astype(F32)).astype(BF16)
            state_ref[h][...] = st
        return c

    lax.fori_loop(0, s // (grp * HG_GROUPS), group, 0)


def _hgrn(hq, hg, hk, hi, sg, gn, tl):
    b, s, w = hq.shape
    nh = HG_WIDTH // HG_HEAD_DIM
    blk = pl.BlockSpec((None, s, w), lambda bi: (bi, 0, 0))
    return pl.pallas_call(
        _hgrn_kernel,
        grid=(b,),
        in_specs=[blk, blk, blk, blk, blk,
                  pl.BlockSpec((1, w), lambda bi: (0, 0)),
                  pl.BlockSpec((LANES, LANES), lambda bi: (0, 0))],
        out_specs=blk,
        out_shape=jax.ShapeDtypeStruct((b, s, w), BF16),
        scratch_shapes=[pltpu.VMEM((HG_HEAD_DIM, HG_HEAD_DIM), F32)] * nh,
        compiler_params=_cparams("parallel"),
        name="hgrn",
    )(hq, hg, hk, hi, sg, gn, tl)


def _outproj_kernel(x_ref, a_ref, r_ref, wo_ref, g2_ref, wr_ref, br_ref, tri_ref,
                    x1_ref, hp_ref, ti_ref, tg_ref, rk_ref, cnt_ref, run_ref):
    i = pl.program_id(0)

    @pl.when(i == 0)
    def _():
        run_ref[...] = jnp.zeros_like(run_ref)

    half = a_ref.shape[1]
    x1 = (x_ref[...]
          + jnp.dot(a_ref[...], wo_ref[:half, :], preferred_element_type=F32)
          + jnp.dot(r_ref[...], wo_ref[half:, :], preferred_element_type=F32))
    x1_ref[...] = x1
    ms = jnp.mean(x1 * x1, axis=-1, keepdims=True)
    h2 = x1 * lax.rsqrt(ms + EPS) * g2_ref[...]
    _store_interleaved(hp_ref, _pack_bf16_pairs(h2))

    h_hi, h_lo = _bf16_split(h2, 2)
    w_hi, w_lo = _bf16_split(wr_ref[...], 2)
    hi_both = jnp.dot(h_hi, jnp.concatenate([w_hi, w_lo], axis=1), preferred_element_type=F32)
    logits = ((hi_both[:, :LANES] + jnp.dot(h_lo, w_hi, preferred_element_type=F32))
              + hi_both[:, LANES:]) + br_ref[...]
    tm = logits.shape[0]
    lane = lax.broadcasted_iota(I32, (tm, LANES), 1).astype(F32)
    neg = jnp.float32(-jnp.inf)
    ti = jnp.zeros((tm, LANES), F32)
    tv = jnp.zeros((tm, LANES), F32)
    onehot = jnp.zeros((tm, LANES), F32)
    sels = []
    avail = lane < N_EXPERTS
    for k in range(TOP_K):
        m = jnp.max(jnp.where(avail, logits, neg), axis=-1, keepdims=True)
        idx = jnp.min(jnp.where(avail & (logits == m), lane, float(LANES)), axis=-1, keepdims=True)
        sel = lane == idx
        sels.append(sel)
        ti = jnp.where(lane == k, idx, ti)
        tv = jnp.where(lane == k, m, tv)
        onehot = jnp.where(sel, 1.0, onehot)
        avail = avail & jnp.logical_not(sel)
    valid = lane < TOP_K
    e = jnp.where(valid, jnp.exp(tv - jnp.max(jnp.where(valid, tv, neg), axis=-1, keepdims=True)), 0.0)
    tg_ref[...] = e / jnp.sum(e, axis=-1, keepdims=True)
    ti_ref[...] = ti.astype(I32)

    before = jnp.dot(tri_ref[...], onehot.astype(BF16), preferred_element_type=F32) + run_ref[...]
    rk = jnp.zeros((tm, LANES), F32)
    for k in range(TOP_K):
        rk = jnp.where(lane == k, jnp.sum(jnp.where(sels[k], before, 0.0), axis=-1, keepdims=True), rk)
    rk_ref[...] = rk.astype(I32)
    run_ref[...] += jnp.sum(onehot, axis=0, keepdims=True)
    cnt_ref[...] = run_ref[...]


def _outproj(x2, a2, r2, wo_bf, g2, wr, br, tri, *, tm):
    t, d = x2.shape
    half = a2.shape[1]
    row = lambda i: (i, 0)
    fix = lambda i: (0, 0)
    return pl.pallas_call(
        _outproj_kernel,
        grid=(t // tm,),
        in_specs=[
            pl.BlockSpec((tm, d), row),
            pl.BlockSpec((tm, half), row),
            pl.BlockSpec((tm, half), row),
            pl.BlockSpec((2 * half, d), fix),
            pl.BlockSpec((1, d), fix),
            pl.BlockSpec((d, LANES), fix),
            pl.BlockSpec((1, LANES), fix),
            pl.BlockSpec((tm, tm), fix),
        ],
        out_specs=[
            pl.BlockSpec((tm, d), row),
            pl.BlockSpec((ROW_GROUPS * tm, LANES), row),
            pl.BlockSpec((tm, LANES), row),
            pl.BlockSpec((tm, LANES), row),
            pl.BlockSpec((tm, LANES), row),
            pl.BlockSpec((1, LANES), fix),
        ],
        out_shape=[
            jax.ShapeDtypeStruct((t, d), F32),
            jax.ShapeDtypeStruct((ROW_GROUPS * t, LANES), U32),
            jax.ShapeDtypeStruct((t, LANES), I32),
            jax.ShapeDtypeStruct((t, LANES), F32),
            jax.ShapeDtypeStruct((t, LANES), I32),
            jax.ShapeDtypeStruct((1, LANES), F32),
        ],
        scratch_shapes=[pltpu.VMEM((1, LANES), F32)],
        compiler_params=_cparams("arbitrary"),
        name="outproj",
    )(x2, a2, r2, wo_bf, g2, wr, br, tri)


def _slab(ref, r):
    return ref.at[pl.ds(pl.multiple_of(ROW_GROUPS * r, ROW_GROUPS), ROW_GROUPS)]


def _slab_copy_wait(src, dst, sem, n):
    pltpu.make_async_copy(src.at[pl.ds(0, ROW_GROUPS * n)], dst.at[pl.ds(0, ROW_GROUPS * n)], sem).wait()


def _scatter_kernel(dest_ref, hp_ref, xz_ref, xb_ref, sem):
    del xz_ref
    tm = hp_ref.shape[0] // ROW_GROUPS
    for t in range(tm):
        for k in range(TOP_K):
            pltpu.make_async_copy(_slab(hp_ref, t), _slab(xb_ref, dest_ref[0, t * TOP_K + k]),
                                  sem).start(priority=k % 2)
    for _ in range(TOP_K):
        _slab_copy_wait(hp_ref, xb_ref, sem, tm)


def _scatter(dest2, hp, xzero, *, tm):
    t = hp.shape[0] // ROW_GROUPS
    return pl.pallas_call(
        _scatter_kernel,
        grid=(t // tm,),
        in_specs=[
            pl.BlockSpec((None, 1, tm * TOP_K), lambda i: (i, 0, 0), memory_space=pltpu.SMEM),
            pl.BlockSpec((ROW_GROUPS * tm, LANES), lambda i: (i, 0)),
            pl.BlockSpec(memory_space=pl.ANY),
        ],
        out_specs=pl.BlockSpec(memory_space=pl.ANY),
        out_shape=jax.ShapeDtypeStruct(xzero.shape, xzero.dtype),
        scratch_shapes=[pltpu.SemaphoreType.DMA],
        input_output_aliases={2: 0},
        compiler_params=_cparams("arbitrary"),
        name="scatter",
    )(dest2, hp, xzero)


def _experts_kernel(be_ref, nu_ref, nv_ref, xb_ref, wgu_ref, bgu_ref, wd_ref, bd_ref, yb_ref, wgu_bf, wd_bf):
    i = pl.program_id(0)
    rb = yb_ref.shape[0] // ROW_GROUPS
    fresh = jnp.logical_or(i == 0, be_ref[i] != be_ref[jnp.maximum(i - 1, 0)])

    @pl.when(fresh)
    def _():
        wgu_bf[...] = wgu_ref[...].astype(BF16)
        wd_bf[...] = wd_ref[...].astype(BF16)

    def compute(rows):
        x_lo, x_hi = _unpack_bf16_pairs(_load_interleaved(xb_ref, rows))
        x = jnp.concatenate([x_lo.astype(BF16), x_hi.astype(BF16)], axis=1)
        gu = jnp.dot(x, wgu_bf[...], preferred_element_type=F32) + bgu_ref[...]
        f = gu.shape[1] // 2
        g_lin = jnp.minimum(gu[:, :f], SWIGLU_LIMIT)
        u_lin = jnp.clip(gu[:, f:], -SWIGLU_LIMIT, SWIGLU_LIMIT)
        act = g_lin * _sigmoid(SWIGLU_ALPHA * g_lin) * (u_lin + 1.0)
        y = jnp.dot(act.astype(BF16), wd_bf[...], preferred_element_type=F32) + bd_ref[...]
        _store_interleaved(yb_ref, _pack_bf16_pairs(y))

    used = i < nu_ref[0]
    small = nv_ref[i] <= rb // 2

    @pl.when(used & jnp.logical_not(small))
    def _():
        compute(rb)

    @pl.when(used & small)
    def _():
        compute(rb // 2)
        yb_ref[pl.ds(ROW_GROUPS * (rb // 2), ROW_GROUPS * (rb // 2)), :] = jnp.zeros(
            (ROW_GROUPS * (rb // 2), LANES), yb_ref.dtype)

    @pl.when(jnp.logical_not(used))
    def _():
        yb_ref[...] = jnp.zeros_like(yb_ref)


def _experts(block_e, n_used, n_valid, xb, w_gate_up, b_gate_up3, w_down, b_down3, *, rb):
    n_rows = xb.shape[0] // ROW_GROUPS
    d = w_gate_up.shape[1]
    f2 = w_gate_up.shape[2]
    f = w_down.shape[1]
    nblk = n_rows // rb
    rowblk = lambda i, be, nu, nv: (jnp.minimum(i, nu[0] - 1), 0)
    wsel = lambda i, be, nu, nv: (be[i], 0, 0)
    grid_spec = pltpu.PrefetchScalarGridSpec(
        num_scalar_prefetch=3,
        grid=(nblk,),
        in_specs=[
            pl.BlockSpec((ROW_GROUPS * rb, LANES), rowblk),
            pl.BlockSpec((None, d, f2), wsel),
            pl.BlockSpec((None, 1, f2), wsel),
            pl.BlockSpec((None, f, d), wsel),
            pl.BlockSpec((None, 1, d), wsel),
        ],
        out_specs=pl.BlockSpec((ROW_GROUPS * rb, LANES), lambda i, be, nu, nv: (i, 0)),
        scratch_shapes=[pltpu.VMEM((d, f2), BF16), pltpu.VMEM((f, d), BF16)],
    )
    return pl.pallas_call(
        _experts_kernel,
        grid_spec=grid_spec,
        out_shape=jax.ShapeDtypeStruct(xb.shape, U32),
        compiler_params=_cparams("arbitrary"),
        name="experts",
    )(block_e, n_used, n_valid, xb, w_gate_up, b_gate_up3, w_down, b_down3)


def _combine_kernel(dest0_ref, destn_ref, x1_ref, tg_ref, yb_ref, o_ref, buf, sem):
    i = pl.program_id(0)
    n = pl.num_programs(0)
    tm = x1_ref.shape[0]
    slot = i % 2

    def gather_one(dest_ref, s, t):
        for k in range(TOP_K):
            pltpu.make_async_copy(_slab(yb_ref, dest_ref[0, t * TOP_K + k]), _slab(buf.at[s, k], t),
                                  sem.at[s]).start(priority=k % 2)

    @pl.when(i == 0)
    def _():
        def body(t, c):
            gather_one(dest0_ref, 0, t)
            return c
        lax.fori_loop(0, tm, body, 0)

    @pl.when(i + 1 < n)
    def _():
        for t in range(tm):
            gather_one(destn_ref, 1 - slot, t)

    for k in range(TOP_K):
        _slab_copy_wait(yb_ref, buf.at[slot, k], sem.at[slot], tm)
    tg = tg_ref[...]
    half = x1_ref.shape[1] // 2
    for j in range(ROW_GROUPS):
        lo_cols = slice(j * LANES, (j + 1) * LANES)
        hi_cols = slice(half + j * LANES, half + (j + 1) * LANES)
        out_lo = x1_ref[:, lo_cols]
        out_hi = x1_ref[:, hi_cols]
        for k in range(TOP_K):
            y_lo, y_hi = _unpack_bf16_pairs(buf[slot, k, pl.ds(j, tm, stride=ROW_GROUPS), :])
            out_lo = out_lo + tg[:, k:k + 1] * y_lo
            out_hi = out_hi + tg[:, k:k + 1] * y_hi
        o_ref[:, lo_cols] = out_lo
        o_ref[:, hi_cols] = out_hi


def _combine(dest2, x1, tg, yb, *, tm):
    t, d = x1.shape
    n = t // tm
    return pl.pallas_call(
        _combine_kernel,
        grid=(n,),
        in_specs=[
            pl.BlockSpec((None, 1, tm * TOP_K), lambda i: (0, 0, 0), memory_space=pltpu.SMEM),
            pl.BlockSpec((None, 1, tm * TOP_K), lambda i: (jnp.minimum(i + 1, n - 1), 0, 0),
                         memory_space=pltpu.SMEM),
            pl.BlockSpec((tm, d), lambda i: (i, 0)),
            pl.BlockSpec((tm, LANES), lambda i: (i, 0)),
            pl.BlockSpec(memory_space=pl.ANY),
        ],
        out_specs=pl.BlockSpec((tm, d), lambda i: (i, 0)),
        out_shape=jax.ShapeDtypeStruct((t, d), F32),
        scratch_shapes=[pltpu.VMEM((2, TOP_K, ROW_GROUPS * tm, LANES), yb.dtype), pltpu.SemaphoreType.DMA((2,))],
        compiler_params=_cparams("arbitrary"),
        name="combine",
    )(dest2, dest2, x1, tg, yb)


def _block_ones(n, blk):
    i = jnp.arange(n)
    return (i[:, None] // blk == i[None, :] // blk).astype(BF16)


def _pair_suffix(tq):
    j = jnp.arange(2 * tq)
    return ((j[:, None] > j[None, :]) & (j[:, None] // tq == j[None, :] // tq)).astype(BF16)


def _chunk_tril(n, blk):
    i = jnp.arange(n)
    return ((i[:, None] // blk == i[None, :] // blk) & (i[None, :] <= i[:, None])).astype(BF16)


def _strict_tril(n):
    i = jnp.arange(n)
    return (i[None, :] < i[:, None]).astype(BF16)


def _layer(x, norm1_g, w_in, q_g, k_g, sbo_g, lb, hgo_g, w_out, norm2_g, w_router, b_router,
           w_gate_up, b_gate_up, w_down, b_down):
    bsz, seq, d = x.shape
    rb = RB_EXPERTS
    t = bsz * seq
    x2 = x.reshape(t, d)
    nsb = SB_WIDTH // SB_HEAD_DIM

    qkv, hq, hg, hk, hi, sg = _inproj(
        x2, norm1_g.reshape(1, d), w_in.astype(BF16),
        jnp.tile(q_g, nsb).reshape(1, SB_WIDTH), jnp.tile(k_g, nsb).reshape(1, SB_WIDTH),
        lb.reshape(1, HG_WIDTH), _block_ones(MXU_TILE, SB_HEAD_DIM), tm=TM_INPROJ)

    n_assign = t * TOP_K
    nblk = n_assign // rb + N_EXPERTS
    a, xzero = _sbattn(qkv.reshape(bsz, seq, 3 * SB_WIDTH), sbo_g.reshape(1, SB_WIDTH),
                       _pair_suffix(TQ_ATTN), _block_ones(LANES, SB_HEAD_DIM), tq=TQ_ATTN,
                       zero_rows=ROW_GROUPS * nblk * rb)
    sh = (bsz, seq, HG_WIDTH)
    r = _hgrn(hq.reshape(sh), hg.reshape(sh), hk.reshape(sh), hi.reshape(sh), sg.reshape(sh),
              hgo_g.reshape(1, HG_WIDTH), _chunk_tril(LANES, HG_CHUNK))

    wr = jnp.zeros((d, LANES), F32).at[:, :N_EXPERTS].set(w_router)
    br = jnp.zeros((1, LANES), F32).at[0, :N_EXPERTS].set(b_router)
    x1, hp, ti, tg, rk, cnt = _outproj(
        x2, a.reshape(t, SB_WIDTH), r.reshape(t, HG_WIDTH), w_out.astype(BF16),
        norm2_g.reshape(1, d), wr, br, _strict_tril(TM_OUTPROJ), tm=TM_OUTPROJ)

    counts = cnt[0, :N_EXPERTS].astype(I32)
    padded = (counts + rb - 1) // rb * rb
    pad_ends = jnp.cumsum(padded)
    pad_starts = pad_ends - padded
    eids = jnp.arange(N_EXPERTS, dtype=I32)
    dest = rk[:, :TOP_K] + jnp.sum(jnp.where(ti[:, :TOP_K, None] == eids, pad_starts, 0), axis=-1)
    n_used = (pad_ends[-1] // rb).astype(I32).reshape(1)
    blk_start = jnp.minimum(jnp.arange(nblk, dtype=I32), n_used[0] - 1) * rb
    block_e = jnp.sum((blk_start[:, None] >= pad_ends[None, :]).astype(I32), axis=1)
    blk_end_valid = jnp.sum(jnp.where(block_e[:, None] == eids, pad_starts + counts, 0), axis=-1)
    n_valid = jnp.clip(blk_end_valid - blk_start, 0, rb).astype(I32)

    xb = _scatter(dest.reshape(t // TM_SCATTER, 1, TM_SCATTER * TOP_K), hp, xzero, tm=TM_SCATTER)
    yb = _experts(block_e, n_used, n_valid, xb, w_gate_up, b_gate_up.reshape(N_EXPERTS, 1, -1),
                  w_down, b_down.reshape(N_EXPERTS, 1, -1), rb=rb)
    out = _combine(dest.reshape(t // TM_COMBINE, 1, TM_COMBINE * TOP_K), x1, tg, yb, tm=TM_COMBINE)
    return out.reshape(bsz, seq, d)


def kernel(x, norm1_g, w_in, sb_q_norm_g, sb_k_norm_g, sb_out_norm_g, hg_lb_logits, hg_out_norm_g,
           w_out, norm2_g, w_router, b_router, w_gate_up, b_gate_up, w_down, b_down):
    depth = w_in.shape[0]
    lb_all = jnp.cumsum(jax.nn.softmax(hg_lb_logits.astype(F32), axis=0), axis=0)
    for l in range(depth):
        x = _layer(x, norm1_g[l], w_in[l], sb_q_norm_g[l], sb_k_norm_g[l], sb_out_norm_g[l], lb_all[l],
                   hg_out_norm_g[l], w_out[l], norm2_g[l], w_router[l], b_router[l],
                   w_gate_up[l], b_gate_up[l], w_down[l], b_down[l])
    return x
```

```python
import functools

import jax
import jax.numpy as jnp
from jax import lax
from jax.experimental import pallas as pl
from jax.experimental.pallas import tpu as pltpu

F32 = jnp.float32
BF16 = jnp.bfloat16
I32 = jnp.int32
U32 = jnp.uint32

EPS = 1e-6
LANES = 128
SB_HEAD_DIM = 64
SB_WIDTH = 512
HG_HEAD_DIM = 128
HG_WIDTH = 512
HG_CHUNK = 32
HG_GROUPS = 8
N_EXPERTS = 32
TOP_K = 4
SWIGLU_LIMIT = 7.0
SWIGLU_ALPHA = 1.702
VMEM_LIMIT = 56 * 1024 * 1024

TM_INPROJ = 512
TQ_ATTN = 256
TM_OUTPROJ = 512
TM_SCATTER = 2048
TM_COMBINE = 256
RB_EXPERTS = 512

SEG = SB_WIDTH
MXU_TILE = 256

NT_DIMS = (((1,), (1,)), ((), ()))
NPAIR = SB_WIDTH // LANES
LOG2E = 1.4426950408889634


def _cparams(*sem):
    return pltpu.CompilerParams(dimension_semantics=sem, vmem_limit_bytes=VMEM_LIMIT)


def _bf16_split(a, n):
    parts = []
    for _ in range(n - 1):
        p = a.astype(BF16)
        parts.append(p)
        a = a - p.astype(F32)
    parts.append(a.astype(BF16))
    return parts


def _sigmoid(a):
    return 1.0 / (1.0 + jnp.exp(-a))


def _pack_bf16_pairs(a):
    w = a.shape[1] // 2
    lo_bits = pltpu.bitcast(a[:, :w].astype(BF16).astype(F32), U32)
    hi_bits = pltpu.bitcast(a[:, w:].astype(BF16).astype(F32), U32)
    return (hi_bits & jnp.uint32(0xFFFF0000)) | (lo_bits >> 16)


def _unpack_bf16_pairs(p):
    return pltpu.bitcast(p << 16, F32), pltpu.bitcast(p & jnp.uint32(0xFFFF0000), F32)


ROW_GROUPS = 4


def _store_interleaved(ref, a):
    rows = a.shape[0]
    for j in range(ROW_GROUPS):
        ref[pl.ds(j, rows, stride=ROW_GROUPS), :] = a[:, j * LANES:(j + 1) * LANES]


def _load_interleaved(ref):
    rows = ref.shape[0] // ROW_GROUPS
    return jnp.concatenate([ref[pl.ds(j, rows, stride=ROW_GROUPS), :] for j in range(ROW_GROUPS)], axis=1)


def _inproj_kernel(x_ref, g1_ref, w_ref, gq_ref, gk_ref, lb_ref, bd_ref,
                   qkv_ref, hq_ref, hg_ref, hk_ref, hi_ref, sg_ref):
    x = x_ref[...]
    ms = jnp.mean(x * x, axis=-1, keepdims=True)
    h = (x * lax.rsqrt(ms + EPS) * g1_ref[...]).astype(BF16)
    bd = bd_ref[...]

    def seg(j):
        return jnp.dot(h, w_ref[:, j * SEG:(j + 1) * SEG], preferred_element_type=F32)

    def head_norm(a, g, scale):
        outs = []
        for c in range(SEG // MXU_TILE):
            ac = a[:, c * MXU_TILE:(c + 1) * MXU_TILE]
            ss = jnp.dot((ac * ac).astype(BF16), bd, preferred_element_type=F32)
            outs.append(ac * lax.rsqrt(ss * (1.0 / SB_HEAD_DIM) + EPS))
        y = jnp.concatenate(outs, axis=1) * g
        return y * scale if scale != 1.0 else y

    qkv_ref[:, 0:SEG] = head_norm(seg(0), gq_ref[...], LOG2E * SB_HEAD_DIM ** -0.5).astype(BF16)
    qkv_ref[:, SEG:2 * SEG] = head_norm(seg(1), gk_ref[...], 1.0).astype(BF16)
    qkv_ref[:, 2 * SEG:3 * SEG] = seg(2).astype(BF16)
    hq_ref[...] = seg(3).astype(BF16)
    f = seg(4)
    lb = lb_ref[...]
    hg_ref[...] = jnp.log(lb + (1.0 - lb) * _sigmoid(f))
    hk_ref[...] = (1.0 - lb) * _sigmoid(-f)
    hi_ref[...] = seg(5).astype(BF16)
    gate = seg(6)
    sg_ref[...] = (gate * _sigmoid(gate)).astype(BF16)


def _inproj(x2, g1, w_bf, gq, gk, lb, bd, *, tm):
    t, d = x2.shape
    n = w_bf.shape[1]
    row = lambda i: (i, 0)
    fix = lambda i: (0, 0)
    return pl.pallas_call(
        _inproj_kernel,
        grid=(t // tm,),
        in_specs=[
            pl.BlockSpec((tm, d), row),
            pl.BlockSpec((1, d), fix),
            pl.BlockSpec((d, n), fix),
            pl.BlockSpec((1, SEG), fix),
            pl.BlockSpec((1, SEG), fix),
            pl.BlockSpec((1, SEG), fix),
            pl.BlockSpec((MXU_TILE, MXU_TILE), fix),
        ],
        out_specs=[
            pl.BlockSpec((tm, 3 * SEG), row),
            pl.BlockSpec((tm, SEG), row),
            pl.BlockSpec((tm, SEG), row),
            pl.BlockSpec((tm, SEG), row),
            pl.BlockSpec((tm, SEG), row),
            pl.BlockSpec((tm, SEG), row),
        ],
        out_shape=[
            jax.ShapeDtypeStruct((t, 3 * SEG), BF16),
            jax.ShapeDtypeStruct((t, SEG), BF16),
            jax.ShapeDtypeStruct((t, SEG), F32),
            jax.ShapeDtypeStruct((t, SEG), F32),
            jax.ShapeDtypeStruct((t, SEG), BF16),
            jax.ShapeDtypeStruct((t, SEG), BF16),
        ],
        compiler_params=_cparams("parallel"),
        name="inproj",
    )(x2, g1, w_bf, gq, gk, lb, bd)


def _sbattn_kernel(q_ref, k_ref, v_ref, go_ref, u2_ref, bd_ref, o_ref, zero_ref, vm_ref, *scr, tq):
    acc_ref = scr[0:NPAIR]
    carry_ref = scr[NPAIR:2 * NPAIR]
    z_ref = scr[2 * NPAIR:3 * NPAIR]
    w_ref = scr[3 * NPAIR:4 * NPAIR]
    qi = pl.program_id(1)
    hps = range(NPAIR)
    lane = lax.broadcasted_iota(I32, (tq, LANES), 1)
    lo = lane < SB_HEAD_DIM

    @pl.when(qi == 0)
    def _():
        v = v_ref[...]
        lo_all = (lax.broadcasted_iota(I32, v.shape, 1) & (LANES - 1)) < SB_HEAD_DIM
        zv = jnp.zeros_like(v)
        vm_ref[0] = jnp.where(lo_all, v, zv)
        vm_ref[1] = jnp.where(lo_all, zv, v)

    u2 = u2_ref[...]
    for hp in hps:
        acc_ref[hp][...] = jnp.zeros_like(acc_ref[hp])
        carry_ref[hp][...] = jnp.zeros_like(carry_ref[hp])
    row = lax.broadcasted_iota(I32, (2 * tq, tq), 0)
    col = lax.broadcasted_iota(I32, (2 * tq, tq), 1)
    strict = col < jnp.where(row >= tq, row - tq, row)
    q2s = []
    for hp in hps:
        q = q_ref[:, hp * LANES:(hp + 1) * LANES]
        zq = jnp.zeros_like(q)
        q2s.append(jnp.concatenate([jnp.where(lo, q, zq), jnp.where(lo, zq, q)], axis=0))

    def scores(kb):
        off = pl.multiple_of(kb * tq, tq)
        return [lax.dot_general(q2s[hp], k_ref[pl.ds(off, tq), hp * LANES:(hp + 1) * LANES], NT_DIMS,
                                preferred_element_type=F32) for hp in hps]

    def weighted_values(kb):
        off = pl.multiple_of(kb * tq, tq)
        for hp in hps:
            vcat = jnp.concatenate([vm_ref[0, pl.ds(off, tq), hp * LANES:(hp + 1) * LANES],
                                    vm_ref[1, pl.ds(off, tq), hp * LANES:(hp + 1) * LANES]], axis=0)
            acc_ref[hp][...] += jnp.dot(w_ref[hp][...], vcat, preferred_element_type=F32)

    def weights(zs, masked):
        lgs = []
        for z in zs:
            nz = -z
            lg = jnp.minimum(nz, 0.0) - jnp.log2(1.0 + jnp.exp2(jnp.minimum(z, nz)))
            if masked:
                lg = jnp.where(strict, lg, 0.0)
            lgs.append(lg)
        withins = []
        for pr in range(NPAIR // 2):
            cat = jnp.concatenate([lgs[2 * pr], lgs[2 * pr + 1]], axis=1).astype(BF16)
            res = jnp.dot(cat, u2, preferred_element_type=F32)
            withins += [res[:, :tq], res[:, tq:]]
        reps = tq // LANES
        carries = [jnp.concatenate([carry_ref[hp][...]] * reps, axis=1) for hp in hps]
        ws = [jnp.exp2((zs[hp] + lgs[hp]) + (carries[hp] + withins[hp])) for hp in hps]
        if masked:
            ws = [jnp.where(strict, w, 0.0) for w in ws]
        for hp in hps:
            carry_ref[hp][...] += jnp.sum(lgs[hp], axis=-1, keepdims=True)
        return [jnp.concatenate([w[:tq], w[tq:]], axis=1).astype(BF16) for w in ws]

    zd = scores(qi)
    zn = scores(jnp.maximum(qi - 1, 0))
    ws = weights(zd, True)
    for hp in hps:
        w_ref[hp][...] = ws[hp]
        z_ref[hp][...] = zn[hp]

    def body(it, c):
        kb = qi - 1 - it
        zs = [z_ref[hp][...] for hp in hps]
        zn = scores(jnp.maximum(kb - 1, 0))
        weighted_values(kb + 1)
        ws = weights(zs, False)
        for hp in hps:
            w_ref[hp][...] = ws[hp]
            z_ref[hp][...] = zn[hp]
        return c

    lax.fori_loop(0, qi, body, 0)
    weighted_values(0)
    for hp in hps:
        a = acc_ref[hp][...]
        ss = jnp.dot((a * a).astype(BF16), bd_ref[...], preferred_element_type=F32)
        o_ref[:, hp * LANES:(hp + 1) * LANES] = (
            a * lax.rsqrt(ss * (1.0 / SB_HEAD_DIM) + EPS) * go_ref[:, hp * LANES:(hp + 1) * LANES]).astype(BF16)
    zero_ref[...] = jnp.zeros_like(zero_ref)


def _sbattn(qkv, go, u2, bd, *, tq, zero_rows):
    b, s, _ = qkv.shape
    nq = s // tq
    zrows = zero_rows // (b * nq)
    assert zrows * b * nq == zero_rows and zrows % 8 == 0
    return pl.pallas_call(
        functools.partial(_sbattn_kernel, tq=tq),
        grid=(b, s // tq),
        in_specs=[
            pl.BlockSpec((None, tq, SB_WIDTH), lambda bi, qi: (bi, qi, 0)),
            pl.BlockSpec((None, s, SB_WIDTH), lambda bi, qi: (bi, 0, 1)),
            pl.BlockSpec((None, s, SB_WIDTH), lambda bi, qi: (bi, 0, 2)),
            pl.BlockSpec((1, SB_WIDTH), lambda bi, qi: (0, 0)),
            pl.BlockSpec((2 * tq, 2 * tq), lambda bi, qi: (0, 0)),
            pl.BlockSpec((LANES, LANES), lambda bi, qi: (0, 0)),
        ],
        out_specs=[pl.BlockSpec((None, tq, SB_WIDTH), lambda bi, qi: (bi, qi, 0)),
                   pl.BlockSpec((zrows, LANES), lambda bi, qi: (bi * nq + qi, 0))],
        out_shape=[jax.ShapeDtypeStruct((b, s, SB_WIDTH), BF16),
                   jax.ShapeDtypeStruct((zero_rows, LANES), U32)],
        scratch_shapes=([pltpu.VMEM((2, s, SB_WIDTH), BF16)]
                        + [pltpu.VMEM((tq, LANES), F32)] * NPAIR
                        + [pltpu.VMEM((2 * tq, LANES), F32)] * NPAIR
                        + [pltpu.VMEM((2 * tq, tq), F32)] * NPAIR
                        + [pltpu.VMEM((tq, 2 * tq), BF16)] * NPAIR),
        compiler_params=_cparams("parallel", "arbitrary"),
        name="sbattn",
    )(qkv, qkv, qkv, go, u2, bd)


def _hgrn_kernel(q_ref, g_ref, k_ref, v_ref, sg_ref, gn_ref, tl_ref, o_ref, *state_ref):
    s = q_ref.shape[0]
    grp = LANES
    nch = grp // HG_CHUNK
    hs = range(HG_WIDTH // HG_HEAD_DIM)
    for h in hs:
        state_ref[h][...] = jnp.zeros_like(state_ref[h])
    tl = tl_ref[...]
    rr = lax.broadcasted_iota(I32, (grp, grp), 0)
    cc = lax.broadcasted_iota(I32, (grp, grp), 1)
    shift = HG_CHUNK.bit_length() - 1
    rchunk = rr >> shift
    causal = (rchunk == (cc >> shift)) & (cc <= rr)
    inchunk = [rchunk == ci for ci in range(nch)]

    def group(r, c):
        col = lambda h: slice(h * HG_HEAD_DIM, (h + 1) * HG_HEAD_DIM)
        gs = range(HG_GROUPS)
        rows = [pl.ds(pl.multiple_of((r * HG_GROUPS + g) * grp, grp), grp) for g in gs]
        ch = [(g, h) for g in gs for h in hs]
        vs = {c_: v_ref[rows[c_[0]], col(c_[1])] for c_ in ch}
        bs = {c_: sum(jnp.dot(tl, p, preferred_element_type=F32)
                      for p in _bf16_split(g_ref[rows[c_[0]], col(c_[1])], 2))
              for c_ in ch}
        lasts = {c_: [bs[c_][(ci + 1) * HG_CHUNK - 1:(ci + 1) * HG_CHUNK, :] for ci in range(nch)] for c_ in ch}
        bls = {c_: jnp.concatenate([jnp.broadcast_to(l, (HG_CHUNK, HG_HEAD_DIM)) for l in lasts[c_]], axis=0)
               for c_ in ch}
        qes = {c_: (q_ref[rows[c_[0]], col(c_[1])].astype(F32) * jnp.exp(bs[c_])).astype(BF16) for c_ in ch}
        kes = {c_: (k_ref[rows[c_[0]], col(c_[1])] * jnp.exp(-bs[c_])).astype(BF16) for c_ in ch}
        kds = {c_: (k_ref[rows[c_[0]], col(c_[1])] * jnp.exp(bls[c_] - bs[c_])).astype(BF16) for c_ in ch}
        attns = {c_: jnp.where(causal, lax.dot_general(qes[c_], kes[c_], NT_DIMS, preferred_element_type=F32), 0.0)
                 .astype(BF16) for c_ in ch}
        os = {c_: jnp.dot(attns[c_], vs[c_], preferred_element_type=F32) for c_ in ch}
        vts = {c_: vs[c_].astype(F32).T.astype(BF16) for c_ in ch}
        uts = {}
        for c_ in ch:
            zk = jnp.zeros_like(kds[c_])
            kdx = jnp.concatenate([jnp.where(inchunk[ci], kds[c_], zk) for ci in range(nch)], axis=1)
            uts[c_] = jnp.dot(vts[c_], kdx, preferred_element_type=F32)
        for h in hs:
            st = state_ref[h][...]
            for g in gs:
                c_ = (g, h)
                parts = []
                for ci in range(nch):
                    parts.append(st.astype(BF16))
                    st = st * jnp.exp(lasts[c_][ci]) + uts[c_][:, ci * HG_HEAD_DIM:(ci + 1) * HG_HEAD_DIM]
                zq = jnp.zeros_like(qes[c_])
                qx = jnp.concatenate([jnp.where(inchunk[ci], qes[c_], zq) for ci in range(nch)], axis=1)
                o = os[c_] + lax.dot_general(qx, jnp.concatenate(parts, axis=1), NT_DIMS,
                                             preferred_element_type=F32)
                ms = jnp.mean(o * o, axis=-1, keepdims=True)
                y = o * lax.rsqrt(ms + EPS) * gn_ref[:, col(h)]
                o_ref[rows[g], col(h)] = (y * sg_ref[rows[g], col(h)].astype(F32)).astype(BF16)
            state_ref[h][...] = st
        return c

    lax.fori_loop(0, s // (grp * HG_GROUPS), group, 0)


def _hgrn(hq, hg, hk, hi, sg, gn, tl):
    b, s, w = hq.shape
    nh = HG_WIDTH // HG_HEAD_DIM
    blk = pl.BlockSpec((None, s, w), lambda bi: (bi, 0, 0))
    return pl.pallas_call(
        _hgrn_kernel,
        grid=(b,),
        in_specs=[blk, blk, blk, blk, blk,
                  pl.BlockSpec((1, w), lambda bi: (0, 0)),
                  pl.BlockSpec((LANES, LANES), lambda bi: (0, 0))],
        out_specs=blk,
        out_shape=jax.ShapeDtypeStruct((b, s, w), BF16),
        scratch_shapes=[pltpu.VMEM((HG_HEAD_DIM, HG_HEAD_DIM), F32)] * nh,
        compiler_params=_cparams("parallel"),
        name="hgrn",
    )(hq, hg, hk, hi, sg, gn, tl)


def _outproj_kernel(x_ref, a_ref, r_ref, wo_ref, g2_ref, wr_ref, br_ref, tri_ref,
                    x1_ref, hp_ref, ti_ref, tg_ref, rk_ref, cnt_ref, run_ref):
    i = pl.program_id(0)

    @pl.when(i == 0)
    def _():
        run_ref[...] = jnp.zeros_like(run_ref)

    half = a_ref.shape[1]
    x1 = (x_ref[...]
          + jnp.dot(a_ref[...], wo_ref[:half, :], preferred_element_type=F32)
          + jnp.dot(r_ref[...], wo_ref[half:, :], preferred_element_type=F32))
    x1_ref[...] = x1
    ms = jnp.mean(x1 * x1, axis=-1, keepdims=True)
    h2 = x1 * lax.rsqrt(ms + EPS) * g2_ref[...]
    _store_interleaved(hp_ref, _pack_bf16_pairs(h2))

    h_hi, h_lo = _bf16_split(h2, 2)
    w_hi, w_lo = _bf16_split(wr_ref[...], 2)
    hi_both = jnp.dot(h_hi, jnp.concatenate([w_hi, w_lo], axis=1), preferred_element_type=F32)
    logits = ((hi_both[:, :LANES] + jnp.dot(h_lo, w_hi, preferred_element_type=F32))
              + hi_both[:, LANES:]) + br_ref[...]
    tm = logits.shape[0]
    lane = lax.broadcasted_iota(I32, (tm, LANES), 1).astype(F32)
    neg = jnp.float32(-jnp.inf)
    ti = jnp.zeros((tm, LANES), F32)
    tv = jnp.zeros((tm, LANES), F32)
    onehot = jnp.zeros((tm, LANES), F32)
    sels = []
    avail = lane < N_EXPERTS
    for k in range(TOP_K):
        m = jnp.max(jnp.where(avail, logits, neg), axis=-1, keepdims=True)
        idx = jnp.min(jnp.where(avail & (logits == m), lane, float(LANES)), axis=-1, keepdims=True)
        sel = lane == idx
        sels.append(sel)
        ti = jnp.where(lane == k, idx, ti)
        tv = jnp.where(lane == k, m, tv)
        onehot = jnp.where(sel, 1.0, onehot)
        avail = avail & jnp.logical_not(sel)
    valid = lane < TOP_K
    e = jnp.where(valid, jnp.exp(tv - jnp.max(jnp.where(valid, tv, neg), axis=-1, keepdims=True)), 0.0)
    tg_ref[...] = e / jnp.sum(e, axis=-1, keepdims=True)
    ti_ref[...] = ti.astype(I32)

    before = jnp.dot(tri_ref[...], onehot.astype(BF16), preferred_element_type=F32) + run_ref[...]
    rk = jnp.zeros((tm, LANES), F32)
    for k in range(TOP_K):
        rk = jnp.where(lane == k, jnp.sum(jnp.where(sels[k], before, 0.0), axis=-1, keepdims=True), rk)
    rk_ref[...] = rk.astype(I32)
    run_ref[...] += jnp.sum(onehot, axis=0, keepdims=True)
    cnt_ref[...] = run_ref[...]


def _outproj(x2, a2, r2, wo_bf, g2, wr, br, tri, *, tm):
    t, d = x2.shape
    half = a2.shape[1]
    row = lambda i: (i, 0)
    fix = lambda i: (0, 0)
    return pl.pallas_call(
        _outproj_kernel,
        grid=(t // tm,),
        in_specs=[
            pl.BlockSpec((tm, d), row),
            pl.BlockSpec((tm, half), row),
            pl.BlockSpec((tm, half), row),
            pl.BlockSpec((2 * half, d), fix),
            pl.BlockSpec((1, d), fix),
            pl.BlockSpec((d, LANES), fix),
            pl.BlockSpec((1, LANES), fix),
            pl.BlockSpec((tm, tm), fix),
        ],
        out_specs=[
            pl.BlockSpec((tm, d), row),
            pl.BlockSpec((ROW_GROUPS * tm, LANES), row),
            pl.BlockSpec((tm, LANES), row),
            pl.BlockSpec((tm, LANES), row),
            pl.BlockSpec((tm, LANES), row),
            pl.BlockSpec((1, LANES), fix),
        ],
        out_shape=[
            jax.ShapeDtypeStruct((t, d), F32),
            jax.ShapeDtypeStruct((ROW_GROUPS * t, LANES), U32),
            jax.ShapeDtypeStruct((t, LANES), I32),
            jax.ShapeDtypeStruct((t, LANES), F32),
            jax.ShapeDtypeStruct((t, LANES), I32),
            jax.ShapeDtypeStruct((1, LANES), F32),
        ],
        scratch_shapes=[pltpu.VMEM((1, LANES), F32)],
        compiler_params=_cparams("arbitrary"),
        name="outproj",
    )(x2, a2, r2, wo_bf, g2, wr, br, tri)


def _slab(ref, r):
    return ref.at[pl.ds(pl.multiple_of(ROW_GROUPS * r, ROW_GROUPS), ROW_GROUPS)]


def _slab_copy_wait(src, dst, sem, n):
    pltpu.make_async_copy(src.at[pl.ds(0, ROW_GROUPS * n)], dst.at[pl.ds(0, ROW_GROUPS * n)], sem).wait()


def _scatter_kernel(dest_ref, hp_ref, xz_ref, xb_ref, sem):
    del xz_ref
    tm = hp_ref.shape[0] // ROW_GROUPS
    for t in range(tm):
        for k in range(TOP_K):
            pltpu.make_async_copy(_slab(hp_ref, t), _slab(xb_ref, dest_ref[0, t * TOP_K + k]),
                                  sem).start(priority=k % 2)
    for _ in range(TOP_K):
        _slab_copy_wait(hp_ref, xb_ref, sem, tm)


def _scatter(dest2, hp, xzero, *, tm):
    t = hp.shape[0] // ROW_GROUPS
    return pl.pallas_call(
        _scatter_kernel,
        grid=(t // tm,),
        in_specs=[
            pl.BlockSpec((None, 1, tm * TOP_K), lambda i: (i, 0, 0), memory_space=pltpu.SMEM),
            pl.BlockSpec((ROW_GROUPS * tm, LANES), lambda i: (i, 0)),
            pl.BlockSpec(memory_space=pl.ANY),
        ],
        out_specs=pl.BlockSpec(memory_space=pl.ANY),
        out_shape=jax.ShapeDtypeStruct(xzero.shape, xzero.dtype),
        scratch_shapes=[pltpu.SemaphoreType.DMA],
        input_output_aliases={2: 0},
        compiler_params=_cparams("arbitrary"),
        name="scatter",
    )(dest2, hp, xzero)


def _experts_kernel(be_ref, nu_ref, xb_ref, wgu_ref, bgu_ref, wd_ref, bd_ref, yb_ref, wgu_bf, wd_bf):
    i = pl.program_id(0)
    fresh = jnp.logical_or(i == 0, be_ref[i] != be_ref[jnp.maximum(i - 1, 0)])

    @pl.when(fresh)
    def _():
        wgu_bf[...] = wgu_ref[...].astype(BF16)
        wd_bf[...] = wd_ref[...].astype(BF16)

    @pl.when(i < nu_ref[0])
    def _():
        x_lo, x_hi = _unpack_bf16_pairs(_load_interleaved(xb_ref))
        x = jnp.concatenate([x_lo.astype(BF16), x_hi.astype(BF16)], axis=1)
        gu = jnp.dot(x, wgu_bf[...], preferred_element_type=F32) + bgu_ref[...]
        f = gu.shape[1] // 2
        g_lin = jnp.minimum(gu[:, :f], SWIGLU_LIMIT)
        u_lin = jnp.clip(gu[:, f:], -SWIGLU_LIMIT, SWIGLU_LIMIT)
        act = g_lin * _sigmoid(SWIGLU_ALPHA * g_lin) * (u_lin + 1.0)
        y = jnp.dot(act.astype(BF16), wd_bf[...], preferred_element_type=F32) + bd_ref[...]
        _store_interleaved(yb_ref, _pack_bf16_pairs(y))

    @pl.when(i >= nu_ref[0])
    def _():
        yb_ref[...] = jnp.zeros_like(yb_ref)


def _experts(block_e, n_used, xb, w_gate_up, b_gate_up3, w_down, b_down3, *, rb):
    n_rows = xb.shape[0] // ROW_GROUPS
    d = w_gate_up.shape[1]
    f2 = w_gate_up.shape[2]
    f = w_down.shape[1]
    nblk = n_rows // rb
    rowblk = lambda i, be, nu: (jnp.minimum(i, nu[0] - 1), 0)
    wsel = lambda i, be, nu: (be[i], 0, 0)
    grid_spec = pltpu.PrefetchScalarGridSpec(
        num_scalar_prefetch=2,
        grid=(nblk,),
        in_specs=[
            pl.BlockSpec((ROW_GROUPS * rb, LANES), rowblk),
            pl.BlockSpec((None, d, f2), wsel),
            pl.BlockSpec((None, 1, f2), wsel),
            pl.BlockSpec((None, f, d), wsel),
            pl.BlockSpec((None, 1, d), wsel),
        ],
        out_specs=pl.BlockSpec((ROW_GROUPS * rb, LANES), lambda i, be, nu: (i, 0)),
        scratch_shapes=[pltpu.VMEM((d, f2), BF16), pltpu.VMEM((f, d), BF16)],
    )
    return pl.pallas_call(
        _experts_kernel,
        grid_spec=grid_spec,
        out_shape=jax.ShapeDtypeStruct(xb.shape, U32),
        compiler_params=_cparams("arbitrary"),
        name="experts",
    )(block_e, n_used, xb, w_gate_up, b_gate_up3, w_down, b_down3)


def _combine_kernel(dest0_ref, destn_ref, x1_ref, tg_ref, yb_ref, o_ref, buf, sem):
    i = pl.program_id(0)
    n = pl.num_programs(0)
    tm = x1_ref.shape[0]
    slot = i % 2

    def gather_one(dest_ref, s, t):
        for k in range(TOP_K):
            pltpu.make_async_copy(_slab(yb_ref, dest_ref[0, t * TOP_K + k]), _slab(buf.at[s, k], t),
                                  sem.at[s]).start(priority=k % 2)

    @pl.when(i == 0)
    def _():
        def body(t, c):
            gather_one(dest0_ref, 0, t)
            return c
        lax.fori_loop(0, tm, body, 0)

    @pl.when(i + 1 < n)
    def _():
        for t in range(tm):
            gather_one(destn_ref, 1 - slot, t)

    for k in range(TOP_K):
        _slab_copy_wait(yb_ref, buf.at[slot, k], sem.at[slot], tm)
    tg = tg_ref[...]
    half = x1_ref.shape[1] // 2
    for j in range(ROW_GROUPS):
        lo_cols = slice(j * LANES, (j + 1) * LANES)
        hi_cols = slice(half + j * LANES, half + (j + 1) * LANES)
        out_lo = x1_ref[:, lo_cols]
        out_hi = x1_ref[:, hi_cols]
        for k in range(TOP_K):
            y_lo, y_hi = _unpack_bf16_pairs(buf[slot, k, pl.ds(j, tm, stride=ROW_GROUPS), :])
            out_lo = out_lo + tg[:, k:k + 1] * y_lo
            out_hi = out_hi + tg[:, k:k + 1] * y_hi
        o_ref[:, lo_cols] = out_lo
        o_ref[:, hi_cols] = out_hi


def _combine(dest2, x1, tg, yb, *, tm):
    t, d = x1.shape
    n = t // tm
    return pl.pallas_call(
        _combine_kernel,
        grid=(n,),
        in_specs=[
            pl.BlockSpec((None, 1, tm * TOP_K), lambda i: (0, 0, 0), memory_space=pltpu.SMEM),
            pl.BlockSpec((None, 1, tm * TOP_K), lambda i: (jnp.minimum(i + 1, n - 1), 0, 0),
                         memory_space=pltpu.SMEM),
            pl.BlockSpec((tm, d), lambda i: (i, 0)),
            pl.BlockSpec((tm, LANES), lambda i: (i, 0)),
            pl.BlockSpec(memory_space=pl.ANY),
        ],
        out_specs=pl.BlockSpec((tm, d), lambda i: (i, 0)),
        out_shape=jax.ShapeDtypeStruct((t, d), F32),
        scratch_shapes=[pltpu.VMEM((2, TOP_K, ROW_GROUPS * tm, LANES), yb.dtype), pltpu.SemaphoreType.DMA((2,))],
        compiler_params=_cparams("arbitrary"),
        name="combine",
    )(dest2, dest2, x1, tg, yb)


def _block_ones(n, blk):
    i = jnp.arange(n)
    return (i[:, None] // blk == i[None, :] // blk).astype(BF16)


def _pair_suffix(tq):
    j = jnp.arange(2 * tq)
    return ((j[:, None] > j[None, :]) & (j[:, None] // tq == j[None, :] // tq)).astype(BF16)


def _chunk_tril(n, blk):
    i = jnp.arange(n)
    return ((i[:, None] // blk == i[None, :] // blk) & (i[None, :] <= i[:, None])).astype(BF16)


def _strict_tril(n):
    i = jnp.arange(n)
    return (i[None, :] < i[:, None]).astype(BF16)


def _layer(x, norm1_g, w_in, q_g, k_g, sbo_g, lb, hgo_g, w_out, norm2_g, w_router, b_router,
           w_gate_up, b_gate_up, w_down, b_down):
    bsz, seq, d = x.shape
    rb = RB_EXPERTS
    t = bsz * seq
    x2 = x.reshape(t, d)
    nsb = SB_WIDTH // SB_HEAD_DIM

    qkv, hq, hg, hk, hi, sg = _inproj(
        x2, norm1_g.reshape(1, d), w_in.astype(BF16),
        jnp.tile(q_g, nsb).reshape(1, SB_WIDTH), jnp.tile(k_g, nsb).reshape(1, SB_WIDTH),
        lb.reshape(1, HG_WIDTH), _block_ones(MXU_TILE, SB_HEAD_DIM), tm=TM_INPROJ)

    n_assign = t * TOP_K
    nblk = n_assign // rb + N_EXPERTS
    a, xzero = _sbattn(qkv.reshape(bsz, seq, 3 * SB_WIDTH), sbo_g.reshape(1, SB_WIDTH),
                       _pair_suffix(TQ_ATTN), _block_ones(LANES, SB_HEAD_DIM), tq=TQ_ATTN,
                       zero_rows=ROW_GROUPS * nblk * rb)
    sh = (bsz, seq, HG_WIDTH)
    r = _hgrn(hq.reshape(sh), hg.reshape(sh), hk.reshape(sh), hi.reshape(sh), sg.reshape(sh),
              hgo_g.reshape(1, HG_WIDTH), _chunk_tril(LANES, HG_CHUNK))

    wr = jnp.zeros((d, LANES), F32).at[:, :N_EXPERTS].set(w_router)
    br = jnp.zeros((1, LANES), F32).at[0, :N_EXPERTS].set(b_router)
    x1, hp, ti, tg, rk, cnt = _outproj(
        x2, a.reshape(t, SB_WIDTH), r.reshape(t, HG_WIDTH), w_out.astype(BF16),
        norm2_g.reshape(1, d), wr, br, _strict_tril(TM_OUTPROJ), tm=TM_OUTPROJ)

    counts = cnt[0, :N_EXPERTS].astype(I32)
    padded = (counts + rb - 1) // rb * rb
    pad_ends = jnp.cumsum(padded)
    pad_starts = pad_ends - padded
    eids = jnp.arange(N_EXPERTS, dtype=I32)
    dest = rk[:, :TOP_K] + jnp.sum(jnp.where(ti[:, :TOP_K, None] == eids, pad_starts, 0), axis=-1)
    n_used = (pad_ends[-1] // rb).astype(I32).reshape(1)
    blk_start = jnp.minimum(jnp.arange(nblk, dtype=I32), n_used[0] - 1) * rb
    block_e = jnp.sum((blk_start[:, None] >= pad_ends[None, :]).astype(I32), axis=1)

    xb = _scatter(dest.reshape(t // TM_SCATTER, 1, TM_SCATTER * TOP_K), hp, xzero, tm=TM_SCATTER)
    yb = _experts(block_e, n_used, xb, w_gate_up, b_gate_up.reshape(N_EXPERTS, 1, -1),
                  w_down, b_down.reshape(N_EXPERTS, 1, -1), rb=rb)
    out = _combine(dest.reshape(t // TM_COMBINE, 1, TM_COMBINE * TOP_K), x1, tg, yb, tm=TM_COMBINE)
    return out.reshape(bsz, seq, d)


def kernel(x, norm1_g, w_in, sb_q_norm_g, sb_k_norm_g, sb_out_norm_g, hg_lb_logits, hg_out_norm_g,
           w_out, norm2_g, w_router, b_router, w_gate_up, b_gate_up, w_down, b_down):
    depth = w_in.shape[0]
    lb_all = jnp.cumsum(jax.nn.softmax(hg_lb_logits.astype(F32), axis=0), axis=0)
    for l in range(depth):
        x = _layer(x, norm1_g[l], w_in[l], sb_q_norm_g[l], sb_k_norm_g[l], sb_out_norm_g[l], lb_all[l],
                   hg_out_norm_g[l], w_out[l], norm2_g[l], w_router[l], b_router[l],
                   w_gate_up[l], b_gate_up[l], w_down[l], b_down[l])
    return x
```
